```python
import math
import jax, jax.numpy as jnp
from jax import lax
import numpy as np

D_MODEL = 2048
BATCH = 4
SEQ = 2048
DEPTH = 1
DEC_BATCH = 128
DEC_SEQ = 1
PAST_LEN = 16384
PAGE_SIZE = 128

D_A = D_MODEL
HEAD_A = 64
N_HEADS_A = D_A // HEAD_A
R_DECAY = 96
R_AAA = 96
R_GATE = 256
D_SHIFT = 3 * D_A + R_DECAY + R_AAA + R_GATE
D_B = D_MODEL // 2
CHUNK = 128
GROUP_B = 128
N_GROUPS_B = D_B // GROUP_B
D_IN = D_SHIFT + 2 * D_B + 2 * D_MODEL
PEER_HEADS = 8
N_KEYS = 128
N_EXPERTS = N_KEYS * N_KEYS
DK_HALF = 128
PEER_TOPK = 16
PEER_BLOCK = 128
ALPHA = (2 * DEPTH) ** 0.25
BETA = (8 * DEPTH) ** -0.25
LN_EPS = 1e-5
GN_EPS = 64e-5

kernel_name = 'rwkv7_gmlp_peer_hybrid_step'


def _layer_norm(x, g, b, eps=LN_EPS):
    xf = x.astype(jnp.float32)
    mu = jnp.mean(xf, axis=-1, keepdims=True)
    var = jnp.mean(jnp.square(xf - mu), axis=-1, keepdims=True)
    return ((xf - mu) * lax.rsqrt(var + eps) * g + b).astype(x.dtype)


def _rwkv7(xs, wkv0, lp):
    Bn, T, _ = xs.shape
    f32 = jnp.float32
    r, k, v, wl, al, gl = jnp.split(xs.astype(f32), [D_A, 2 * D_A, 3 * D_A, 3 * D_A + R_DECAY, 3 * D_A + R_DECAY + R_AAA], axis=-1)
    w = -jax.nn.softplus(-(lp['w0'] + jnp.tanh(wl) @ lp['w2'])) - 0.5
    decay = jnp.exp(-jnp.exp(w))
    a = jax.nn.sigmoid(lp['a0'] + al @ lp['a2'])
    g = jax.nn.sigmoid(gl) @ lp['g2']
    hs = lambda t: t.reshape(Bn, T, N_HEADS_A, HEAD_A)
    kk = hs(k * lp['k_k'])
    kk = kk / jnp.maximum(jnp.sqrt(jnp.sum(kk * kk, axis=-1, keepdims=True)), 1e-12)
    k = k * (1.0 + (a - 1.0) * lp['k_a'])
    r_h, k_h, v_h, w_h, a_h = hs(r), hs(k), hs(v), hs(decay), hs(a)
    aa = -kk
    bb = kk * a_h

    def step(S, inp):
        r_t, w_t, k_t, v_t, aa_t, bb_t = inp
        sa = jnp.einsum('bhvk,bhk->bhv', S, aa_t)
        S = S * w_t[:, :, None, :] + sa[..., None] * bb_t[:, :, None, :] + v_t[..., None] * k_t[:, :, None, :]
        return S, jnp.einsum('bhvk,bhk->bhv', S, r_t)

    seq = (jnp.moveaxis(r_h, 1, 0), jnp.moveaxis(w_h, 1, 0), jnp.moveaxis(k_h, 1, 0),
           jnp.moveaxis(v_h, 1, 0), jnp.moveaxis(aa, 1, 0), jnp.moveaxis(bb, 1, 0))
    S_T, y = lax.scan(step, wkv0.astype(f32), seq)
    y = jnp.moveaxis(y, 0, 1)
    mu = jnp.mean(y, axis=-1, keepdims=True)
    var = jnp.mean(jnp.square(y - mu), axis=-1, keepdims=True)
    y = ((y - mu) * lax.rsqrt(var + GN_EPS)).reshape(Bn, T, D_A) * lp['lnx_g'] + lp['lnx_b']
    bonus = jnp.sum(r_h * k_h * lp['r_k'], axis=-1, keepdims=True) * v_h
    y = (y + bonus.reshape(Bn, T, D_A)) * g
    return y.astype(xs.dtype), S_T.astype(wkv0.dtype)


def _chunk_sgu(p_u, p_v, lp):
    Bn, T, _ = p_u.shape
    u = jax.nn.gelu(p_u)
    v = _layer_norm(jax.nn.gelu(p_v), lp['sgu_ln_g'], lp['sgu_ln_b'])
    pad = (-T) % CHUNK
    vp = jnp.pad(v, ((0, 0), (0, pad), (0, 0))).reshape(Bn, -1, CHUNK, N_GROUPS_B, GROUP_B)
    mask = jnp.tril(jnp.ones((CHUNK, CHUNK), dtype=bool))
    ws = jnp.where(mask[None], lp['sgu_w'], 0)
    mixed = jnp.einsum('gts,bnsgd->bntgd', ws, vp) + lp['sgu_b'].T[:, :, None]
    mixed = mixed.reshape(Bn, -1, D_B)[:, :T]
    return u * mixed, v


def _peer(h, lp):
    n = h.shape[0]
    pad = (-n) % PEER_BLOCK
    hp = jnp.pad(h, ((0, pad), (0, 0))).reshape(-1, PEER_BLOCK, D_MODEL)
    w_q, keys, u_tab, v_tab = lp['peer_wq'], lp['peer_keys'], lp['peer_u'], lp['peer_v']

    def block(hb):
        q = (hb @ w_q).reshape(PEER_BLOCK, PEER_HEADS, 2, DK_HALF).astype(jnp.float32)
        s = jnp.einsum('thpd,hpkd->thpk', q, keys.astype(jnp.float32))
        s1, i1 = lax.top_k(s[:, :, 0], PEER_TOPK)
        s2, i2 = lax.top_k(s[:, :, 1], PEER_TOPK)
        cand = (s1[..., :, None] + s2[..., None, :]).reshape(PEER_BLOCK, PEER_HEADS, PEER_TOPK * PEER_TOPK)
        cidx = (i1[..., :, None] * N_KEYS + i2[..., None, :]).reshape(PEER_BLOCK, PEER_HEADS, PEER_TOPK * PEER_TOPK)
        top, pos = lax.top_k(cand, PEER_TOPK)
        eidx = jnp.take_along_axis(cidx, pos, axis=-1)
        gate = jax.nn.softmax(top, axis=-1)
        act = jax.nn.gelu(jnp.einsum('thkd,td->thk', u_tab[eidx], hb).astype(jnp.float32)) * gate
        return jnp.einsum('thk,thkd->td', act.astype(hb.dtype), v_tab[eidx])

    return lax.map(block, hp).reshape(-1, D_MODEL)[:n]


def _layer(x, c, wkv0, shift0, lp):
    Bn, T, _ = x.shape
    mod = jax.nn.silu(c) @ lp['w_ada'] + lp['b_ada']
    sh1, sc1, gt1, sh2, sc2, gt2 = jnp.split(mod[:, None, :], 6, axis=-1)
    h = x * (1.0 + sc1) + sh1
    proj = h @ lp['w_in']
    o1 = D_SHIFT
    o2 = o1 + D_B
    o3 = o2 + D_B
    o4 = o3 + D_MODEL
    p_shift, p_u, p_v, p_ga, p_gb = jnp.split(proj, [o1, o2, o3, o4], axis=-1)
    prev = jnp.concatenate([shift0[:, None, :].astype(proj.dtype), p_shift[:, :-1]], axis=1)
    xs = p_shift + (prev - p_shift) * lp['mu_shift']
    y_a, wkv_T = _rwkv7(xs, wkv0, lp)
    y_b, v_rows = _chunk_sgu(p_u, p_v, lp)
    merged = jax.nn.sigmoid(p_ga) * (y_a @ lp['w_br_a']) + jax.nn.sigmoid(p_gb) * (y_b @ lp['w_br_b'])
    x1 = _layer_norm(ALPHA * x + gt1 * (merged @ lp['w_o']), lp['ln1_g'], lp['ln1_b'])
    h2 = x1 * (1.0 + sc2) + sh2
    f = _peer(h2.reshape(Bn * T, D_MODEL), lp).reshape(Bn, T, D_MODEL)
    x2 = _layer_norm(ALPHA * x1 + gt2 * f, lp['ln2_g'], lp['ln2_b'])
    return x2, wkv_T, p_shift[:, -1], v_rows


def setup_inputs(seed: int = 0) -> dict:
    key = jax.random.key(seed)
    ks = jax.random.split(key, 40)
    f32 = jnp.float32
    nrm = lambda k, shape, s: jax.random.normal(k, shape, f32) * s
    L = DEPTH
    return {
        'x_prompt': nrm(ks[0], (BATCH, SEQ, D_MODEL), 1.0),
        'x_sample': nrm(ks[1], (DEC_BATCH, DEC_SEQ, D_MODEL), 1.0),
        'c_prompt': nrm(ks[2], (BATCH, D_MODEL), 1.0),
        'c_sample': nrm(ks[3], (DEC_BATCH, D_MODEL), 1.0),
        'state_wkv': nrm(ks[4], (L, DEC_BATCH, N_HEADS_A, HEAD_A, HEAD_A), 0.3),
        'state_shift': nrm(ks[5], (L, DEC_BATCH, D_SHIFT), 1.0),
        'w_ada': nrm(ks[6], (L, D_MODEL, 6 * D_MODEL), 0.5 * D_MODEL ** -0.5),
        'b_ada': nrm(ks[7], (L, 6 * D_MODEL), 0.02),
        'w_in': nrm(ks[8], (L, D_MODEL, D_IN), D_MODEL ** -0.5),
        'mu_shift': jax.random.uniform(ks[9], (L, D_SHIFT), f32),
        'w0': -1.0 + nrm(ks[10], (L, D_A), 0.5),
        'w2': nrm(ks[11], (L, R_DECAY, D_A), 0.1 * R_DECAY ** -0.5),
        'a0': nrm(ks[12], (L, D_A), 0.1),
        'a2': nrm(ks[13], (L, R_AAA, D_A), 0.1 * R_AAA ** -0.5),
        'g2': nrm(ks[14], (L, R_GATE, D_A), R_GATE ** -0.5),
        'k_k': 0.85 + nrm(ks[15], (L, D_A), 0.05),
        'k_a': 1.0 + nrm(ks[16], (L, D_A), 0.05),
        'r_k': nrm(ks[17], (L, N_HEADS_A, HEAD_A), 0.1),
        'lnx_g': 1.0 + nrm(ks[18], (L, D_A), 0.02),
        'lnx_b': nrm(ks[19], (L, D_A), 0.02),
        'sgu_ln_g': 1.0 + nrm(ks[20], (L, D_B), 0.02),
        'sgu_ln_b': nrm(ks[21], (L, D_B), 0.02),
        'sgu_w': nrm(ks[22], (L, N_GROUPS_B, CHUNK, CHUNK), 0.5 * CHUNK ** -0.5),
        'sgu_b': 1.0 + nrm(ks[23], (L, N_GROUPS_B, CHUNK), 0.01),
        'w_br_a': nrm(ks[24], (L, D_A, D_MODEL), D_A ** -0.5),
        'w_br_b': nrm(ks[25], (L, D_B, D_MODEL), D_B ** -0.5),
        'w_o': nrm(ks[26], (L, D_MODEL, D_MODEL), BETA * D_MODEL ** -0.5),
        'ln1_g': 1.0 + nrm(ks[27], (L, D_MODEL), 0.02),
        'ln1_b': nrm(ks[28], (L, D_MODEL), 0.02),
        'peer_wq': nrm(ks[29], (L, D_MODEL, PEER_HEADS * 2 * DK_HALF), D_MODEL ** -0.5),
        'peer_keys': nrm(ks[30], (L, PEER_HEADS, 2, N_KEYS, DK_HALF), DK_HALF ** -0.5),
        'peer_u': nrm(ks[31], (L, N_EXPERTS, D_MODEL), D_MODEL ** -0.5),
        'peer_v': nrm(ks[32], (L, N_EXPERTS, D_MODEL), BETA),
        'ln2_g': 1.0 + nrm(ks[33], (L, D_MODEL), 0.02),
        'ln2_b': nrm(ks[34], (L, D_MODEL), 0.02),
    }


def reference(x_prompt, x_sample, c_prompt, c_sample, state_wkv, state_shift,
              w_ada, b_ada, w_in, mu_shift, w0, w2, a0, a2, g2, k_k, k_a, r_k, lnx_g, lnx_b,
              sgu_ln_g, sgu_ln_b, sgu_w, sgu_b, w_br_a, w_br_b, w_o, ln1_g, ln1_b,
              peer_wq, peer_keys, peer_u, peer_v, ln2_g, ln2_b):
    y_p, y_s = x_prompt, x_sample
    wkv_p_l, sh_p_l, wkv_s_l, sh_s_l, vr_s_l = [], [], [], [], []
    for l in range(DEPTH):
        lp = {
            'w_ada': w_ada[l], 'b_ada': b_ada[l], 'w_in': w_in[l], 'mu_shift': mu_shift[l],
            'w0': w0[l], 'w2': w2[l], 'a0': a0[l], 'a2': a2[l], 'g2': g2[l],
            'k_k': k_k[l], 'k_a': k_a[l], 'r_k': r_k[l], 'lnx_g': lnx_g[l], 'lnx_b': lnx_b[l],
            'sgu_ln_g': sgu_ln_g[l], 'sgu_ln_b': sgu_ln_b[l], 'sgu_w': sgu_w[l], 'sgu_b': sgu_b[l],
            'w_br_a': w_br_a[l], 'w_br_b': w_br_b[l], 'w_o': w_o[l], 'ln1_g': ln1_g[l], 'ln1_b': ln1_b[l],
            'peer_wq': peer_wq[l], 'peer_keys': peer_keys[l], 'peer_u': peer_u[l], 'peer_v': peer_v[l],
            'ln2_g': ln2_g[l], 'ln2_b': ln2_b[l],
        }
        wkv0_p = jnp.zeros((y_p.shape[0], N_HEADS_A, HEAD_A, HEAD_A), x_prompt.dtype)
        shift0_p = jnp.zeros((y_p.shape[0], D_SHIFT), x_prompt.dtype)
        y_p, wkv_p, sh_p, _ = _layer(y_p, c_prompt, wkv0_p, shift0_p, lp)
        y_s, wkv_s, sh_s, vr_s = _layer(y_s, c_sample, state_wkv[l], state_shift[l], lp)
        wkv_p_l.append(wkv_p)
        sh_p_l.append(sh_p)
        wkv_s_l.append(wkv_s)
        sh_s_l.append(sh_s)
        vr_s_l.append(vr_s)
    new_wkv_prompt = jnp.stack(wkv_p_l)
    new_shift_prompt = jnp.stack(sh_p_l)
    new_wkv_sample = jnp.stack(wkv_s_l)
    new_shift_sample = jnp.stack(sh_s_l)
    new_vrows_sample = jnp.stack(vr_s_l)
    return (y_p, y_s, new_wkv_prompt, new_shift_prompt, new_wkv_sample, new_shift_sample, new_vrows_sample)
```

```python
import functools

import jax
import jax.numpy as jnp
from jax import lax
from jax.experimental import pallas as pl
from jax.experimental.pallas import tpu as pltpu

F32 = jnp.float32
BF16 = jnp.bfloat16

HEAD_A = 64
R_DECAY = 96
R_AAA = 96
R_GATE = 256
SMALL = R_DECAY + R_AAA + R_GATE
SMALL_PAD = 512
CHUNK = 128
GROUP_B = 128
PEER_HEADS = 8
N_KEYS = 128
DK_HALF = 128
TOPK = 16
LN_EPS = 1e-5
GN_EPS = 64e-5

LANES = 128
SUBLANES = 8
VMEM_LIMIT = 56 * 1024 * 1024


def _cparams(*sem):
    return pltpu.CompilerParams(dimension_semantics=sem, vmem_limit_bytes=VMEM_LIMIT)


def _gelu(x):
    return jax.nn.gelu(x, approximate=True)


def _layer_norm_rows(x, g, b, eps):
    mu = jnp.mean(x, axis=-1, keepdims=True)
    xc = x - mu
    var = jnp.mean(xc * xc, axis=-1, keepdims=True)
    return xc * lax.rsqrt(var + eps) * g + b


def _pick(n, prefs):
    for p in prefs:
        if n % p == 0:
            return p
    return n


def _mod_spec(per_row, tm, d, chunk, rows_per_seq):
    if per_row:
        return pl.BlockSpec((tm, d), lambda i, *_: (i, chunk))
    return pl.BlockSpec((None, 1, d), lambda i, *_: ((i * tm) // rows_per_seq, 0, chunk))


def _mod_kernel(c_ref, w_ref, b_ref, o_ref):
    c = c_ref[...]
    s = (c * jax.nn.sigmoid(c)).astype(BF16)
    o_ref[...] = jnp.dot(s, w_ref[...].astype(BF16), preferred_element_type=F32) + b_ref[...]


def _modulation(c_all, w_ada, b_ada):
    m, d = c_all.shape
    n = w_ada.shape[1]
    tn = _pick(n, (1536, 512))
    return pl.pallas_call(
        _mod_kernel,
        grid=(n // tn,),
        in_specs=[pl.BlockSpec((m, d), lambda j: (0, 0)),
                  pl.BlockSpec((d, tn), lambda j: (0, j)),
                  pl.BlockSpec((1, tn), lambda j: (0, j))],
        out_specs=pl.BlockSpec((m, tn), lambda j: (0, j)),
        out_shape=jax.ShapeDtypeStruct((m, n), F32),
        compiler_params=_cparams("arbitrary"),
        name="adaln_mod",
    )(c_all, w_ada, b_ada.reshape(1, n))


def _inproj_kernel(x_ref, sh_ref, sc_ref, w_ref, o_ref, h_scr):
    @pl.when(pl.program_id(1) == 0)
    def _():
        h_scr[...] = (x_ref[...] * (1.0 + sc_ref[...]) + sh_ref[...]).astype(BF16)

    o_ref[...] = jnp.dot(h_scr[...], w_ref[...], preferred_element_type=F32)


def _inproj(x, mod, w_pack, per_row, seq_len, tm, tn):
    m, d = x.shape
    n = w_pack.shape[1]
    return pl.pallas_call(
        _inproj_kernel,
        grid=(m // tm, n // tn),
        in_specs=[pl.BlockSpec((tm, d), lambda i, j: (i, 0)),
                  _mod_spec(per_row, tm, d, 0, seq_len),
                  _mod_spec(per_row, tm, d, 1, seq_len),
                  pl.BlockSpec((d, tn), lambda i, j: (0, j))],
        out_specs=pl.BlockSpec((tm, tn), lambda i, j: (i, j)),
        out_shape=jax.ShapeDtypeStruct((m, n), F32),
        scratch_shapes=[pltpu.VMEM((tm, d), BF16)],
        compiler_params=_cparams("arbitrary", "arbitrary"),
        name="in_proj",
    )(x, mod, mod, w_pack)


def _rwkv_pre_kernel(pm_ref, ps_ref, qm_ref, qs_ref, stm_ref, sts_ref, mum_ref, mus_ref,
                     w0_ref, w2_ref, a0_ref, a2_ref, g2_ref, kk_ref, ka_ref,
                     r_o, w_o, k_o, v_o, kk_o, a_o, g_o, *, d, per_row_state, tiles_per_seq):
    def shifted(p, q, st):
        if per_row_state:
            return st
        first = (pl.program_id(0) % tiles_per_seq) == 0
        prev_row = jnp.where(first, st, q[SUBLANES - 1:SUBLANES, :])
        row = lax.broadcasted_iota(jnp.int32, p.shape, 0)
        return jnp.where(row == 0, prev_row, pltpu.roll(p, 1, 0))

    def mix(p, q, st, mu):
        return p + (shifted(p, q, st) - p) * mu

    xs_s = mix(ps_ref[...], qs_ref[...], sts_ref[...], mus_ref[...])
    wl = xs_s[:, 0:R_DECAY]
    al = xs_s[:, R_DECAY:R_DECAY + R_AAA]
    gl = xs_s[:, R_DECAY + R_AAA:SMALL]
    wlin = w0_ref[...] + jnp.dot(jnp.tanh(wl).astype(BF16), w2_ref[...].astype(BF16),
                                 preferred_element_type=F32)
    w = -jax.nn.softplus(-wlin) - 0.5
    w_o[...] = jnp.exp(-jnp.exp(w))
    a = jax.nn.sigmoid(a0_ref[...] + jnp.dot(al.astype(BF16), a2_ref[...].astype(BF16),
                                             preferred_element_type=F32))
    a_o[...] = a
    g_o[...] = jnp.dot(jax.nn.sigmoid(gl).astype(BF16), g2_ref[...].astype(BF16),
                       preferred_element_type=F32)

    def part(n):
        cols = slice(n * d, (n + 1) * d)
        return mix(pm_ref[:, cols], qm_ref[:, cols], stm_ref[:, cols], mum_ref[:, cols])

    r_o[...] = part(0)
    k = part(1)
    kk_o[...] = k * kk_ref[...]
    k_o[...] = k * (1.0 + (a - 1.0) * ka_ref[...])
    v_o[...] = part(2)


def _rwkv_pre(proj, state_main, state_small, lp, d, seq_len, tm):
    m = proj.shape[0]
    per_row_state = seq_len == 1
    tiles_per_seq = max(seq_len // tm, 1)
    small_blk = (6 * d) // SMALL_PAD

    def prev_map(col):
        def f(i):
            return (jnp.maximum(i * (tm // SUBLANES) - 1, 0), col)
        return f

    if per_row_state:
        stm_spec = pl.BlockSpec((tm, 3 * d), lambda i: (i, 0))
        sts_spec = pl.BlockSpec((tm, SMALL_PAD), lambda i: (i, 0))
    else:
        stm_spec = pl.BlockSpec((None, 1, 3 * d), lambda i: (i // tiles_per_seq, 0, 0))
        sts_spec = pl.BlockSpec((None, 1, SMALL_PAD), lambda i: (i // tiles_per_seq, 0, 0))

    vec = lambda w: pl.BlockSpec((1, w), lambda i: (0, 0))
    full = lambda a: pl.BlockSpec(a.shape, lambda i: (0,) * a.ndim)
    out = pl.BlockSpec((tm, d), lambda i: (i, 0))
    kern = functools.partial(_rwkv_pre_kernel, d=d, per_row_state=per_row_state,
                             tiles_per_seq=tiles_per_seq)
    in_specs = [pl.BlockSpec((tm, 3 * d), lambda i: (i, 0)),
                pl.BlockSpec((tm, SMALL_PAD), lambda i: (i, small_blk)),
                pl.BlockSpec((SUBLANES, 3 * d), prev_map(0)),
                pl.BlockSpec((SUBLANES, SMALL_PAD), prev_map(small_blk)),
                stm_spec, sts_spec, vec(3 * d), vec(SMALL_PAD), vec(d), full(lp["w2"]), vec(d),
                full(lp["a2"]), full(lp["g2"]), vec(d), vec(d)]
    return pl.pallas_call(
        kern,
        grid=(m // tm,),
        in_specs=in_specs,
        out_specs=[out] * 7,
        out_shape=[jax.ShapeDtypeStruct((m, d), F32)] * 7,
        compiler_params=_cparams("arbitrary"),
        name="rwkv_pre",
    )(proj, proj, proj, proj, state_main, state_small, lp["mu_main"], lp["mu_small"], lp["w0"],
      lp["w2"], lp["a0"], lp["a2"], lp["g2"], lp["k_k"], lp["k_a"])


def _scan_kernel(r_ref, w_ref, k_ref, v_ref, kk_ref, a_ref, s0_ref, lng_ref, lnb_ref, rk_ref,
                 y_ref, st_ref, s_scr, aa_scr, bb_scr, *, tc):
    n = HEAD_A
    c = pl.program_id(1)

    @pl.when(c == 0)
    def _():
        s_scr[...] = s0_ref[...]

    kk = kk_ref[...]
    nrm = jnp.sqrt(jnp.sum(kk * kk, axis=1, keepdims=True))
    kkn = kk / jnp.maximum(nrm, 1e-12)
    aa_scr[...] = -kkn
    bb_scr[...] = kkn * a_ref[...]

    zeros = jnp.zeros((n, LANES), F32)
    nblk = n // SUBLANES

    def sa_first(kb, acc):
        base = pl.multiple_of(kb * SUBLANES, SUBLANES)
        aab = aa_scr[0, pl.ds(base, SUBLANES), :]
        for j in range(SUBLANES):
            acc = acc + s_scr[base + j] * aab[j:j + 1, :]
        return acc

    sa0 = lax.fori_loop(0, nblk, sa_first, zeros)

    lng = lng_ref[...]
    lnb = lnb_ref[...]
    rk = rk_ref[...]

    def step(t, sa):
        tn = jnp.minimum(t + 1, tc - 1)
        vt = v_ref[t]

        def kblock(kb, carry):
            y, san = carry
            base = pl.multiple_of(kb * SUBLANES, SUBLANES)
            wb = w_ref[t, pl.ds(base, SUBLANES), :]
            bbb = bb_scr[t, pl.ds(base, SUBLANES), :]
            kbk = k_ref[t, pl.ds(base, SUBLANES), :]
            rb = r_ref[t, pl.ds(base, SUBLANES), :]
            aan = aa_scr[tn, pl.ds(base, SUBLANES), :]
            for j in range(SUBLANES):
                sk = s_scr[base + j]
                sn = sk * wb[j:j + 1, :] + sa * bbb[j:j + 1, :] + vt * kbk[j:j + 1, :]
                s_scr[base + j] = sn
                y = y + sn * rb[j:j + 1, :]
                san = san + sn * aan[j:j + 1, :]
            return y, san

        y, san = lax.fori_loop(0, nblk, kblock, (zeros, zeros))
        mu = jnp.mean(y, axis=0, keepdims=True)
        yc = y - mu
        var = jnp.mean(yc * yc, axis=0, keepdims=True)
        yn = yc * lax.rsqrt(var + GN_EPS) * lng + lnb
        bonus = jnp.sum(r_ref[t] * k_ref[t] * rk, axis=0, keepdims=True) * vt
        y_ref[t] = yn + bonus
        return san

    lax.fori_loop(0, tc, step, sa0)

    @pl.when(c == pl.num_programs(1) - 1)
    def _():
        st_ref[...] = s_scr[...]


def _scan(seq, s0, lng, lnb, rk, tc):
    t_len, n, gl = seq[0].shape
    g = gl // LANES
    step_spec = pl.BlockSpec((tc, n, LANES), lambda gi, c: (c, 0, gi))
    par_spec = pl.BlockSpec((n, LANES), lambda gi, c: (0, gi))
    st_spec = pl.BlockSpec((None, n, n, LANES), lambda gi, c: (gi, 0, 0, 0))
    return pl.pallas_call(
        functools.partial(_scan_kernel, tc=tc),
        grid=(g, t_len // tc),
        in_specs=[step_spec] * 6 + [st_spec, par_spec, par_spec, par_spec],
        out_specs=[step_spec, st_spec],
        out_shape=[jax.ShapeDtypeStruct((t_len, n, gl), F32),
                   jax.ShapeDtypeStruct((g, n, n, LANES), F32)],
        scratch_shapes=[pltpu.VMEM((n, n, LANES), F32),
                        pltpu.VMEM((tc, n, LANES), F32),
                        pltpu.VMEM((tc, n, LANES), F32)],
        compiler_params=_cparams("arbitrary", "arbitrary"),
        name="rwkv_scan",
    )(*seq, s0, lng, lnb, rk)


def _sgu_kernel(pu_ref, pv_ref, lg_ref, lb_ref, w_ref, b_ref, y_ref, *, chunks):
    ng = w_ref.shape[0]
    row = lax.broadcasted_iota(jnp.int32, (CHUNK, CHUNK), 0)
    col = lax.broadcasted_iota(jnp.int32, (CHUNK, CHUNK), 1)
    causal = col <= row
    for ci in range(chunks):
        rows = slice(ci * CHUNK, (ci + 1) * CHUNK)
        u = _gelu(pu_ref[rows, :])
        v = _layer_norm_rows(_gelu(pv_ref[rows, :]), lg_ref[...], lb_ref[...], LN_EPS)
        for g in range(ng):
            cols = slice(g * GROUP_B, (g + 1) * GROUP_B)
            ws = jnp.where(causal, w_ref[g], 0.0).astype(BF16)
            mixed = jnp.dot(ws, v[:, cols].astype(BF16), preferred_element_type=F32) + b_ref[g]
            y_ref[rows, cols] = (u[:, cols] * mixed).astype(y_ref.dtype)


def _sgu_prompt(proj, lp, d_b, u_blk, tm):
    m = proj.shape[0]
    ng = d_b // GROUP_B
    return pl.pallas_call(
        functools.partial(_sgu_kernel, chunks=tm // CHUNK),
        grid=(m // tm,),
        in_specs=[pl.BlockSpec((tm, d_b), lambda i: (i, u_blk)),
                  pl.BlockSpec((tm, d_b), lambda i: (i, u_blk + 1)),
                  pl.BlockSpec((1, d_b), lambda i: (0, 0)),
                  pl.BlockSpec((1, d_b), lambda i: (0, 0)),
                  pl.BlockSpec((ng, CHUNK, CHUNK), lambda i: (0, 0, 0)),
                  pl.BlockSpec((ng, CHUNK, GROUP_B), lambda i: (0, 0, 0))],
        out_specs=pl.BlockSpec((tm, d_b), lambda i: (i, 0)),
        out_shape=jax.ShapeDtypeStruct((m, d_b), BF16),
        compiler_params=_cparams("arbitrary"),
        name="sgu_chunked",
    )(proj, proj, lp["sgu_ln_g"], lp["sgu_ln_b"], lp["sgu_w"], lp["sgu_b_full"])


def _sgu_first_kernel(pu_ref, pv_ref, lg_ref, lb_ref, wd_ref, bd_ref, y_ref, v_ref):
    u = _gelu(pu_ref[...])
    v = _layer_norm_rows(_gelu(pv_ref[...]), lg_ref[...], lb_ref[...], LN_EPS)
    v_ref[...] = v
    vb = v.astype(BF16).astype(F32)
    wd = wd_ref[...].astype(BF16).astype(F32)
    y_ref[...] = (u * (vb * wd + bd_ref[...])).astype(y_ref.dtype)


def _sgu_single(proj, lp, d_b, u_blk):
    m = proj.shape[0]
    vec = pl.BlockSpec((1, d_b), lambda i: (0, 0))
    return pl.pallas_call(
        _sgu_first_kernel,
        grid=(1,),
        in_specs=[pl.BlockSpec((m, d_b), lambda i: (0, u_blk)),
                  pl.BlockSpec((m, d_b), lambda i: (0, u_blk + 1)), vec, vec, vec, vec],
        out_specs=[pl.BlockSpec((m, d_b), lambda i: (0, 0))] * 2,
        out_shape=[jax.ShapeDtypeStruct((m, d_b), BF16), jax.ShapeDtypeStruct((m, d_b), F32)],
        compiler_params=_cparams("arbitrary"),
        name="sgu_single",
    )(proj, proj, lp["sgu_ln_g"], lp["sgu_ln_b"], lp["sgu_w_first"], lp["sgu_b_first"])


def _merge_kernel(ya_ref, g_ref, yb_ref, ga_ref, gb_ref, wa_ref, wb_ref, o_ref, a_scr):
    @pl.when(pl.program_id(1) == 0)
    def _():
        a_scr[...] = (ya_ref[...] * g_ref[...]).astype(BF16)

    ta = jnp.dot(a_scr[...], wa_ref[...], preferred_element_type=F32)
    tb = jnp.dot(yb_ref[...], wb_ref[...], preferred_element_type=F32)
    o_ref[...] = (jax.nn.sigmoid(ga_ref[...]) * ta
                  + jax.nn.sigmoid(gb_ref[...]) * tb).astype(o_ref.dtype)


def _merge(ya, g, yb, proj, wa, wb, d, tm, tn):
    m = ya.shape[0]
    d_b = yb.shape[1]
    ga_blk = (3 * d) // tn
    gb_blk = (4 * d) // tn
    return pl.pallas_call(
        _merge_kernel,
        grid=(m // tm, d // tn),
        in_specs=[pl.BlockSpec((tm, d), lambda i, j: (i, 0)),
                  pl.BlockSpec((tm, d), lambda i, j: (i, 0)),
                  pl.BlockSpec((tm, d_b), lambda i, j: (i, 0)),
                  pl.BlockSpec((tm, tn), lambda i, j: (i, ga_blk + j)),
                  pl.BlockSpec((tm, tn), lambda i, j: (i, gb_blk + j)),
                  pl.BlockSpec((d, tn), lambda i, j: (0, j)),
                  pl.BlockSpec((d_b, tn), lambda i, j: (0, j))],
        out_specs=pl.BlockSpec((tm, tn), lambda i, j: (i, j)),
        out_shape=jax.ShapeDtypeStruct((m, d), BF16),
        scratch_shapes=[pltpu.VMEM((tm, d), BF16)],
        compiler_params=_cparams("arbitrary", "arbitrary"),
        name="branch_merge",
    )(ya, g, yb, proj, proj, wa, wb)


def _out_ln_kernel(m_ref, x_ref, gt_ref, sh_ref, sc_ref, wo_ref, wq_ref, g_ref, b_ref,
                   x1_ref, h2_ref, q_ref, *, alpha):
    z = jnp.dot(m_ref[...], wo_ref[...], preferred_element_type=F32)
    x1 = _layer_norm_rows(alpha * x_ref[...] + gt_ref[...] * z, g_ref[...], b_ref[...], LN_EPS)
    x1_ref[...] = x1
    h2 = (x1 * (1.0 + sc_ref[...]) + sh_ref[...]).astype(BF16)
    h2_ref[...] = h2
    q_ref[...] = jnp.dot(h2, wq_ref[...], preferred_element_type=F32).astype(q_ref.dtype)


def _out_ln(mrg, x, mod, wo, wq, ln_g, ln_b, per_row, seq_len, alpha, tm):
    m, d = x.shape
    nq = wq.shape[1]
    vec = pl.BlockSpec((1, d), lambda i: (0, 0))
    row = pl.BlockSpec((tm, d), lambda i: (i, 0))
    return pl.pallas_call(
        functools.partial(_out_ln_kernel, alpha=alpha),
        grid=(m // tm,),
        in_specs=[row, row,
                  _mod_spec(per_row, tm, d, 2, seq_len),
                  _mod_spec(per_row, tm, d, 3, seq_len),
                  _mod_spec(per_row, tm, d, 4, seq_len),
                  pl.BlockSpec((d, d), lambda i: (0, 0)),
                  pl.BlockSpec((d, nq), lambda i: (0, 0)), vec, vec],
        out_specs=[row, row, pl.BlockSpec((tm, nq), lambda i: (i, 0))],
        out_shape=[jax.ShapeDtypeStruct((m, d), F32), jax.ShapeDtypeStruct((m, d), BF16),
                   jax.ShapeDtypeStruct((m, nq), BF16)],
        compiler_params=_cparams("arbitrary"),
        name="out_proj_ln1",
    )(mrg, x, mod, mod, mod, wo, wq, ln_g, ln_b)


_PAIRS = [(i, j) for i in range(TOPK) for j in range(TOPK) if (i + 1) * (j + 1) <= TOPK]
_NPAIR_PAD = -(-len(_PAIRS) // SUBLANES) * SUBLANES


def _take_top(work, rounds, record_rank):
    rows = work.shape[0]
    iota = lax.broadcasted_iota(jnp.int32, work.shape, 0).astype(F32)
    mark = jnp.full(work.shape, float(rounds) if record_rank else 0.0, F32)
    vals = []
    for r in range(rounds):
        mx = jnp.max(work, axis=0, keepdims=True)
        first = jnp.min(jnp.where(work == mx, iota, float(rows)), axis=0, keepdims=True)
        hit = iota == first
        mark = jnp.where(hit, float(r) if record_rank else 1.0, mark)
        work = jnp.where(hit, -jnp.inf, work)
        vals.append(mx)
    return vals, mark


def _peer_topk_kernel(q_ref, keys_ref, cnt_ref, n1_ref, p1_ref, r2_ref, p2_ref, cand_scr):
    tt = q_ref.shape[0]
    nt = (((1,), (1,)), ((), ()))
    cand_scr[...] = jnp.full(cand_scr.shape, -jnp.inf, F32)

    def head(h, carry):
        base = pl.multiple_of(h * 2 * DK_HALF, 2 * DK_HALF)
        q1 = q_ref[:, pl.ds(base, DK_HALF)]
        q2 = q_ref[:, pl.ds(base + DK_HALF, DK_HALF)]
        s1 = lax.dot_general(keys_ref[h, 0].astype(BF16), q1, nt, preferred_element_type=F32)
        s2 = lax.dot_general(keys_ref[h, 1].astype(BF16), q2, nt, preferred_element_type=F32)
        a1, rank1 = _take_top(s1, TOPK, True)
        a2, rank2 = _take_top(s2, TOPK, True)
        for pos, (i, j) in enumerate(_PAIRS):
            cand_scr[pos:pos + 1, :] = a1[i] + a2[j]
        cand = cand_scr[...]
        _, sel = _take_top(cand, TOPK, False)
        top = a1[0] + a2[0]
        z = jnp.sum(sel * jnp.exp(cand - top), axis=0, keepdims=True)
        cnt = jnp.dot(cnt_ref[...], sel.astype(BF16), preferred_element_type=F32)
        n1 = jnp.zeros((N_KEYS, tt), F32)
        for i in range(TOPK):
            n1 = jnp.where(rank1 == float(i), cnt[i:i + 1, :], n1)
        n1_ref[h] = n1
        p1_ref[h] = jnp.where(rank1 < float(TOPK), jnp.exp(s1 - a1[0]), 0.0)
        r2_ref[h] = rank2
        p2_ref[h] = jnp.where(rank2 < float(TOPK), jnp.exp(s2 - a2[0]), 0.0) / z
        return carry

    lax.fori_loop(0, PEER_HEADS, head, 0)


def _peer_topk(q, keys, tt):
    m = q.shape[0]
    cnt = jnp.zeros((TOPK, _NPAIR_PAD), F32)
    for pos, (i, _) in enumerate(_PAIRS):
        cnt = cnt.at[i, pos].set(1.0)
    out = pl.BlockSpec((PEER_HEADS, N_KEYS, tt), lambda i: (0, 0, i))
    return pl.pallas_call(
        _peer_topk_kernel,
        grid=(m // tt,),
        in_specs=[pl.BlockSpec((tt, q.shape[1]), lambda i: (i, 0)),
                  pl.BlockSpec(keys.shape, lambda i: (0, 0, 0, 0)),
                  pl.BlockSpec(cnt.shape, lambda i: (0, 0))],
        out_specs=[out] * 4,
        out_shape=[jax.ShapeDtypeStruct((PEER_HEADS, N_KEYS, m), F32)] * 4,
        scratch_shapes=[pltpu.VMEM((_NPAIR_PAD, tt), F32)],
        compiler_params=_cparams("arbitrary"),
        name="peer_topk",
    )(q, keys, cnt.astype(BF16))


def _peer_dense_kernel(h_ref, u_ref, v_ref, n1_ref, p1_ref, r2_ref, p2_ref, o_ref, w_scr, *, et):
    e = pl.program_id(1)
    nt = (((1,), (1,)), ((), ()))
    tn = (((0,), (0,)), ((), ()))

    @pl.when(e == 0)
    def _():
        o_ref[...] = jnp.zeros(o_ref.shape, F32)

    s = lax.dot_general(u_ref[...], h_ref[...], nt, preferred_element_type=F32)
    for j in range(et // N_KEYS):
        e1 = e * (et // N_KEYS) + j
        n1 = n1_ref[e1]
        p1 = p1_ref[e1]
        gate = jnp.zeros((N_KEYS, s.shape[1]), F32)
        for h in range(PEER_HEADS):
            gate = gate + jnp.where(r2_ref[h] < n1[h:h + 1, :], p2_ref[h], 0.0) * p1[h:h + 1, :]
        rows = slice(j * N_KEYS, (j + 1) * N_KEYS)
        w_scr[rows, :] = (_gelu(s[rows, :]) * gate).astype(BF16)
    o_ref[...] += lax.dot_general(w_scr[...], v_ref[...], tn, preferred_element_type=F32)


def _peer_dense(h2, u_tab, v_tab, n1, p1, r2, p2, tt, et):
    m, d = h2.shape
    ne = u_tab.shape[0]
    sel_a = pl.BlockSpec((N_KEYS, PEER_HEADS, tt), lambda i, e: (0, 0, i))
    sel_b = pl.BlockSpec((PEER_HEADS, N_KEYS, tt), lambda i, e: (0, 0, i))
    return pl.pallas_call(
        functools.partial(_peer_dense_kernel, et=et),
        grid=(m // tt, ne // et),
        in_specs=[pl.BlockSpec((tt, d), lambda i, e: (i, 0)),
                  pl.BlockSpec((et, d), lambda i, e: (e, 0)),
                  pl.BlockSpec((et, d), lambda i, e: (e, 0)),
                  sel_a, sel_a, sel_b, sel_b],
        out_specs=pl.BlockSpec((tt, d), lambda i, e: (i, 0)),
        out_shape=jax.ShapeDtypeStruct((m, d), F32),
        scratch_shapes=[pltpu.VMEM((et, tt), BF16)],
        compiler_params=_cparams("arbitrary", "arbitrary"),
        name="peer_dense",
    )(h2, u_tab, v_tab, n1, p1, r2, p2)


def _final_kernel(x1_ref, f_ref, gt_ref, g_ref, b_ref, o_ref, *, alpha):
    o_ref[...] = _layer_norm_rows(alpha * x1_ref[...] + gt_ref[...] * f_ref[...],
                                  g_ref[...], b_ref[...], LN_EPS)


def _final_ln(x1, f, mod, ln_g, ln_b, per_row, seq_len, alpha, tm):
    m, d = x1.shape
    row = pl.BlockSpec((tm, d), lambda i: (i, 0))
    vec = pl.BlockSpec((1, d), lambda i: (0, 0))
    return pl.pallas_call(
        functools.partial(_final_kernel, alpha=alpha),
        grid=(m // tm,),
        in_specs=[row, row, _mod_spec(per_row, tm, d, 5, seq_len), vec, vec],
        out_specs=row,
        out_shape=jax.ShapeDtypeStruct((m, d), F32),
        compiler_params=_cparams("arbitrary"),
        name="final_ln2",
    )(x1, f, mod, ln_g, ln_b)


def _pack_w_in(w_in, d, d_b):
    o_small = 3 * d
    o_u = o_small + SMALL
    o_v = o_u + d_b
    o_ga = o_v + d_b
    o_gb = o_ga + d
    small = jnp.pad(w_in[:, o_small:o_u], ((0, 0), (0, SMALL_PAD - SMALL)))
    return jnp.concatenate([w_in[:, :o_small], w_in[:, o_ga:o_gb], w_in[:, o_gb:o_gb + d],
                            w_in[:, o_u:o_v], w_in[:, o_v:o_ga], small], axis=1).astype(BF16)


def _split_shift(s, d):
    return s[:, :3 * d], jnp.pad(s[:, 3 * d:], ((0, 0), (0, SMALL_PAD - SMALL)))


def _join_shift(proj_rows, d):
    small_off = 6 * d
    return jnp.concatenate([proj_rows[:, :3 * d], proj_rows[:, small_off:small_off + SMALL]], axis=1)


def _rwkv_prompt(pre, bsz, t_len, heads, p, tc):
    r, w, k, v, kk, a, g = pre
    to_l = lambda x: x.reshape(bsz, t_len, heads, HEAD_A).transpose(1, 3, 0, 2).reshape(
        t_len, HEAD_A, bsz * heads)
    tile = lambda x: jnp.tile(x.reshape(heads, HEAD_A).T, (1, bsz))
    s0 = jnp.zeros((bsz * heads // LANES, HEAD_A, HEAD_A, LANES), F32)
    y, st = _scan([to_l(x) for x in (r, w, k, v, kk, a)], s0,
                  tile(p["lnx_g"]), tile(p["lnx_b"]), tile(p["r_k"]), tc)
    y = y.reshape(t_len, HEAD_A, bsz, heads).transpose(2, 0, 3, 1).reshape(bsz * t_len, -1)
    wkv = st.transpose(1, 2, 0, 3).reshape(HEAD_A, HEAD_A, bsz, heads).transpose(2, 3, 1, 0)
    return y, g, wkv


def _rwkv_sample(pre, nb, heads, wkv0, p):
    r, w, k, v, kk, a, g = pre
    to_l = lambda x: x.reshape(nb, heads, HEAD_A).transpose(2, 1, 0).reshape(1, HEAD_A, heads * nb)
    rep = lambda x: jnp.repeat(x.reshape(heads, HEAD_A).T, nb, axis=1)
    s0 = wkv0.transpose(1, 3, 2, 0)
    y, st = _scan([to_l(x) for x in (r, w, k, v, kk, a)], s0,
                  rep(p["lnx_g"]), rep(p["lnx_b"]), rep(p["r_k"]), 1)
    y = y.reshape(HEAD_A, heads, nb).transpose(2, 1, 0).reshape(nb, heads * HEAD_A)
    return y, g, st.transpose(3, 0, 2, 1)


def _layer(x_p, x_s, c_p, c_s, wkv_s, shift_s, p):
    bsz, t_len, d = x_p.shape
    nb = x_s.shape[0]
    heads = d // HEAD_A
    d_b = d // 2
    alpha = p["alpha"]
    mp = bsz * t_len

    lp = {
        "mu_main": p["mu_shift"][None, :3 * d],
        "mu_small": jnp.pad(p["mu_shift"][None, 3 * d:], ((0, 0), (0, SMALL_PAD - SMALL))),
        "w0": p["w0"][None], "w2": p["w2"], "a0": p["a0"][None], "a2": p["a2"], "g2": p["g2"],
        "k_k": p["k_k"][None], "k_a": p["k_a"][None],
        "sgu_ln_g": p["sgu_ln_g"][None], "sgu_ln_b": p["sgu_ln_b"][None], "sgu_w": p["sgu_w"],
        "sgu_b_full": jnp.broadcast_to(p["sgu_b"][:, :, None], p["sgu_b"].shape + (GROUP_B,)),
        "sgu_w_first": jnp.repeat(p["sgu_w"][:, 0, 0], GROUP_B)[None],
        "sgu_b_first": jnp.repeat(p["sgu_b"][:, 0], GROUP_B)[None],
    }
    ln1 = (p["ln1_g"][None], p["ln1_b"][None])
    ln2 = (p["ln2_g"][None], p["ln2_b"][None])

    c_all = jnp.concatenate([c_p, c_s], axis=0)
    mod = _modulation(jnp.pad(c_all, ((0, (-c_all.shape[0]) % SUBLANES), (0, 0))),
                      p["w_ada"], p["b_ada"])
    mod_p = mod[:bsz].reshape(bsz, 1, 6 * d)
    mod_s = mod[bsz:bsz + nb]

    w_pack = _pack_w_in(p["w_in"], d, d_b)
    tn_in = _pick(w_pack.shape[1], (1280, 512))
    xp2 = x_p.reshape(mp, d)
    xs2 = x_s.reshape(nb, d)
    tm_big = _pick(t_len, (1024, 512, 256, 128))
    tm_mid = _pick(t_len, (512, 256, 128))
    tm_small = _pick(t_len, (256, 128))
    proj_p = _inproj(xp2, mod_p, w_pack, False, t_len, tm_big, tn_in)
    proj_s = _inproj(xs2, mod_s, w_pack, True, 1, nb, tn_in)

    zero_state = jnp.zeros((bsz, 3 * d + SMALL), F32)
    zm, zs = _split_shift(zero_state, d)
    pre_p = _rwkv_pre(proj_p, zm[:, None, :], zs[:, None, :], lp, d, t_len, CHUNK)
    sm, ss = _split_shift(shift_s, d)
    pre_s = _rwkv_pre(proj_s, sm, ss, lp, d, 1, nb)
    ya_p, g_p, wkv_p = _rwkv_prompt(pre_p, bsz, t_len, heads, p, _pick(t_len, (64, 32, 16, 8)))
    ya_s, g_s, wkv_s_new = _rwkv_sample(pre_s, nb, heads, wkv_s, p)

    u_blk = (5 * d) // d_b
    yb_p = _sgu_prompt(proj_p, lp, d_b, u_blk, tm_mid)
    yb_s, vrows_s = _sgu_single(proj_s, lp, d_b, u_blk)

    wa = p["w_br_a"].astype(BF16)
    wb = p["w_br_b"].astype(BF16)
    wo = p["w_o"].astype(BF16)
    wq = p["peer_wq"].astype(BF16)
    tn_mrg = _pick(d, (512, 256, 128))
    mrg_p = _merge(ya_p, g_p, yb_p, proj_p, wa, wb, d, tm_mid, tn_mrg)
    mrg_s = _merge(ya_s, g_s, yb_s, proj_s, wa, wb, d, nb, tn_mrg)
    x1_p, h2_p, q_p = _out_ln(mrg_p, xp2, mod_p, wo, wq, *ln1, False, t_len, alpha, tm_small)
    x1_s, h2_s, q_s = _out_ln(mrg_s, xs2, mod_s, wo, wq, *ln1, True, 1, alpha, nb)

    h2 = jnp.concatenate([h2_p, h2_s], axis=0)
    q = jnp.concatenate([q_p, q_s], axis=0)
    n1, p1, r2, p2 = _peer_topk(q, p["peer_keys"], LANES)
    n1 = n1.transpose(1, 0, 2)
    p1 = p1.transpose(1, 0, 2)
    m_all = h2.shape[0]
    f = _peer_dense(h2, p["peer_u"].astype(BF16), p["peer_v"].astype(BF16), n1, p1, r2, p2,
                    _pick(m_all, (640, 512, 256, 128)), 4 * N_KEYS)

    y_p = _final_ln(x1_p, f[:mp], mod_p, *ln2, False, t_len, alpha, tm_small)
    y_s = _final_ln(x1_s, f[mp:], mod_s, *ln2, True, 1, alpha, nb)

    last = proj_p.reshape(bsz, t_len, -1)[:, -1]
    return (y_p.reshape(bsz, t_len, d), y_s.reshape(nb, 1, d), wkv_p, _join_shift(last, d),
            wkv_s_new, _join_shift(proj_s, d), vrows_s.reshape(nb, 1, d_b))


def kernel(x_prompt, x_sample, c_prompt, c_sample, state_wkv, state_shift, w_ada, b_ada, w_in, mu_shift, w0, w2, a0, a2, g2, k_k, k_a, r_k, lnx_g, lnx_b, sgu_ln_g, sgu_ln_b, sgu_w, sgu_b, w_br_a, w_br_b, w_o, ln1_g, ln1_b, peer_wq, peer_keys, peer_u, peer_v, ln2_g, ln2_b):
    names = ("w_ada", "b_ada", "w_in", "mu_shift", "w0", "w2", "a0", "a2", "g2", "k_k", "k_a",
             "r_k", "lnx_g", "lnx_b", "sgu_ln_g", "sgu_ln_b", "sgu_w", "sgu_b", "w_br_a",
             "w_br_b", "w_o", "ln1_g", "ln1_b", "peer_wq", "peer_keys", "peer_u", "peer_v",
             "ln2_g", "ln2_b")
    stacked = (w_ada, b_ada, w_in, mu_shift, w0, w2, a0, a2, g2, k_k, k_a, r_k, lnx_g, lnx_b,
               sgu_ln_g, sgu_ln_b, sgu_w, sgu_b, w_br_a, w_br_b, w_o, ln1_g, ln1_b, peer_wq,
               peer_keys, peer_u, peer_v, ln2_g, ln2_b)
    depth = w_ada.shape[0]
    alpha = (2 * depth) ** 0.25
    y_p, y_s = x_prompt, x_sample
    outs = [[] for _ in range(5)]
    for l in range(depth):
        p = {n: a[l] for n, a in zip(names, stacked)}
        p["alpha"] = alpha
        y_p, y_s, *state = _layer(y_p, y_s, c_prompt, c_sample, state_wkv[l], state_shift[l], p)
        for acc, s in zip(outs, state):
            acc.append(s)
    return (y_p, y_s) + tuple(jnp.stack(o) for o in outs)
```

```python
import functools

import jax
import jax.numpy as jnp
from jax import lax
from jax.experimental import pallas as pl
from jax.experimental.pallas import tpu as pltpu

F32 = jnp.float32
BF16 = jnp.bfloat16

HEAD_A = 64
R_DECAY = 96
R_AAA = 96
R_GATE = 256
SMALL = R_DECAY + R_AAA + R_GATE
SMALL_PAD = 512
CHUNK = 128
GROUP_B = 128
PEER_HEADS = 8
N_KEYS = 128
DK_HALF = 128
TOPK = 16
LN_EPS = 1e-5
GN_EPS = 64e-5

LANES = 128
SUBLANES = 8
VMEM_LIMIT = 56 * 1024 * 1024


def _cparams(*sem):
    return pltpu.CompilerParams(dimension_semantics=sem, vmem_limit_bytes=VMEM_LIMIT)


def _gelu(x):
    return jax.nn.gelu(x, approximate=True)


def _layer_norm_rows(x, g, b, eps):
    mu = jnp.mean(x, axis=-1, keepdims=True)
    xc = x - mu
    var = jnp.mean(xc * xc, axis=-1, keepdims=True)
    return xc * lax.rsqrt(var + eps) * g + b


def _pick(n, prefs):
    for p in prefs:
        if n % p == 0:
            return p
    return n


def _mod_spec(per_row, tm, d, chunk, rows_per_seq):
    if per_row:
        return pl.BlockSpec((tm, d), lambda i, *_: (i, chunk))
    return pl.BlockSpec((None, 1, d), lambda i, *_: ((i * tm) // rows_per_seq, 0, chunk))


def _mod_kernel(c_ref, w_ref, b_ref, o_ref):
    c = c_ref[...]
    s = (c * jax.nn.sigmoid(c)).astype(BF16)
    o_ref[...] = jnp.dot(s, w_ref[...].astype(BF16), preferred_element_type=F32) + b_ref[...]


def _modulation(c_all, w_ada, b_ada):
    m, d = c_all.shape
    n = w_ada.shape[1]
    tn = _pick(n, (1536, 512))
    return pl.pallas_call(
        _mod_kernel,
        grid=(n // tn,),
        in_specs=[pl.BlockSpec((m, d), lambda j: (0, 0)),
                  pl.BlockSpec((d, tn), lambda j: (0, j)),
                  pl.BlockSpec((1, tn), lambda j: (0, j))],
        out_specs=pl.BlockSpec((m, tn), lambda j: (0, j)),
        out_shape=jax.ShapeDtypeStruct((m, n), F32),
        compiler_params=_cparams("arbitrary"),
        name="adaln_mod",
    )(c_all, w_ada, b_ada.reshape(1, n))


def _inproj_kernel(x_ref, sh_ref, sc_ref, w_ref, o_ref, h_scr):
    @pl.when(pl.program_id(1) == 0)
    def _():
        h_scr[...] = (x_ref[...] * (1.0 + sc_ref[...]) + sh_ref[...]).astype(BF16)

    o_ref[...] = jnp.dot(h_scr[...], w_ref[...], preferred_element_type=F32)


def _inproj(x, mod, w_pack, per_row, seq_len, tm, tn):
    m, d = x.shape
    n = w_pack.shape[1]
    return pl.pallas_call(
        _inproj_kernel,
        grid=(m // tm, n // tn),
        in_specs=[pl.BlockSpec((tm, d), lambda i, j: (i, 0)),
                  _mod_spec(per_row, tm, d, 0, seq_len),
                  _mod_spec(per_row, tm, d, 1, seq_len),
                  pl.BlockSpec((d, tn), lambda i, j: (0, j))],
        out_specs=pl.BlockSpec((tm, tn), lambda i, j: (i, j)),
        out_shape=jax.ShapeDtypeStruct((m, n), F32),
        scratch_shapes=[pltpu.VMEM((tm, d), BF16)],
        compiler_params=_cparams("arbitrary", "arbitrary"),
        name="in_proj",
    )(x, mod, mod, w_pack)


def _rwkv_pre_kernel(pm_ref, ps_ref, qm_ref, qs_ref, stm_ref, sts_ref, mum_ref, mus_ref,
                     w0_ref, w2_ref, a0_ref, a2_ref, g2_ref, kk_ref, ka_ref,
                     r_o, w_o, k_o, v_o, kk_o, a_o, g_o, *, d, per_row_state, tiles_per_seq):
    def shifted(p, q, st):
        if per_row_state:
            return st
        first = (pl.program_id(0) % tiles_per_seq) == 0
        prev_row = jnp.where(first, st, q[SUBLANES - 1:SUBLANES, :])
        row = lax.broadcasted_iota(jnp.int32, p.shape, 0)
        return jnp.where(row == 0, prev_row, pltpu.roll(p, 1, 0))

    def mix(p, q, st, mu):
        return p + (shifted(p, q, st) - p) * mu

    xs_s = mix(ps_ref[...], qs_ref[...], sts_ref[...], mus_ref[...])
    wl = xs_s[:, 0:R_DECAY]
    al = xs_s[:, R_DECAY:R_DECAY + R_AAA]
    gl = xs_s[:, R_DECAY + R_AAA:SMALL]
    wlin = w0_ref[...] + jnp.dot(jnp.tanh(wl).astype(BF16), w2_ref[...].astype(BF16),
                                 preferred_element_type=F32)
    w = -jax.nn.softplus(-wlin) - 0.5
    w_o[...] = jnp.exp(-jnp.exp(w))
    a = jax.nn.sigmoid(a0_ref[...] + jnp.dot(al.astype(BF16), a2_ref[...].astype(BF16),
                                             preferred_element_type=F32))
    a_o[...] = a
    g_o[...] = jnp.dot(jax.nn.sigmoid(gl).astype(BF16), g2_ref[...].astype(BF16),
                       preferred_element_type=F32)

    def part(n):
        cols = slice(n * d, (n + 1) * d)
        return mix(pm_ref[:, cols], qm_ref[:, cols], stm_ref[:, cols], mum_ref[:, cols])

    r_o[...] = part(0)
    k = part(1)
    kk_o[...] = k * kk_ref[...]
    k_o[...] = k * (1.0 + (a - 1.0) * ka_ref[...])
    v_o[...] = part(2)


def _rwkv_pre(proj, state_main, state_small, lp, d, seq_len, tm):
    m = proj.shape[0]
    per_row_state = seq_len == 1
    tiles_per_seq = max(seq_len // tm, 1)
    small_blk = (6 * d) // SMALL_PAD

    def prev_map(col):
        def f(i):
            return (jnp.maximum(i * (tm // SUBLANES) - 1, 0), col)
        return f

    if per_row_state:
        stm_spec = pl.BlockSpec((tm, 3 * d), lambda i: (i, 0))
        sts_spec = pl.BlockSpec((tm, SMALL_PAD), lambda i: (i, 0))
    else:
        stm_spec = pl.BlockSpec((None, 1, 3 * d), lambda i: (i // tiles_per_seq, 0, 0))
        sts_spec = pl.BlockSpec((None, 1, SMALL_PAD), lambda i: (i // tiles_per_seq, 0, 0))

    vec = lambda w: pl.BlockSpec((1, w), lambda i: (0, 0))
    full = lambda a: pl.BlockSpec(a.shape, lambda i: (0,) * a.ndim)
    out = pl.BlockSpec((tm, d), lambda i: (i, 0))
    kern = functools.partial(_rwkv_pre_kernel, d=d, per_row_state=per_row_state,
                             tiles_per_seq=tiles_per_seq)
    in_specs = [pl.BlockSpec((tm, 3 * d), lambda i: (i, 0)),
                pl.BlockSpec((tm, SMALL_PAD), lambda i: (i, small_blk)),
                pl.BlockSpec((SUBLANES, 3 * d), prev_map(0)),
                pl.BlockSpec((SUBLANES, SMALL_PAD), prev_map(small_blk)),
                stm_spec, sts_spec, vec(3 * d), vec(SMALL_PAD), vec(d), full(lp["w2"]), vec(d),
                full(lp["a2"]), full(lp["g2"]), vec(d), vec(d)]
    return pl.pallas_call(
        kern,
        grid=(m // tm,),
        in_specs=in_specs,
        out_specs=[out] * 7,
        out_shape=[jax.ShapeDtypeStruct((m, d), F32)] * 7,
        compiler_params=_cparams("arbitrary"),
        name="rwkv_pre",
    )(proj, proj, proj, proj, state_main, state_small, lp["mu_main"], lp["mu_small"], lp["w0"],
      lp["w2"], lp["a0"], lp["a2"], lp["g2"], lp["k_k"], lp["k_a"])


def _scan_kernel(r_ref, w_ref, k_ref, v_ref, kk_ref, a_ref, s0_ref, lng_ref, lnb_ref, rk_ref,
                 y_ref, st_ref, s_scr, aa_scr, bb_scr, *, tc):
    n = HEAD_A
    c = pl.program_id(1)

    @pl.when(c == 0)
    def _():
        s_scr[...] = s0_ref[...]

    kk = kk_ref[...]
    nrm = jnp.sqrt(jnp.sum(kk * kk, axis=1, keepdims=True))
    kkn = kk / jnp.maximum(nrm, 1e-12)
    aa_scr[...] = -kkn
    bb_scr[...] = kkn * a_ref[...]

    zeros = jnp.zeros((n, LANES), F32)
    nblk = n // SUBLANES

    def sa_first(kb, acc):
        base = pl.multiple_of(kb * SUBLANES, SUBLANES)
        aab = aa_scr[0, pl.ds(base, SUBLANES), :]
        for j in range(SUBLANES):
            acc = acc + s_scr[base + j] * aab[j:j + 1, :]
        return acc

    sa0 = lax.fori_loop(0, nblk, sa_first, zeros)

    lng = lng_ref[...]
    lnb = lnb_ref[...]
    rk = rk_ref[...]

    def step(t, sa):
        tn = jnp.minimum(t + 1, tc - 1)
        vt = v_ref[t]

        def kblock(kb, carry):
            y, san = carry
            base = pl.multiple_of(kb * SUBLANES, SUBLANES)
            wb = w_ref[t, pl.ds(base, SUBLANES), :]
            bbb = bb_scr[t, pl.ds(base, SUBLANES), :]
            kbk = k_ref[t, pl.ds(base, SUBLANES), :]
            rb = r_ref[t, pl.ds(base, SUBLANES), :]
            aan = aa_scr[tn, pl.ds(base, SUBLANES), :]
            for j in range(SUBLANES):
                sk = s_scr[base + j]
                sn = sk * wb[j:j + 1, :] + sa * bbb[j:j + 1, :] + vt * kbk[j:j + 1, :]
                s_scr[base + j] = sn
                y = y + sn * rb[j:j + 1, :]
                san = san + sn * aan[j:j + 1, :]
            return y, san

        y, san = lax.fori_loop(0, nblk, kblock, (zeros, zeros))
        mu = jnp.mean(y, axis=0, keepdims=True)
        yc = y - mu
        var = jnp.mean(yc * yc, axis=0, keepdims=True)
        yn = yc * lax.rsqrt(var + GN_EPS) * lng + lnb
        bonus = jnp.sum(r_ref[t] * k_ref[t] * rk, axis=0, keepdims=True) * vt
        y_ref[t] = yn + bonus
        return san

    lax.fori_loop(0, tc, step, sa0)

    @pl.when(c == pl.num_programs(1) - 1)
    def _():
        st_ref[...] = s_scr[...]


def _scan(seq, s0, lng, lnb, rk, tc):
    t_len, n, gl = seq[0].shape
    g = gl // LANES
    step_spec = pl.BlockSpec((tc, n, LANES), lambda gi, c: (c, 0, gi))
    par_spec = pl.BlockSpec((n, LANES), lambda gi, c: (0, gi))
    st_spec = pl.BlockSpec((None, n, n, LANES), lambda gi, c: (gi, 0, 0, 0))
    return pl.pallas_call(
        functools.partial(_scan_kernel, tc=tc),
        grid=(g, t_len // tc),
        in_specs=[step_spec] * 6 + [st_spec, par_spec, par_spec, par_spec],
        out_specs=[step_spec, st_spec],
        out_shape=[jax.ShapeDtypeStruct((t_len, n, gl), F32),
                   jax.ShapeDtypeStruct((g, n, n, LANES), F32)],
        scratch_shapes=[pltpu.VMEM((n, n, LANES), F32),
                        pltpu.VMEM((tc, n, LANES), F32),
                        pltpu.VMEM((tc, n, LANES), F32)],
        compiler_params=_cparams("arbitrary", "arbitrary"),
        name="rwkv_scan",
    )(*seq, s0, lng, lnb, rk)


def _sgu_kernel(pu_ref, pv_ref, lg_ref, lb_ref, w_ref, b_ref, y_ref, *, chunks):
    ng = w_ref.shape[0]
    row = lax.broadcasted_iota(jnp.int32, (CHUNK, CHUNK), 0)
    col = lax.broadcasted_iota(jnp.int32, (CHUNK, CHUNK), 1)
    causal = col <= row
    for ci in range(chunks):
        rows = slice(ci * CHUNK, (ci + 1) * CHUNK)
        u = _gelu(pu_ref[rows, :])
        v = _layer_norm_rows(_gelu(pv_ref[rows, :]), lg_ref[...], lb_ref[...], LN_EPS)
        for g in range(ng):
            cols = slice(g * GROUP_B, (g + 1) * GROUP_B)
            ws = jnp.where(causal, w_ref[g], 0.0).astype(BF16)
            mixed = jnp.dot(ws, v[:, cols].astype(BF16), preferred_element_type=F32) + b_ref[g]
            y_ref[rows, cols] = (u[:, cols] * mixed).astype(y_ref.dtype)


def _sgu_prompt(proj, lp, d_b, u_blk, tm):
    m = proj.shape[0]
    ng = d_b // GROUP_B
    return pl.pallas_call(
        functools.partial(_sgu_kernel, chunks=tm // CHUNK),
        grid=(m // tm,),
        in_specs=[pl.BlockSpec((tm, d_b), lambda i: (i, u_blk)),
                  pl.BlockSpec((tm, d_b), lambda i: (i, u_blk + 1)),
                  pl.BlockSpec((1, d_b), lambda i: (0, 0)),
                  pl.BlockSpec((1, d_b), lambda i: (0, 0)),
                  pl.BlockSpec((ng, CHUNK, CHUNK), lambda i: (0, 0, 0)),
                  pl.BlockSpec((ng, CHUNK, GROUP_B), lambda i: (0, 0, 0))],
        out_specs=pl.BlockSpec((tm, d_b), lambda i: (i, 0)),
        out_shape=jax.ShapeDtypeStruct((m, d_b), BF16),
        compiler_params=_cparams("arbitrary"),
        name="sgu_chunked",
    )(proj, proj, lp["sgu_ln_g"], lp["sgu_ln_b"], lp["sgu_w"], lp["sgu_b_full"])


def _sgu_first_kernel(pu_ref, pv_ref, lg_ref, lb_ref, wd_ref, bd_ref, y_ref, v_ref):
    u = _gelu(pu_ref[...])
    v = _layer_norm_rows(_gelu(pv_ref[...]), lg_ref[...], lb_ref[...], LN_EPS)
    v_ref[...] = v
    vb = v.astype(BF16).astype(F32)
    wd = wd_ref[...].astype(BF16).astype(F32)
    y_ref[...] = (u * (vb * wd + bd_ref[...])).astype(y_ref.dtype)


def _sgu_single(proj, lp, d_b, u_blk):
    m = proj.shape[0]
    vec = pl.BlockSpec((1, d_b), lambda i: (0, 0))
    return pl.pallas_call(
        _sgu_first_kernel,
        grid=(1,),
        in_specs=[pl.BlockSpec((m, d_b), lambda i: (0, u_blk)),
                  pl.BlockSpec((m, d_b), lambda i: (0, u_blk + 1)), vec, vec, vec, vec],
        out_specs=[pl.BlockSpec((m, d_b), lambda i: (0, 0))] * 2,
        out_shape=[jax.ShapeDtypeStruct((m, d_b), BF16), jax.ShapeDtypeStruct((m, d_b), F32)],
        compiler_params=_cparams("arbitrary"),
        name="sgu_single",
    )(proj, proj, lp["sgu_ln_g"], lp["sgu_ln_b"], lp["sgu_w_first"], lp["sgu_b_first"])


def _merge_kernel(ya_ref, g_ref, yb_ref, ga_ref, gb_ref, wa_ref, wb_ref, o_ref, a_scr):
    @pl.when(pl.program_id(1) == 0)
    def _():
        a_scr[...] = (ya_ref[...] * g_ref[...]).astype(BF16)

    ta = jnp.dot(a_scr[...], wa_ref[...], preferred_element_type=F32)
    tb = jnp.dot(yb_ref[...], wb_ref[...], preferred_element_type=F32)
    o_ref[...] = (jax.nn.sigmoid(ga_ref[...]) * ta
                  + jax.nn.sigmoid(gb_ref[...]) * tb).astype(o_ref.dtype)


def _merge(ya, g, yb, proj, wa, wb, d, tm, tn):
    m = ya.shape[0]
    d_b = yb.shape[1]
    ga_blk = (3 * d) // tn
    gb_blk = (4 * d) // tn
    return pl.pallas_call(
        _merge_kernel,
        grid=(m // tm, d // tn),
        in_specs=[pl.BlockSpec((tm, d), lambda i, j: (i, 0)),
                  pl.BlockSpec((tm, d), lambda i, j: (i, 0)),
                  pl.BlockSpec((tm, d_b), lambda i, j: (i, 0)),
                  pl.BlockSpec((tm, tn), lambda i, j: (i, ga_blk + j)),
                  pl.BlockSpec((tm, tn), lambda i, j: (i, gb_blk + j)),
                  pl.BlockSpec((d, tn), lambda i, j: (0, j)),
                  pl.BlockSpec((d_b, tn), lambda i, j: (0, j))],
        out_specs=pl.BlockSpec((tm, tn), lambda i, j: (i, j)),
        out_shape=jax.ShapeDtypeStruct((m, d), BF16),
        scratch_shapes=[pltpu.VMEM((tm, d), BF16)],
        compiler_params=_cparams("arbitrary", "arbitrary"),
        name="branch_merge",
    )(ya, g, yb, proj, proj, wa, wb)


def _out_ln_kernel(m_ref, x_ref, gt_ref, sh_ref, sc_ref, wo_ref, wq_ref, g_ref, b_ref,
                   x1_ref, h2_ref, q_ref, *, alpha):
    z = jnp.dot(m_ref[...], wo_ref[...], preferred_element_type=F32)
    x1 = _layer_norm_rows(alpha * x_ref[...] + gt_ref[...] * z, g_ref[...], b_ref[...], LN_EPS)
    x1_ref[...] = x1
    h2 = (x1 * (1.0 + sc_ref[...]) + sh_ref[...]).astype(BF16)
    h2_ref[...] = h2
    q_ref[...] = jnp.dot(h2, wq_ref[...], preferred_element_type=F32).astype(q_ref.dtype)


def _out_ln(mrg, x, mod, wo, wq, ln_g, ln_b, per_row, seq_len, alpha, tm):
    m, d = x.shape
    nq = wq.shape[1]
    vec = pl.BlockSpec((1, d), lambda i: (0, 0))
    row = pl.BlockSpec((tm, d), lambda i: (i, 0))
    return pl.pallas_call(
        functools.partial(_out_ln_kernel, alpha=alpha),
        grid=(m // tm,),
        in_specs=[row, row,
                  _mod_spec(per_row, tm, d, 2, seq_len),
                  _mod_spec(per_row, tm, d, 3, seq_len),
                  _mod_spec(per_row, tm, d, 4, seq_len),
                  pl.BlockSpec((d, d), lambda i: (0, 0)),
                  pl.BlockSpec((d, nq), lambda i: (0, 0)), vec, vec],
        out_specs=[row, row, pl.BlockSpec((tm, nq), lambda i: (i, 0))],
        out_shape=[jax.ShapeDtypeStruct((m, d), F32), jax.ShapeDtypeStruct((m, d), BF16),
                   jax.ShapeDtypeStruct((m, nq), BF16)],
        compiler_params=_cparams("arbitrary"),
        name="out_proj_ln1",
    )(mrg, x, mod, mod, mod, wo, wq, ln_g, ln_b)


_PAIRS = [(i, j) for i in range(TOPK) for j in range(TOPK) if (i + 1) * (j + 1) <= TOPK]
_NPAIR_PAD = -(-len(_PAIRS) // SUBLANES) * SUBLANES


def _take_top(work, rounds, break_ties):
    rows = work.shape[0]
    iota = lax.broadcasted_iota(jnp.int32, work.shape, 0).astype(F32)
    rank = jnp.full(work.shape, float(rounds), F32)
    vals = []
    for r in range(rounds):
        mx = jnp.max(work, axis=0, keepdims=True)
        hit = work == mx
        if break_ties:
            first = jnp.min(jnp.where(hit, iota, float(rows)), axis=0, keepdims=True)
            hit = iota == first
        rank = jnp.where(hit, float(r), rank)
        work = jnp.where(hit, -jnp.inf, work)
        vals.append(mx)
    return vals, rank


def _exactly_k(rank, k):
    n = jnp.sum(jnp.where(rank < float(k), 1.0, 0.0), axis=0, keepdims=True)
    return jnp.where(n == float(k), 1.0, 0.0)


def _head_select(s1, s2, cnt_mat, cand_scr, break_ties):
    a1, rank1 = _take_top(s1, TOPK, break_ties)
    a2, rank2 = _take_top(s2, TOPK, break_ties)
    for pos, (i, j) in enumerate(_PAIRS):
        cand_scr[pos:pos + 1, :] = a1[i] + a2[j]
    cand = cand_scr[...]
    _, rank_c = _take_top(cand, TOPK, break_ties)
    sel = jnp.where(rank_c < float(TOPK), 1.0, 0.0)
    top = a1[0] + a2[0]
    z = jnp.sum(sel * jnp.exp(cand - top), axis=0, keepdims=True)
    cnt = jnp.dot(cnt_mat, sel.astype(BF16), preferred_element_type=F32)
    n1 = jnp.zeros(s1.shape, F32)
    for i in range(TOPK):
        n1 = jnp.where(rank1 == float(i), cnt[i:i + 1, :], n1)
    p1 = jnp.where(rank1 < float(TOPK), jnp.exp(s1 - a1[0]), 0.0)
    p2 = jnp.where(rank2 < float(TOPK), jnp.exp(s2 - a2[0]), 0.0) / z
    clean = _exactly_k(rank1, TOPK) * _exactly_k(rank2, TOPK) * _exactly_k(rank_c, TOPK)
    return n1, p1, rank2, p2, clean


def _peer_topk_kernel(q_ref, keys_ref, cnt_ref, n1_ref, p1_ref, r2_ref, p2_ref, cand_a, cand_b):
    nt = (((1,), (1,)), ((), ()))
    cand_a[...] = jnp.full(cand_a.shape, -jnp.inf, F32)
    cand_b[...] = jnp.full(cand_b.shape, -jnp.inf, F32)

    def scores(h):
        base = pl.multiple_of(h * 2 * DK_HALF, 2 * DK_HALF)
        q1 = q_ref[:, pl.ds(base, DK_HALF)]
        q2 = q_ref[:, pl.ds(base + DK_HALF, DK_HALF)]
        s1 = lax.dot_general(keys_ref[h, 0].astype(BF16), q1, nt, preferred_element_type=F32)
        s2 = lax.dot_general(keys_ref[h, 1].astype(BF16), q2, nt, preferred_element_type=F32)
        return s1, s2

    def write(h, res):
        n1_ref[h], p1_ref[h], r2_ref[h], p2_ref[h] = res[:4]

    def head_pair(hp, carry):
        heads = (2 * hp, 2 * hp + 1)
        scr = (cand_a, cand_b)
        sc = [scores(h) for h in heads]
        res = [_head_select(*sc[n], cnt_ref[...], scr[n], False) for n in range(2)]
        for n in range(2):
            write(heads[n], res[n])
        for n in range(2):
            @pl.when(jnp.min(res[n][4]) < 0.5)
            def _():
                write(heads[n], _head_select(*sc[n], cnt_ref[...], scr[n], True))
        return carry

    lax.fori_loop(0, PEER_HEADS // 2, head_pair, 0)


def _pair_count_matrix():
    return jnp.array([[1.0 if i == r else 0.0 for (i, _) in _PAIRS]
                      + [0.0] * (_NPAIR_PAD - len(_PAIRS)) for r in range(TOPK)], BF16)


def _peer_topk(q, keys, tt):
    m = q.shape[0]
    cnt = _pair_count_matrix()
    out = pl.BlockSpec((PEER_HEADS, N_KEYS, tt), lambda i: (0, 0, i))
    return pl.pallas_call(
        _peer_topk_kernel,
        grid=(m // tt,),
        in_specs=[pl.BlockSpec((tt, q.shape[1]), lambda i: (i, 0)),
                  pl.BlockSpec(keys.shape, lambda i: (0, 0, 0, 0)),
                  pl.BlockSpec(cnt.shape, lambda i: (0, 0))],
        out_specs=[out] * 4,
        out_shape=[jax.ShapeDtypeStruct((PEER_HEADS, N_KEYS, m), F32)] * 4,
        scratch_shapes=[pltpu.VMEM((_NPAIR_PAD, tt), F32)] * 2,
        compiler_params=_cparams("arbitrary"),
        name="peer_topk",
    )(q, keys, cnt)


def _peer_dense_kernel(h_ref, u_ref, vt_ref, n1_ref, p1_ref, r2_ref, p2_ref, o_ref, w_scr, g_scr,
                       *, et, n_et):
    n = pl.program_id(0)
    cur = jnp.minimum(n, pl.num_programs(0) - 2)
    e_cur = cur % n_et
    e_prev = jnp.maximum(n - 1, 0) % n_et
    nt = (((1,), (1,)), ((), ()))

    @pl.when(n == 0)
    def _():
        w_scr[1] = jnp.zeros(w_scr.shape[1:], BF16)

    @pl.when(e_prev == 0)
    def _():
        o_ref[...] = jnp.zeros(o_ref.shape, F32)

    for j in range(et // N_KEYS):
        e1 = e_cur * (et // N_KEYS) + j
        n1 = n1_ref[e1]
        p1 = p1_ref[e1]
        gate = jnp.zeros((N_KEYS, g_scr.shape[1]), F32)
        for h in range(PEER_HEADS):
            gate = gate + jnp.where(r2_ref[h] < n1[h:h + 1, :], p2_ref[h], 0.0) * p1[h:h + 1, :]
        g_scr[j * N_KEYS:(j + 1) * N_KEYS, :] = gate
    s = lax.dot_general(u_ref[...], h_ref[...], nt, preferred_element_type=F32)
    o_ref[...] += jnp.dot(vt_ref[...], w_scr[(n + 1) % 2], preferred_element_type=F32)
    w_scr[n % 2] = (_gelu(s) * g_scr[...]).astype(BF16)


def _peer_dense(h2, u_tab, vt_tab, n1, p1, r2, p2, tt, et):
    m, d = h2.shape
    n_et = u_tab.shape[0] // et
    steps = (m // tt) * n_et

    def cur(fn):
        return lambda n: fn(jnp.minimum(n, steps - 1))

    def prev(fn):
        return lambda n: fn(jnp.maximum(n - 1, 0))

    once = pl.Buffered(1)
    sel_a = pl.BlockSpec((N_KEYS, PEER_HEADS, tt), cur(lambda c: (0, 0, c // n_et)),
                         pipeline_mode=once)
    sel_b = pl.BlockSpec((PEER_HEADS, N_KEYS, tt), cur(lambda c: (0, 0, c // n_et)),
                         pipeline_mode=once)
    return pl.pallas_call(
        functools.partial(_peer_dense_kernel, et=et, n_et=n_et),
        grid=(steps + 1,),
        in_specs=[pl.BlockSpec((tt, d), cur(lambda c: (c // n_et, 0))),
                  pl.BlockSpec((et, d), cur(lambda c: (c % n_et, 0))),
                  pl.BlockSpec((d, et), prev(lambda c: (0, c % n_et))),
                  sel_a, sel_a, sel_b, sel_b],
        out_specs=pl.BlockSpec((d, tt), prev(lambda c: (0, c // n_et))),
        out_shape=jax.ShapeDtypeStruct((d, m), F32),
        scratch_shapes=[pltpu.VMEM((2, et, tt), BF16), pltpu.VMEM((et, tt), F32)],
        compiler_params=_cparams("arbitrary"),
        name="peer_dense",
    )(h2, u_tab, vt_tab, n1, p1, r2, p2)


def _final_kernel(x1_ref, ft_ref, gt_ref, g_ref, b_ref, o_ref, *, alpha):
    f = ft_ref[...].T
    o_ref[...] = _layer_norm_rows(alpha * x1_ref[...] + gt_ref[...] * f,
                                  g_ref[...], b_ref[...], LN_EPS)


def _final_ln(x1, ft, first_token, mod, ln_g, ln_b, per_row, seq_len, alpha, tm):
    m, d = x1.shape
    col0 = first_token // tm
    row = pl.BlockSpec((tm, d), lambda i: (i, 0))
    vec = pl.BlockSpec((1, d), lambda i: (0, 0))
    return pl.pallas_call(
        functools.partial(_final_kernel, alpha=alpha),
        grid=(m // tm,),
        in_specs=[row, pl.BlockSpec((d, tm), lambda i: (0, col0 + i)),
                  _mod_spec(per_row, tm, d, 5, seq_len), vec, vec],
        out_specs=row,
        out_shape=jax.ShapeDtypeStruct((m, d), F32),
        compiler_params=_cparams("arbitrary"),
        name="final_ln2",
    )(x1, ft, mod, ln_g, ln_b)


def _pack_w_in(w_in, d, d_b):
    o_small = 3 * d
    o_u = o_small + SMALL
    o_v = o_u + d_b
    o_ga = o_v + d_b
    o_gb = o_ga + d
    small = jnp.pad(w_in[:, o_small:o_u], ((0, 0), (0, SMALL_PAD - SMALL)))
    return jnp.concatenate([w_in[:, :o_small], w_in[:, o_ga:o_gb], w_in[:, o_gb:o_gb + d],
                            w_in[:, o_u:o_v], w_in[:, o_v:o_ga], small], axis=1).astype(BF16)


def _dense_token_tile(m):
    for tt in (768, 512, 256):
        if (-m) % tt <= m // 20:
            return tt
    return LANES


def _split_shift(s, d):
    return s[:, :3 * d], jnp.pad(s[:, 3 * d:], ((0, 0), (0, SMALL_PAD - SMALL)))


def _join_shift(proj_rows, d):
    small_off = 6 * d
    return jnp.concatenate([proj_rows[:, :3 * d], proj_rows[:, small_off:small_off + SMALL]], axis=1)


def _rwkv_prompt(pre, bsz, t_len, heads, p, tc):
    r, w, k, v, kk, a, g = pre
    to_l = lambda x: x.reshape(bsz, t_len, heads, HEAD_A).transpose(1, 3, 0, 2).reshape(
        t_len, HEAD_A, bsz * heads)
    tile = lambda x: jnp.tile(x.reshape(heads, HEAD_A).T, (1, bsz))
    s0 = jnp.zeros((bsz * heads // LANES, HEAD_A, HEAD_A, LANES), F32)
    y, st = _scan([to_l(x) for x in (r, w, k, v, kk, a)], s0,
                  tile(p["lnx_g"]), tile(p["lnx_b"]), tile(p["r_k"]), tc)
    y = y.reshape(t_len, HEAD_A, bsz, heads).transpose(2, 0, 3, 1).reshape(bsz * t_len, -1)
    wkv = st.transpose(1, 2, 0, 3).reshape(HEAD_A, HEAD_A, bsz, heads).transpose(2, 3, 1, 0)
    return y, g, wkv


def _rwkv_sample(pre, nb, heads, wkv0, p):
    r, w, k, v, kk, a, g = pre
    to_l = lambda x: x.reshape(nb, heads, HEAD_A).transpose(2, 1, 0).reshape(1, HEAD_A, heads * nb)
    rep = lambda x: jnp.repeat(x.reshape(heads, HEAD_A).T, nb, axis=1)
    s0 = wkv0.transpose(1, 3, 2, 0)
    y, st = _scan([to_l(x) for x in (r, w, k, v, kk, a)], s0,
                  rep(p["lnx_g"]), rep(p["lnx_b"]), rep(p["r_k"]), 1)
    y = y.reshape(HEAD_A, heads, nb).transpose(2, 1, 0).reshape(nb, heads * HEAD_A)
    return y, g, st.transpose(3, 0, 2, 1)


def _layer(x_p, x_s, c_p, c_s, wkv_s, shift_s, p):
    bsz, t_len, d = x_p.shape
    nb = x_s.shape[0]
    heads = d // HEAD_A
    d_b = d // 2
    alpha = p["alpha"]
    mp = bsz * t_len

    lp = {
        "mu_main": p["mu_shift"][None, :3 * d],
        "mu_small": jnp.pad(p["mu_shift"][None, 3 * d:], ((0, 0), (0, SMALL_PAD - SMALL))),
        "w0": p["w0"][None], "w2": p["w2"], "a0": p["a0"][None], "a2": p["a2"], "g2": p["g2"],
        "k_k": p["k_k"][None], "k_a": p["k_a"][None],
        "sgu_ln_g": p["sgu_ln_g"][None], "sgu_ln_b": p["sgu_ln_b"][None], "sgu_w": p["sgu_w"],
        "sgu_b_full": jnp.broadcast_to(p["sgu_b"][:, :, None], p["sgu_b"].shape + (GROUP_B,)),
        "sgu_w_first": jnp.repeat(p["sgu_w"][:, 0, 0], GROUP_B)[None],
        "sgu_b_first": jnp.repeat(p["sgu_b"][:, 0], GROUP_B)[None],
    }
    ln1 = (p["ln1_g"][None], p["ln1_b"][None])
    ln2 = (p["ln2_g"][None], p["ln2_b"][None])

    c_all = jnp.concatenate([c_p, c_s], axis=0)
    mod = _modulation(jnp.pad(c_all, ((0, (-c_all.shape[0]) % SUBLANES), (0, 0))),
                      p["w_ada"], p["b_ada"])
    mod_p = mod[:bsz].reshape(bsz, 1, 6 * d)
    mod_s = mod[bsz:bsz + nb]

    w_pack = _pack_w_in(p["w_in"], d, d_b)
    tn_in = _pick(w_pack.shape[1], (1280, 512))
    xp2 = x_p.reshape(mp, d)
    xs2 = x_s.reshape(nb, d)
    tm_big = _pick(t_len, (1024, 512, 256, 128))
    tm_mid = _pick(t_len, (512, 256, 128))
    tm_small = _pick(t_len, (256, 128))
    proj_p = _inproj(xp2, mod_p, w_pack, False, t_len, tm_big, tn_in)
    proj_s = _inproj(xs2, mod_s, w_pack, True, 1, nb, tn_in)

    zero_state = jnp.zeros((bsz, 3 * d + SMALL), F32)
    zm, zs = _split_shift(zero_state, d)
    pre_p = _rwkv_pre(proj_p, zm[:, None, :], zs[:, None, :], lp, d, t_len, CHUNK)
    sm, ss = _split_shift(shift_s, d)
    pre_s = _rwkv_pre(proj_s, sm, ss, lp, d, 1, nb)
    ya_p, g_p, wkv_p = _rwkv_prompt(pre_p, bsz, t_len, heads, p, _pick(t_len, (64, 32, 16, 8)))
    ya_s, g_s, wkv_s_new = _rwkv_sample(pre_s, nb, heads, wkv_s, p)

    u_blk = (5 * d) // d_b
    yb_p = _sgu_prompt(proj_p, lp, d_b, u_blk, tm_mid)
    yb_s, vrows_s = _sgu_single(proj_s, lp, d_b, u_blk)

    wa = p["w_br_a"].astype(BF16)
    wb = p["w_br_b"].astype(BF16)
    wo = p["w_o"].astype(BF16)
    wq = p["peer_wq"].astype(BF16)
    tn_mrg = _pick(d, (512, 256, 128))
    mrg_p = _merge(ya_p, g_p, yb_p, proj_p, wa, wb, d, tm_mid, tn_mrg)
    mrg_s = _merge(ya_s, g_s, yb_s, proj_s, wa, wb, d, nb, tn_mrg)
    x1_p, h2_p, q_p = _out_ln(mrg_p, xp2, mod_p, wo, wq, *ln1, False, t_len, alpha, tm_small)
    x1_s, h2_s, q_s = _out_ln(mrg_s, xs2, mod_s, wo, wq, *ln1, True, 1, alpha, nb)

    m_all = mp + nb
    tt = _dense_token_tile(m_all)
    pad_rows = (-m_all) % tt
    h2 = jnp.concatenate([h2_p, h2_s, jnp.zeros((pad_rows, d), BF16)], axis=0)
    q = jnp.concatenate([q_p, q_s, jnp.zeros((pad_rows, q_p.shape[1]), BF16)], axis=0)
    n1, p1, r2, p2 = _peer_topk(q, p["peer_keys"], LANES)
    n1 = n1.transpose(1, 0, 2)
    p1 = p1.transpose(1, 0, 2)
    ft = _peer_dense(h2, p["peer_u"].astype(BF16), p["peer_v"].T.astype(BF16), n1, p1, r2, p2,
                     tt, 4 * N_KEYS)

    y_p = _final_ln(x1_p, ft, 0, mod_p, *ln2, False, t_len, alpha, tm_small)
    y_s = _final_ln(x1_s, ft, mp, mod_s, *ln2, True, 1, alpha, nb)

    last = proj_p.reshape(bsz, t_len, -1)[:, -1]
    return (y_p.reshape(bsz, t_len, d), y_s.reshape(nb, 1, d), wkv_p, _join_shift(last, d),
            wkv_s_new, _join_shift(proj_s, d), vrows_s.reshape(nb, 1, d_b))


def kernel(x_prompt, x_sample, c_prompt, c_sample, state_wkv, state_shift, w_ada, b_ada, w_in, mu_shift, w0, w2, a0, a2, g2, k_k, k_a, r_k, lnx_g, lnx_b, sgu_ln_g, sgu_ln_b, sgu_w, sgu_b, w_br_a, w_br_b, w_o, ln1_g, ln1_b, peer_wq, peer_keys, peer_u, peer_v, ln2_g, ln2_b):
    names = ("w_ada", "b_ada", "w_in", "mu_shift", "w0", "w2", "a0", "a2", "g2", "k_k", "k_a",
             "r_k", "lnx_g", "lnx_b", "sgu_ln_g", "sgu_ln_b", "sgu_w", "sgu_b", "w_br_a",
             "w_br_b", "w_o", "ln1_g", "ln1_b", "peer_wq", "peer_keys", "peer_u", "peer_v",
             "ln2_g", "ln2_b")
    stacked = (w_ada, b_ada, w_in, mu_shift, w0, w2, a0, a2, g2, k_k, k_a, r_k, lnx_g, lnx_b,
               sgu_ln_g, sgu_ln_b, sgu_w, sgu_b, w_br_a, w_br_b, w_o, ln1_g, ln1_b, peer_wq,
               peer_keys, peer_u, peer_v, ln2_g, ln2_b)
    depth = w_ada.shape[0]
    alpha = (2 * depth) ** 0.25
    y_p, y_s = x_prompt, x_sample
    outs = [[] for _ in range(5)]
    for l in range(depth):
        p = {n: a[l] for n, a in zip(names, stacked)}
        p["alpha"] = alpha
        y_p, y_s, *state = _layer(y_p, y_s, c_prompt, c_sample, state_wkv[l], state_shift[l], p)
        for acc, s in zip(outs, state):
            acc.append(s)
    return (y_p, y_s) + tuple(jnp.stack(o) for o in outs)
```

```python
import functools

import jax
import jax.numpy as jnp
from jax import lax
from jax.experimental import pallas as pl
from jax.experimental.pallas import tpu as pltpu

F32 = jnp.float32
BF16 = jnp.bfloat16

HEAD_A = 64
R_DECAY = 96
R_AAA = 96
R_GATE = 256
SMALL = R_DECAY + R_AAA + R_GATE
SMALL_PAD = 512
CHUNK = 128
GROUP_B = 128
PEER_HEADS = 8
N_KEYS = 128
DK_HALF = 128
TOPK = 16
LN_EPS = 1e-5
GN_EPS = 64e-5

LANES = 128
SUBLANES = 8
VMEM_LIMIT = 56 * 1024 * 1024


def _cparams(*sem):
    return pltpu.CompilerParams(dimension_semantics=sem, vmem_limit_bytes=VMEM_LIMIT)


def _gelu(x):
    return jax.nn.gelu(x, approximate=True)


def _layer_norm_rows(x, g, b, eps):
    mu = jnp.mean(x, axis=-1, keepdims=True)
    xc = x - mu
    var = jnp.mean(xc * xc, axis=-1, keepdims=True)
    return xc * lax.rsqrt(var + eps) * g + b


def _pick(n, prefs):
    for p in prefs:
        if n % p == 0:
            return p
    return n


def _mod_spec(per_row, tm, d, chunk, rows_per_seq):
    if per_row:
        return pl.BlockSpec((tm, d), lambda i, *_: (i, chunk))
    return pl.BlockSpec((None, 1, d), lambda i, *_: ((i * tm) // rows_per_seq, 0, chunk))


def _mod_kernel(c_ref, w_ref, b_ref, o_ref):
    c = c_ref[...]
    s = (c * jax.nn.sigmoid(c)).astype(BF16)
    o_ref[...] = jnp.dot(s, w_ref[...].astype(BF16), preferred_element_type=F32) + b_ref[...]


def _modulation(c_all, w_ada, b_ada):
    m, d = c_all.shape
    n = w_ada.shape[1]
    tn = _pick(n, (1536, 512))
    return pl.pallas_call(
        _mod_kernel,
        grid=(n // tn,),
        in_specs=[pl.BlockSpec((m, d), lambda j: (0, 0)),
                  pl.BlockSpec((d, tn), lambda j: (0, j)),
                  pl.BlockSpec((1, tn), lambda j: (0, j))],
        out_specs=pl.BlockSpec((m, tn), lambda j: (0, j)),
        out_shape=jax.ShapeDtypeStruct((m, n), F32),
        compiler_params=_cparams("arbitrary"),
        name="adaln_mod",
    )(c_all, w_ada, b_ada.reshape(1, n))


def _inproj_kernel(x_ref, sh_ref, sc_ref, w_ref, o_ref, h_scr):
    @pl.when(pl.program_id(1) == 0)
    def _():
        h_scr[...] = (x_ref[...] * (1.0 + sc_ref[...]) + sh_ref[...]).astype(BF16)

    o_ref[...] = jnp.dot(h_scr[...], w_ref[...], preferred_element_type=F32)


def _inproj(x, mod, w_pack, per_row, seq_len, tm, tn):
    m, d = x.shape
    n = w_pack.shape[1]
    return pl.pallas_call(
        _inproj_kernel,
        grid=(m // tm, n // tn),
        in_specs=[pl.BlockSpec((tm, d), lambda i, j: (i, 0)),
                  _mod_spec(per_row, tm, d, 0, seq_len),
                  _mod_spec(per_row, tm, d, 1, seq_len),
                  pl.BlockSpec((d, tn), lambda i, j: (0, j))],
        out_specs=pl.BlockSpec((tm, tn), lambda i, j: (i, j)),
        out_shape=jax.ShapeDtypeStruct((m, n), F32),
        scratch_shapes=[pltpu.VMEM((tm, d), BF16)],
        compiler_params=_cparams("arbitrary", "arbitrary"),
        name="in_proj",
    )(x, mod, mod, w_pack)


def _swap_lane_groups(xs, width):
    n = len(xs)
    grp = lax.broadcasted_iota(jnp.int32, xs[0].shape, 1) // width
    moved = []
    for s in range(n):
        acc = xs[0]
        for b in range(1, n):
            acc = jnp.where(grp == (b - s) % n, xs[b], acc)
        moved.append(acc if s == 0 else pltpu.roll(acc, s * width, 1))
    outs = []
    for q in range(n):
        acc = moved[0]
        for s in range(1, n):
            acc = jnp.where(grp == (q + s) % n, moved[s], acc)
        outs.append(acc)
    return outs


def _to_planes(xs, o_ref, heads):
    per = LANES // heads
    for j in range(xs[0].shape[1] // LANES):
        outs = _swap_lane_groups([x[:, j * LANES:(j + 1) * LANES] for x in xs], heads)
        for q in range(per):
            o_ref[j * per + q] = outs[q]


def _from_planes_kernel(y_ref, o_ref, *, heads):
    per = LANES // heads
    for j in range(y_ref.shape[0] // per):
        outs = _swap_lane_groups([y_ref[j * per + q] for q in range(per)], heads)
        for b in range(per):
            o_ref[b, :, j * LANES:(j + 1) * LANES] = outs[b]


def _from_planes(y, nseq, heads, tm):
    n, t_len, _ = y.shape
    d = n * heads
    return pl.pallas_call(
        functools.partial(_from_planes_kernel, heads=heads),
        grid=(t_len // tm,),
        in_specs=[pl.BlockSpec((n, tm, LANES), lambda i: (0, i, 0))],
        out_specs=pl.BlockSpec((nseq, tm, d), lambda i: (0, i, 0)),
        out_shape=jax.ShapeDtypeStruct((nseq, t_len, d), F32),
        compiler_params=_cparams("arbitrary"),
        name="rwkv_from_planes",
    )(y)


def _rwkv_pre_kernel(pm_ref, ps_ref, qm_ref, qs_ref, stm_ref, sts_ref, mum_ref, mus_ref,
                     w0_ref, w2_ref, a0_ref, a2_ref, g2_ref, kk_ref, ka_ref,
                     r_o, w_o, k_o, v_o, kk_o, a_o, g_o, *, d, heads, per_row_state):
    nseq, tm, _ = pm_ref.shape

    def shifted(p, q, st):
        if per_row_state:
            return st
        first = pl.program_id(0) == 0
        prev_row = jnp.where(first, st, q[SUBLANES - 1:SUBLANES, :])
        row = lax.broadcasted_iota(jnp.int32, p.shape, 0)
        return jnp.where(row == 0, prev_row, pltpu.roll(p, 1, 0))

    def mix(p, q, st, mu):
        return p + (shifted(p, q, st) - p) * mu

    seqs = range(nseq)
    xs_s = jnp.concatenate([mix(ps_ref[b], qs_ref[b], sts_ref[b], mus_ref[...]) for b in seqs],
                           axis=0)
    wl = xs_s[:, 0:R_DECAY]
    al = xs_s[:, R_DECAY:R_DECAY + R_AAA]
    gl = xs_s[:, R_DECAY + R_AAA:SMALL]
    wlin = w0_ref[...] + jnp.dot(jnp.tanh(wl).astype(BF16), w2_ref[...].astype(BF16),
                                 preferred_element_type=F32)
    w = -jax.nn.softplus(-wlin) - 0.5
    decay = jnp.exp(-jnp.exp(w))
    a = jax.nn.sigmoid(a0_ref[...] + jnp.dot(al.astype(BF16), a2_ref[...].astype(BF16),
                                             preferred_element_type=F32))
    g = jnp.dot(jax.nn.sigmoid(gl).astype(BF16), g2_ref[...].astype(BF16),
                preferred_element_type=F32)

    def of_seq(x, b):
        return x[b * tm:(b + 1) * tm]

    def part(n, b):
        cols = slice(n * d, (n + 1) * d)
        return mix(pm_ref[b, :, cols], qm_ref[b, :, cols], stm_ref[b, :, cols], mum_ref[:, cols])

    def emit(o_ref, xs):
        if per_row_state:
            for b in seqs:
                o_ref[b] = xs[b]
        else:
            _to_planes(xs, o_ref, heads)

    for b in seqs:
        g_o[b] = of_seq(g, b)
    emit(r_o, [part(0, b) for b in seqs])
    emit(w_o, [of_seq(decay, b) for b in seqs])
    ks = [part(1, b) for b in seqs]
    emit(kk_o, [ks[b] * kk_ref[...] for b in seqs])
    emit(k_o, [ks[b] * (1.0 + (of_seq(a, b) - 1.0) * ka_ref[...]) for b in seqs])
    emit(v_o, [part(2, b) for b in seqs])
    emit(a_o, [of_seq(a, b) for b in seqs])


def _rwkv_pre(proj, state_main, state_small, lp, d, heads, per_row_state, tm):
    nseq, t_len, _ = proj.shape
    small_blk = (6 * d) // SMALL_PAD

    def prev_map(col):
        def f(i):
            return (0, jnp.maximum(i * (tm // SUBLANES) - 1, 0), col)
        return f

    if per_row_state:
        stm_spec = pl.BlockSpec((nseq, tm, 3 * d), lambda i: (0, i, 0))
        sts_spec = pl.BlockSpec((nseq, tm, SMALL_PAD), lambda i: (0, i, 0))
    else:
        stm_spec = pl.BlockSpec((nseq, 1, 3 * d), lambda i: (0, 0, 0))
        sts_spec = pl.BlockSpec((nseq, 1, SMALL_PAD), lambda i: (0, 0, 0))

    vec = lambda w: pl.BlockSpec((1, w), lambda i: (0, 0))
    full = lambda a: pl.BlockSpec(a.shape, lambda i: (0,) * a.ndim)
    rows = pl.BlockSpec((nseq, tm, d), lambda i: (0, i, 0))
    rows_shape = jax.ShapeDtypeStruct((nseq, t_len, d), F32)
    if per_row_state:
        vec_out, vec_shape = rows, rows_shape
    else:
        vec_out = pl.BlockSpec((HEAD_A, tm, LANES), lambda i: (0, i, 0))
        vec_shape = jax.ShapeDtypeStruct((HEAD_A, t_len, LANES), F32)
    kern = functools.partial(_rwkv_pre_kernel, d=d, heads=heads, per_row_state=per_row_state)
    in_specs = [pl.BlockSpec((nseq, tm, 3 * d), lambda i: (0, i, 0)),
                pl.BlockSpec((nseq, tm, SMALL_PAD), lambda i: (0, i, small_blk)),
                pl.BlockSpec((nseq, SUBLANES, 3 * d), prev_map(0)),
                pl.BlockSpec((nseq, SUBLANES, SMALL_PAD), prev_map(small_blk)),
                stm_spec, sts_spec, vec(3 * d), vec(SMALL_PAD), vec(d), full(lp["w2"]), vec(d),
                full(lp["a2"]), full(lp["g2"]), vec(d), vec(d)]
    return pl.pallas_call(
        kern,
        grid=(t_len // tm,),
        in_specs=in_specs,
        out_specs=[vec_out] * 6 + [rows],
        out_shape=[vec_shape] * 6 + [rows_shape],
        compiler_params=_cparams("arbitrary"),
        name="rwkv_pre",
    )(proj, proj, proj, proj, state_main, state_small, lp["mu_main"], lp["mu_small"], lp["w0"],
      lp["w2"], lp["a0"], lp["a2"], lp["g2"], lp["k_k"], lp["k_a"])


def _scan_kernel(r_ref, w_ref, k_ref, v_ref, kk_ref, a_ref, s0_ref, lng_ref, lnb_ref, rk_ref,
                 y_ref, st_ref, s_scr, aa_scr, bb_scr, bon_scr, yt_scr, *, tc):
    n = HEAD_A
    c = pl.program_id(1)

    @pl.when(c == 0)
    def _():
        s_scr[...] = s0_ref[...]

    kk = kk_ref[...]
    nrm = jnp.sqrt(jnp.sum(kk * kk, axis=0, keepdims=True))
    kkn = kk / jnp.maximum(nrm, 1e-12)
    aa_scr[...] = -kkn
    bb_scr[...] = kkn * a_ref[...]
    bon_scr[...] = jnp.sum(r_ref[...] * k_ref[...] * rk_ref[...], axis=0)

    zeros = jnp.zeros((n, LANES), F32)
    nblk = n // SUBLANES
    sub = lax.broadcasted_iota(jnp.int32, (SUBLANES, LANES), 0)

    def row(ref, i, t):
        return ref[i, pl.ds(t, 1), :]

    def tile_of_step(ref, t):
        blocks = []
        for m in range(nblk):
            acc = jnp.broadcast_to(row(ref, m * SUBLANES, t), (SUBLANES, LANES))
            for s in range(1, SUBLANES):
                acc = jnp.where(sub == s, row(ref, m * SUBLANES + s, t), acc)
            blocks.append(acc)
        return jnp.concatenate(blocks, axis=0)

    def sa_first(kb, acc):
        for j in range(SUBLANES):
            i = kb * SUBLANES + j
            acc = acc + s_scr[i] * row(aa_scr, i, 0)
        return acc

    sa0 = lax.fori_loop(0, nblk, sa_first, zeros)

    lng = lng_ref[...]
    lnb = lnb_ref[...]

    def step(t, sa):
        tn = jnp.minimum(t + 1, tc - 1)
        vt = tile_of_step(v_ref, t)

        def kblock(kb, carry):
            y, san = carry
            for j in range(SUBLANES):
                i = kb * SUBLANES + j
                sn = s_scr[i] * row(w_ref, i, t) + sa * row(bb_scr, i, t) + vt * row(k_ref, i, t)
                s_scr[i] = sn
                y = y + sn * row(r_ref, i, t)
                san = san + sn * row(aa_scr, i, tn)
            return y, san

        y, san = lax.fori_loop(0, nblk, kblock, (zeros, zeros))
        mu = jnp.mean(y, axis=0, keepdims=True)
        yc = y - mu
        var = jnp.mean(yc * yc, axis=0, keepdims=True)
        yn = yc * lax.rsqrt(var + GN_EPS) * lng + lnb
        return san, yn + bon_scr[pl.ds(t, 1), :] * vt

    grp = min(SUBLANES, tc)

    def group(gb, sa):
        t0 = pl.multiple_of(gb * grp, grp)

        def one(s, sa):
            sa, out = step(t0 + s, sa)
            yt_scr[s] = out
            return sa

        sa = lax.fori_loop(0, grp, one, sa)
        for i in range(n):
            acc = jnp.broadcast_to(yt_scr[0, i:i + 1, :], (grp, LANES))
            for s in range(1, grp):
                acc = jnp.where(sub[:grp] == s, yt_scr[s, i:i + 1, :], acc)
            y_ref[i, pl.ds(t0, grp), :] = acc
        return sa

    lax.fori_loop(0, tc // grp, group, sa0)

    @pl.when(c == pl.num_programs(1) - 1)
    def _():
        st_ref[...] = s_scr[...]


def _scan(seq, s0, lng, lnb, rk, tc):
    n, t_len, gl = seq[0].shape
    g = gl // LANES
    step_spec = pl.BlockSpec((n, tc, LANES), lambda gi, c: (0, c, gi))
    par_spec = pl.BlockSpec((n, LANES), lambda gi, c: (0, gi))
    rk_spec = pl.BlockSpec((n, 1, LANES), lambda gi, c: (0, 0, gi))
    st_spec = pl.BlockSpec((None, n, n, LANES), lambda gi, c: (gi, 0, 0, 0))
    return pl.pallas_call(
        functools.partial(_scan_kernel, tc=tc),
        grid=(g, t_len // tc),
        in_specs=[step_spec] * 6 + [st_spec, par_spec, par_spec, rk_spec],
        out_specs=[step_spec, st_spec],
        out_shape=[jax.ShapeDtypeStruct((n, t_len, gl), F32),
                   jax.ShapeDtypeStruct((g, n, n, LANES), F32)],
        scratch_shapes=[pltpu.VMEM((n, n, LANES), F32),
                        pltpu.VMEM((n, tc, LANES), F32),
                        pltpu.VMEM((n, tc, LANES), F32),
                        pltpu.VMEM((tc, LANES), F32),
                        pltpu.VMEM((min(SUBLANES, tc), n, LANES), F32)],
        compiler_params=_cparams("arbitrary", "arbitrary"),
        name="rwkv_scan",
    )(*seq, s0, lng, lnb, rk)


def _sgu_kernel(pu_ref, pv_ref, lg_ref, lb_ref, w_ref, b_ref, y_ref, *, chunks):
    ng = w_ref.shape[0]
    row = lax.broadcasted_iota(jnp.int32, (CHUNK, CHUNK), 0)
    col = lax.broadcasted_iota(jnp.int32, (CHUNK, CHUNK), 1)
    causal = col <= row
    for ci in range(chunks):
        rows = slice(ci * CHUNK, (ci + 1) * CHUNK)
        u = _gelu(pu_ref[rows, :])
        v = _layer_norm_rows(_gelu(pv_ref[rows, :]), lg_ref[...], lb_ref[...], LN_EPS)
        for g in range(ng):
            cols = slice(g * GROUP_B, (g + 1) * GROUP_B)
            ws = jnp.where(causal, w_ref[g], 0.0).astype(BF16)
            mixed = jnp.dot(ws, v[:, cols].astype(BF16), preferred_element_type=F32) + b_ref[g]
            y_ref[rows, cols] = (u[:, cols] * mixed).astype(y_ref.dtype)


def _sgu_prompt(proj, lp, d_b, u_blk, tm):
    m = proj.shape[0]
    ng = d_b // GROUP_B
    return pl.pallas_call(
        functools.partial(_sgu_kernel, chunks=tm // CHUNK),
        grid=(m // tm,),
        in_specs=[pl.BlockSpec((tm, d_b), lambda i: (i, u_blk)),
                  pl.BlockSpec((tm, d_b), lambda i: (i, u_blk + 1)),
                  pl.BlockSpec((1, d_b), lambda i: (0, 0)),
                  pl.BlockSpec((1, d_b), lambda i: (0, 0)),
                  pl.BlockSpec((ng, CHUNK, CHUNK), lambda i: (0, 0, 0)),
                  pl.BlockSpec((ng, CHUNK, GROUP_B), lambda i: (0, 0, 0))],
        out_specs=pl.BlockSpec((tm, d_b), lambda i: (i, 0)),
        out_shape=jax.ShapeDtypeStruct((m, d_b), BF16),
        compiler_params=_cparams("arbitrary"),
        name="sgu_chunked",
    )(proj, proj, lp["sgu_ln_g"], lp["sgu_ln_b"], lp["sgu_w"], lp["sgu_b_full"])


def _sgu_first_kernel(pu_ref, pv_ref, lg_ref, lb_ref, wd_ref, bd_ref, y_ref, v_ref):
    u = _gelu(pu_ref[...])
    v = _layer_norm_rows(_gelu(pv_ref[...]), lg_ref[...], lb_ref[...], LN_EPS)
    v_ref[...] = v
    vb = v.astype(BF16).astype(F32)
    wd = wd_ref[...].astype(BF16).astype(F32)
    y_ref[...] = (u * (vb * wd + bd_ref[...])).astype(y_ref.dtype)


def _sgu_single(proj, lp, d_b, u_blk):
    m = proj.shape[0]
    vec = pl.BlockSpec((1, d_b), lambda i: (0, 0))
    return pl.pallas_call(
        _sgu_first_kernel,
        grid=(1,),
        in_specs=[pl.BlockSpec((m, d_b), lambda i: (0, u_blk)),
                  pl.BlockSpec((m, d_b), lambda i: (0, u_blk + 1)), vec, vec, vec, vec],
        out_specs=[pl.BlockSpec((m, d_b), lambda i: (0, 0))] * 2,
        out_shape=[jax.ShapeDtypeStruct((m, d_b), BF16), jax.ShapeDtypeStruct((m, d_b), F32)],
        compiler_params=_cparams("arbitrary"),
        name="sgu_single",
    )(proj, proj, lp["sgu_ln_g"], lp["sgu_ln_b"], lp["sgu_w_first"], lp["sgu_b_first"])


def _merge_kernel(ya_ref, g_ref, yb_ref, ga_ref, gb_ref, wa_ref, wb_ref, o_ref, a_scr):
    @pl.when(pl.program_id(1) == 0)
    def _():
        a_scr[...] = (ya_ref[...] * g_ref[...]).astype(BF16)

    ta = jnp.dot(a_scr[...], wa_ref[...], preferred_element_type=F32)
    tb = jnp.dot(yb_ref[...], wb_ref[...], preferred_element_type=F32)
    o_ref[...] = (jax.nn.sigmoid(ga_ref[...]) * ta
                  + jax.nn.sigmoid(gb_ref[...]) * tb).astype(o_ref.dtype)


def _merge(ya, g, yb, proj, wa, wb, d, tm, tn):
    m = ya.shape[0]
    d_b = yb.shape[1]
    ga_blk = (3 * d) // tn
    gb_blk = (4 * d) // tn
    return pl.pallas_call(
        _merge_kernel,
        grid=(m // tm, d // tn),
        in_specs=[pl.BlockSpec((tm, d), lambda i, j: (i, 0)),
                  pl.BlockSpec((tm, d), lambda i, j: (i, 0)),
                  pl.BlockSpec((tm, d_b), lambda i, j: (i, 0)),
                  pl.BlockSpec((tm, tn), lambda i, j: (i, ga_blk + j)),
                  pl.BlockSpec((tm, tn), lambda i, j: (i, gb_blk + j)),
                  pl.BlockSpec((d, tn), lambda i, j: (0, j)),
                  pl.BlockSpec((d_b, tn), lambda i, j: (0, j))],
        out_specs=pl.BlockSpec((tm, tn), lambda i, j: (i, j)),
        out_shape=jax.ShapeDtypeStruct((m, d), BF16),
        scratch_shapes=[pltpu.VMEM((tm, d), BF16)],
        compiler_params=_cparams("arbitrary", "arbitrary"),
        name="branch_merge",
    )(ya, g, yb, proj, proj, wa, wb)


def _out_ln_kernel(m_ref, x_ref, gt_ref, sh_ref, sc_ref, wo_ref, wq_ref, g_ref, b_ref,
                   x1_ref, h2_ref, q_ref, *, alpha):
    z = jnp.dot(m_ref[...], wo_ref[...], preferred_element_type=F32)
    x1 = _layer_norm_rows(alpha * x_ref[...] + gt_ref[...] * z, g_ref[...], b_ref[...], LN_EPS)
    x1_ref[...] = x1
    h2 = (x1 * (1.0 + sc_ref[...]) + sh_ref[...]).astype(BF16)
    h2_ref[...] = h2
    q_ref[...] = jnp.dot(h2, wq_ref[...], preferred_element_type=F32).astype(q_ref.dtype)


def _out_ln(mrg, x, mod, wo, wq, ln_g, ln_b, per_row, seq_len, alpha, tm):
    m, d = x.shape
    nq = wq.shape[1]
    vec = pl.BlockSpec((1, d), lambda i: (0, 0))
    row = pl.BlockSpec((tm, d), lambda i: (i, 0))
    return pl.pallas_call(
        functools.partial(_out_ln_kernel, alpha=alpha),
        grid=(m // tm,),
        in_specs=[row, row,
                  _mod_spec(per_row, tm, d, 2, seq_len),
                  _mod_spec(per_row, tm, d, 3, seq_len),
                  _mod_spec(per_row, tm, d, 4, seq_len),
                  pl.BlockSpec((d, d), lambda i: (0, 0)),
                  pl.BlockSpec((d, nq), lambda i: (0, 0)), vec, vec],
        out_specs=[row, row, pl.BlockSpec((tm, nq), lambda i: (i, 0))],
        out_shape=[jax.ShapeDtypeStruct((m, d), F32), jax.ShapeDtypeStruct((m, d), BF16),
                   jax.ShapeDtypeStruct((m, nq), BF16)],
        compiler_params=_cparams("arbitrary"),
        name="out_proj_ln1",
    )(mrg, x, mod, mod, mod, wo, wq, ln_g, ln_b)


_PAIRS = [(i, j) for i in range(TOPK) for j in range(TOPK) if (i + 1) * (j + 1) <= TOPK]
_NPAIR_PAD = -(-len(_PAIRS) // SUBLANES) * SUBLANES


def _take_top(work, rounds, break_ties):
    rows = work.shape[0]
    iota = lax.broadcasted_iota(jnp.int32, work.shape, 0).astype(F32)
    rank = jnp.full(work.shape, float(rounds), F32)
    vals = []
    for r in range(rounds):
        mx = jnp.max(work, axis=0, keepdims=True)
        hit = work == mx
        if break_ties:
            first = jnp.min(jnp.where(hit, iota, float(rows)), axis=0, keepdims=True)
            hit = iota == first
        rank = jnp.where(hit, float(r), rank)
        work = jnp.where(hit, -jnp.inf, work)
        vals.append(mx)
    return vals, rank


def _exactly_k(rank, k):
    n = jnp.sum(jnp.where(rank < float(k), 1.0, 0.0), axis=0, keepdims=True)
    return jnp.where(n == float(k), 1.0, 0.0)


def _head_select(s1, s2, cnt_mat, cand_scr, break_ties):
    a1, rank1 = _take_top(s1, TOPK, break_ties)
    a2, rank2 = _take_top(s2, TOPK, break_ties)
    for pos, (i, j) in enumerate(_PAIRS):
        cand_scr[pos:pos + 1, :] = a1[i] + a2[j]
    cand = cand_scr[...]
    _, rank_c = _take_top(cand, TOPK, break_ties)
    sel = jnp.where(rank_c < float(TOPK), 1.0, 0.0)
    top = a1[0] + a2[0]
    z = jnp.sum(sel * jnp.exp(cand - top), axis=0, keepdims=True)
    cnt = jnp.dot(cnt_mat, sel.astype(BF16), preferred_element_type=F32)
    n1 = jnp.zeros(s1.shape, F32)
    for i in range(TOPK):
        n1 = jnp.where(rank1 == float(i), cnt[i:i + 1, :], n1)
    p1 = jnp.where(rank1 < float(TOPK), jnp.exp(s1 - a1[0]), 0.0)
    p2 = jnp.where(rank2 < float(TOPK), jnp.exp(s2 - a2[0]), 0.0) / z
    clean = _exactly_k(rank1, TOPK) * _exactly_k(rank2, TOPK) * _exactly_k(rank_c, TOPK)
    return n1, p1, rank2, p2, clean


def _peer_topk_kernel(q_ref, keys_ref, cnt_ref, n1_ref, p1_ref, r2_ref, p2_ref, cand_a, cand_b):
    nt = (((1,), (1,)), ((), ()))
    cand_a[...] = jnp.full(cand_a.shape, -jnp.inf, F32)
    cand_b[...] = jnp.full(cand_b.shape, -jnp.inf, F32)

    def scores(h):
        base = pl.multiple_of(h * 2 * DK_HALF, 2 * DK_HALF)
        q1 = q_ref[:, pl.ds(base, DK_HALF)]
        q2 = q_ref[:, pl.ds(base + DK_HALF, DK_HALF)]
        s1 = lax.dot_general(keys_ref[h, 0].astype(BF16), q1, nt, preferred_element_type=F32)
        s2 = lax.dot_general(keys_ref[h, 1].astype(BF16), q2, nt, preferred_element_type=F32)
        return s1, s2

    def write(h, res):
        n1_ref[h], p1_ref[h], r2_ref[h], p2_ref[h] = res[:4]

    def head_pair(hp, carry):
        heads = (2 * hp, 2 * hp + 1)
        scr = (cand_a, cand_b)
        sc = [scores(h) for h in heads]
        res = [_head_select(*sc[n], cnt_ref[...], scr[n], False) for n in range(2)]
        for n in range(2):
            write(heads[n], res[n])
        for n in range(2):
            @pl.when(jnp.min(res[n][4]) < 0.5)
            def _():
                write(heads[n], _head_select(*sc[n], cnt_ref[...], scr[n], True))
        return carry

    lax.fori_loop(0, PEER_HEADS // 2, head_pair, 0)


def _pair_count_matrix():
    return jnp.array([[1.0 if i == r else 0.0 for (i, _) in _PAIRS]
                      + [0.0] * (_NPAIR_PAD - len(_PAIRS)) for r in range(TOPK)], BF16)


def _peer_topk(q, keys, tt):
    m = q.shape[0]
    cnt = _pair_count_matrix()
    out = pl.BlockSpec((PEER_HEADS, N_KEYS, tt), lambda i: (0, 0, i))
    return pl.pallas_call(
        _peer_topk_kernel,
        grid=(m // tt,),
        in_specs=[pl.BlockSpec((tt, q.shape[1]), lambda i: (i, 0)),
                  pl.BlockSpec(keys.shape, lambda i: (0, 0, 0, 0)),
                  pl.BlockSpec(cnt.shape, lambda i: (0, 0))],
        out_specs=[out] * 4,
        out_shape=[jax.ShapeDtypeStruct((PEER_HEADS, N_KEYS, m), F32)] * 4,
        scratch_shapes=[pltpu.VMEM((_NPAIR_PAD, tt), F32)] * 2,
        compiler_params=_cparams("arbitrary"),
        name="peer_topk",
    )(q, keys, cnt)


def _peer_dense_kernel(h_ref, u_ref, vt_ref, n1_ref, p1_ref, r2_ref, p2_ref, o_ref, w_scr, g_scr,
                       *, et, n_et):
    n = pl.program_id(0)
    cur = jnp.minimum(n, pl.num_programs(0) - 2)
    e_cur = cur % n_et
    e_prev = jnp.maximum(n - 1, 0) % n_et
    nt = (((1,), (1,)), ((), ()))

    @pl.when(n == 0)
    def _():
        w_scr[1] = jnp.zeros(w_scr.shape[1:], BF16)

    @pl.when(e_prev == 0)
    def _():
        o_ref[...] = jnp.zeros(o_ref.shape, F32)

    for j in range(et // N_KEYS):
        e1 = e_cur * (et // N_KEYS) + j
        n1 = n1_ref[e1]
        p1 = p1_ref[e1]
        gate = jnp.zeros((N_KEYS, g_scr.shape[1]), F32)
        for h in range(PEER_HEADS):
            gate = gate + jnp.where(r2_ref[h] < n1[h:h + 1, :], p2_ref[h], 0.0) * p1[h:h + 1, :]
        g_scr[j * N_KEYS:(j + 1) * N_KEYS, :] = gate
    s = lax.dot_general(u_ref[...], h_ref[...], nt, preferred_element_type=F32)
    o_ref[...] += jnp.dot(vt_ref[...], w_scr[(n + 1) % 2], preferred_element_type=F32)
    w_scr[n % 2] = (_gelu(s) * g_scr[...]).astype(BF16)


def _peer_dense(h2, u_tab, vt_tab, n1, p1, r2, p2, tt, et):
    m, d = h2.shape
    n_et = u_tab.shape[0] // et
    steps = (m // tt) * n_et

    def cur(fn):
        return lambda n: fn(jnp.minimum(n, steps - 1))

    def prev(fn):
        return lambda n: fn(jnp.maximum(n - 1, 0))

    once = pl.Buffered(1)
    sel_a = pl.BlockSpec((N_KEYS, PEER_HEADS, tt), cur(lambda c: (0, 0, c // n_et)),
                         pipeline_mode=once)
    sel_b = pl.BlockSpec((PEER_HEADS, N_KEYS, tt), cur(lambda c: (0, 0, c // n_et)),
                         pipeline_mode=once)
    return pl.pallas_call(
        functools.partial(_peer_dense_kernel, et=et, n_et=n_et),
        grid=(steps + 1,),
        in_specs=[pl.BlockSpec((tt, d), cur(lambda c: (c // n_et, 0))),
                  pl.BlockSpec((et, d), cur(lambda c: (c % n_et, 0))),
                  pl.BlockSpec((d, et), prev(lambda c: (0, c % n_et))),
                  sel_a, sel_a, sel_b, sel_b],
        out_specs=pl.BlockSpec((d, tt), prev(lambda c: (0, c // n_et))),
        out_shape=jax.ShapeDtypeStruct((d, m), F32),
        scratch_shapes=[pltpu.VMEM((2, et, tt), BF16), pltpu.VMEM((et, tt), F32)],
        compiler_params=_cparams("arbitrary"),
        name="peer_dense",
    )(h2, u_tab, vt_tab, n1, p1, r2, p2)


def _final_kernel(x1_ref, ft_ref, gt_ref, g_ref, b_ref, o_ref, *, alpha):
    f = ft_ref[...].T
    o_ref[...] = _layer_norm_rows(alpha * x1_ref[...] + gt_ref[...] * f,
                                  g_ref[...], b_ref[...], LN_EPS)


def _final_ln(x1, ft, first_token, mod, ln_g, ln_b, per_row, seq_len, alpha, tm):
    m, d = x1.shape
    col0 = first_token // tm
    row = pl.BlockSpec((tm, d), lambda i: (i, 0))
    vec = pl.BlockSpec((1, d), lambda i: (0, 0))
    return pl.pallas_call(
        functools.partial(_final_kernel, alpha=alpha),
        grid=(m // tm,),
        in_specs=[row, pl.BlockSpec((d, tm), lambda i: (0, col0 + i)),
                  _mod_spec(per_row, tm, d, 5, seq_len), vec, vec],
        out_specs=row,
        out_shape=jax.ShapeDtypeStruct((m, d), F32),
        compiler_params=_cparams("arbitrary"),
        name="final_ln2",
    )(x1, ft, mod, ln_g, ln_b)


def _pack_w_in(w_in, d, d_b, heads):
    o_small = 3 * d
    o_u = o_small + SMALL
    o_v = o_u + d_b
    o_ga = o_v + d_b
    o_gb = o_ga + d
    small = jnp.pad(w_in[:, o_small:o_u], ((0, 0), (0, SMALL_PAD - SMALL)))
    rkv = _chan_swap(w_in[:, :o_small].reshape(-1, 3, d), heads).reshape(-1, o_small)
    return jnp.concatenate([rkv, w_in[:, o_ga:o_gb], w_in[:, o_gb:o_gb + d],
                            w_in[:, o_u:o_v], w_in[:, o_v:o_ga], small], axis=1).astype(BF16)


def _dense_token_tile(m):
    for tt in (768, 512, 256):
        if (-m) % tt <= m // 20:
            return tt
    return LANES


def _chan_swap(x, heads, axis=-1):
    axis = axis % x.ndim
    shp = x.shape
    x = x.reshape(shp[:axis] + (heads, HEAD_A) + shp[axis + 1:])
    return jnp.swapaxes(x, axis, axis + 1).reshape(shp)


def _chan_unswap(x, heads, axis=-1):
    axis = axis % x.ndim
    shp = x.shape
    x = x.reshape(shp[:axis] + (HEAD_A, heads) + shp[axis + 1:])
    return jnp.swapaxes(x, axis, axis + 1).reshape(shp)


def _split_shift(s, d, heads):
    main = _chan_swap(s[:, :3 * d].reshape(-1, 3, d), heads).reshape(-1, 3 * d)
    return main, jnp.pad(s[:, 3 * d:], ((0, 0), (0, SMALL_PAD - SMALL)))


def _join_shift(proj_rows, d, heads):
    small_off = 6 * d
    main = _chan_unswap(proj_rows[:, :3 * d].reshape(-1, 3, d), heads).reshape(-1, 3 * d)
    return jnp.concatenate([main, proj_rows[:, small_off:small_off + SMALL]], axis=1)


def _rwkv_prompt(planes, bsz, heads, p, tc):
    tile = lambda x: jnp.tile(x.reshape(heads, HEAD_A).T, (1, bsz))
    s0 = jnp.zeros((1, HEAD_A, HEAD_A, LANES), F32)
    y, st = _scan(planes, s0, tile(p["lnx_g"]), tile(p["lnx_b"]), tile(p["r_k"])[:, None, :], tc)
    wkv = st[0].reshape(HEAD_A, HEAD_A, bsz, heads).transpose(2, 3, 1, 0)
    return y, wkv


def _rwkv_sample(pre, nb, heads, wkv0, p):
    to_l = lambda x: x.reshape(nb, HEAD_A, heads).transpose(1, 2, 0).reshape(HEAD_A, 1, heads * nb)
    rep = lambda x: jnp.repeat(x.reshape(heads, HEAD_A).T, nb, axis=1)
    s0 = wkv0.transpose(1, 3, 2, 0)
    y, st = _scan([to_l(x) for x in pre], s0,
                  rep(p["lnx_g"]), rep(p["lnx_b"]), rep(p["r_k"])[:, None, :], 1)
    y = y.reshape(HEAD_A, heads, nb).transpose(2, 0, 1).reshape(nb, heads * HEAD_A)
    return y, st.transpose(3, 0, 2, 1)


def _layer(x_p, x_s, c_p, c_s, wkv_s, shift_s, p):
    bsz, t_len, d = x_p.shape
    nb = x_s.shape[0]
    heads = d // HEAD_A
    d_b = d // 2
    alpha = p["alpha"]
    mp = bsz * t_len

    assert bsz * heads == LANES and nb == LANES, "RWKV-7 recurrence fills 128 lanes per group"
    swap = functools.partial(_chan_swap, heads=heads)
    mu_main, mu_small = _split_shift(p["mu_shift"][None], d, heads)
    lp = {
        "mu_main": mu_main, "mu_small": mu_small,
        "w0": swap(p["w0"])[None], "w2": swap(p["w2"]), "a0": swap(p["a0"])[None],
        "a2": swap(p["a2"]), "g2": swap(p["g2"]),
        "k_k": swap(p["k_k"])[None], "k_a": swap(p["k_a"])[None],
        "sgu_ln_g": p["sgu_ln_g"][None], "sgu_ln_b": p["sgu_ln_b"][None], "sgu_w": p["sgu_w"],
        "sgu_b_full": jnp.broadcast_to(p["sgu_b"][:, :, None], p["sgu_b"].shape + (GROUP_B,)),
        "sgu_w_first": jnp.repeat(p["sgu_w"][:, 0, 0], GROUP_B)[None],
        "sgu_b_first": jnp.repeat(p["sgu_b"][:, 0], GROUP_B)[None],
    }
    ln1 = (p["ln1_g"][None], p["ln1_b"][None])
    ln2 = (p["ln2_g"][None], p["ln2_b"][None])

    c_all = jnp.concatenate([c_p, c_s], axis=0)
    mod = _modulation(jnp.pad(c_all, ((0, (-c_all.shape[0]) % SUBLANES), (0, 0))),
                      p["w_ada"], p["b_ada"])
    mod_p = mod[:bsz].reshape(bsz, 1, 6 * d)
    mod_s = mod[bsz:bsz + nb]

    w_pack = _pack_w_in(p["w_in"], d, d_b, heads)
    n_pack = w_pack.shape[1]
    tn_in = _pick(n_pack, (1280, 512))
    xp2 = x_p.reshape(mp, d)
    xs2 = x_s.reshape(nb, d)
    tm_big = _pick(t_len, (1024, 512, 256, 128))
    tm_mid = _pick(t_len, (512, 256, 128))
    tm_small = _pick(t_len, (256, 128))
    proj_p = _inproj(xp2, mod_p, w_pack, False, t_len, tm_big, tn_in)
    proj_s = _inproj(xs2, mod_s, w_pack, True, 1, nb, tn_in)

    zm = jnp.zeros((bsz, 1, 3 * d), F32)
    zs = jnp.zeros((bsz, 1, SMALL_PAD), F32)
    *planes_p, g_p = _rwkv_pre(proj_p.reshape(bsz, t_len, n_pack), zm, zs, lp, d, heads, False,
                               _pick(t_len, (32, 16, 8)))
    sm, ss = _split_shift(shift_s, d, heads)
    *pre_s, g_s = _rwkv_pre(proj_s[None], sm[None], ss[None], lp, d, heads, True, nb)
    y_planes, wkv_p = _rwkv_prompt(planes_p, bsz, heads, p, _pick(t_len, (64, 32, 16, 8)))
    ya_p = _from_planes(y_planes, bsz, heads, _pick(t_len, (128, 64, 32, 16, 8))).reshape(mp, d)
    g_p = g_p.reshape(mp, d)
    ya_s, wkv_s_new = _rwkv_sample([x[0] for x in pre_s], nb, heads, wkv_s, p)
    g_s = g_s[0]

    u_blk = (5 * d) // d_b
    yb_p = _sgu_prompt(proj_p, lp, d_b, u_blk, tm_mid)
    yb_s, vrows_s = _sgu_single(proj_s, lp, d_b, u_blk)

    wa = _chan_swap(p["w_br_a"], heads, axis=0).astype(BF16)
    wb = p["w_br_b"].astype(BF16)
    wo = p["w_o"].astype(BF16)
    wq = p["peer_wq"].astype(BF16)
    tn_mrg = _pick(d, (512, 256, 128))
    mrg_p = _merge(ya_p, g_p, yb_p, proj_p, wa, wb, d, tm_mid, tn_mrg)
    mrg_s = _merge(ya_s, g_s, yb_s, proj_s, wa, wb, d, nb, tn_mrg)
    x1_p, h2_p, q_p = _out_ln(mrg_p, xp2, mod_p, wo, wq, *ln1, False, t_len, alpha, tm_small)
    x1_s, h2_s, q_s = _out_ln(mrg_s, xs2, mod_s, wo, wq, *ln1, True, 1, alpha, nb)

    m_all = mp + nb
    tt = _dense_token_tile(m_all)
    pad_rows = (-m_all) % tt
    h2 = jnp.concatenate([h2_p, h2_s, jnp.zeros((pad_rows, d), BF16)], axis=0)
    q = jnp.concatenate([q_p, q_s, jnp.zeros((pad_rows, q_p.shape[1]), BF16)], axis=0)
    n1, p1, r2, p2 = _peer_topk(q, p["peer_keys"], LANES)
    n1 = n1.transpose(1, 0, 2)
    p1 = p1.transpose(1, 0, 2)
    ft = _peer_dense(h2, p["peer_u"].astype(BF16), p["peer_v"].T.astype(BF16), n1, p1, r2, p2,
                     tt, 4 * N_KEYS)

    y_p = _final_ln(x1_p, ft, 0, mod_p, *ln2, False, t_len, alpha, tm_small)
    y_s = _final_ln(x1_s, ft, mp, mod_s, *ln2, True, 1, alpha, nb)

    last = proj_p.reshape(bsz, t_len, -1)[:, -1]
    return (y_p.reshape(bsz, t_len, d), y_s.reshape(nb, 1, d), wkv_p, _join_shift(last, d, heads),
            wkv_s_new, _join_shift(proj_s, d, heads), vrows_s.reshape(nb, 1, d_b))


def kernel(x_prompt, x_sample, c_prompt, c_sample, state_wkv, state_shift, w_ada, b_ada, w_in, mu_shift, w0, w2, a0, a2, g2, k_k, k_a, r_k, lnx_g, lnx_b, sgu_ln_g, sgu_ln_b, sgu_w, sgu_b, w_br_a, w_br_b, w_o, ln1_g, ln1_b, peer_wq, peer_keys, peer_u, peer_v, ln2_g, ln2_b):
    names = ("w_ada", "b_ada", "w_in", "mu_shift", "w0", "w2", "a0", "a2", "g2", "k_k", "k_a",
             "r_k", "lnx_g", "lnx_b", "sgu_ln_g", "sgu_ln_b", "sgu_w", "sgu_b", "w_br_a",
             "w_br_b", "w_o", "ln1_g", "ln1_b", "peer_wq", "peer_keys", "peer_u", "peer_v",
             "ln2_g", "ln2_b")
    stacked = (w_ada, b_ada, w_in, mu_shift, w0, w2, a0, a2, g2, k_k, k_a, r_k, lnx_g, lnx_b,
               sgu_ln_g, sgu_ln_b, sgu_w, sgu_b, w_br_a, w_br_b, w_o, ln1_g, ln1_b, peer_wq,
               peer_keys, peer_u, peer_v, ln2_g, ln2_b)
    depth = w_ada.shape[0]
    alpha = (2 * depth) ** 0.25
    y_p, y_s = x_prompt, x_sample
    outs = [[] for _ in range(5)]
    for l in range(depth):
        p = {n: a[l] for n, a in zip(names, stacked)}
        p["alpha"] = alpha
        y_p, y_s, *state = _layer(y_p, y_s, c_prompt, c_sample, state_wkv[l], state_shift[l], p)
        for acc, s in zip(outs, state):
            acc.append(s)
    return (y_p, y_s) + tuple(jnp.stack(o) for o in outs)
```

```python
import functools

import jax
import jax.numpy as jnp
from jax import lax
from jax.experimental import pallas as pl
from jax.experimental.pallas import tpu as pltpu

F32 = jnp.float32
BF16 = jnp.bfloat16

HEAD_A = 64
R_DECAY = 96
R_AAA = 96
R_GATE = 256
SMALL = R_DECAY + R_AAA + R_GATE
SMALL_PAD = 512
CHUNK = 128
GROUP_B = 128
PEER_HEADS = 8
N_KEYS = 128
DK_HALF = 128
TOPK = 16
LN_EPS = 1e-5
GN_EPS = 64e-5

LANES = 128
SUBLANES = 8
VMEM_LIMIT = 56 * 1024 * 1024


def _cparams(*sem):
    return pltpu.CompilerParams(dimension_semantics=sem, vmem_limit_bytes=VMEM_LIMIT)


def _gelu(x):
    return jax.nn.gelu(x, approximate=True)


def _layer_norm_rows(x, g, b, eps):
    mu = jnp.mean(x, axis=-1, keepdims=True)
    xc = x - mu
    var = jnp.mean(xc * xc, axis=-1, keepdims=True)
    return xc * lax.rsqrt(var + eps) * g + b


def _pick(n, prefs):
    for p in prefs:
        if n % p == 0:
            return p
    return n


def _mod_spec(per_row, tm, d, chunk, rows_per_seq):
    if per_row:
        return pl.BlockSpec((tm, d), lambda i, *_: (i, chunk))
    return pl.BlockSpec((None, 1, d), lambda i, *_: ((i * tm) // rows_per_seq, 0, chunk))


def _mod_kernel(c_ref, w_ref, b_ref, o_ref):
    c = c_ref[...]
    s = (c * jax.nn.sigmoid(c)).astype(BF16)
    o_ref[...] = jnp.dot(s, w_ref[...].astype(BF16), preferred_element_type=F32) + b_ref[...]


def _modulation(c_all, w_ada, b_ada):
    m, d = c_all.shape
    n = w_ada.shape[1]
    tn = _pick(n, (1536, 512))
    return pl.pallas_call(
        _mod_kernel,
        grid=(n // tn,),
        in_specs=[pl.BlockSpec((m, d), lambda j: (0, 0)),
                  pl.BlockSpec((d, tn), lambda j: (0, j)),
                  pl.BlockSpec((1, tn), lambda j: (0, j))],
        out_specs=pl.BlockSpec((m, tn), lambda j: (0, j)),
        out_shape=jax.ShapeDtypeStruct((m, n), F32),
        compiler_params=_cparams("arbitrary"),
        name="adaln_mod",
    )(c_all, w_ada, b_ada.reshape(1, n))


def _inproj_kernel(x_ref, sh_ref, sc_ref, w_ref, o_ref, h_scr):
    @pl.when(pl.program_id(1) == 0)
    def _():
        h_scr[...] = (x_ref[...] * (1.0 + sc_ref[...]) + sh_ref[...]).astype(BF16)

    o_ref[...] = jnp.dot(h_scr[...], w_ref[...], preferred_element_type=F32)


def _inproj(x, mod, w_pack, per_row, seq_len, tm, tn):
    m, d = x.shape
    n = w_pack.shape[1]
    return pl.pallas_call(
        _inproj_kernel,
        grid=(m // tm, n // tn),
        in_specs=[pl.BlockSpec((tm, d), lambda i, j: (i, 0)),
                  _mod_spec(per_row, tm, d, 0, seq_len),
                  _mod_spec(per_row, tm, d, 1, seq_len),
                  pl.BlockSpec((d, tn), lambda i, j: (0, j))],
        out_specs=pl.BlockSpec((tm, tn), lambda i, j: (i, j)),
        out_shape=jax.ShapeDtypeStruct((m, n), F32),
        scratch_shapes=[pltpu.VMEM((tm, d), BF16)],
        compiler_params=_cparams("arbitrary", "arbitrary"),
        name="in_proj",
    )(x, mod, mod, w_pack)


def _swap_lane_groups(xs, width):
    n = len(xs)
    grp = lax.broadcasted_iota(jnp.int32, xs[0].shape, 1) // width
    moved = []
    for s in range(n):
        acc = xs[0]
        for b in range(1, n):
            acc = jnp.where(grp == (b - s) % n, xs[b], acc)
        moved.append(acc if s == 0 else pltpu.roll(acc, s * width, 1))
    outs = []
    for q in range(n):
        acc = moved[0]
        for s in range(1, n):
            acc = jnp.where(grp == (q + s) % n, moved[s], acc)
        outs.append(acc)
    return outs


def _to_planes(xs, o_ref, heads):
    per = LANES // heads
    for j in range(xs[0].shape[1] // LANES):
        outs = _swap_lane_groups([x[:, j * LANES:(j + 1) * LANES] for x in xs], heads)
        for q in range(per):
            o_ref[j * per + q] = outs[q]


def _from_planes_kernel(y_ref, o_ref, *, heads):
    per = LANES // heads
    for j in range(y_ref.shape[0] // per):
        outs = _swap_lane_groups([y_ref[j * per + q] for q in range(per)], heads)
        for b in range(per):
            o_ref[b, :, j * LANES:(j + 1) * LANES] = outs[b]


def _from_planes(y, nseq, heads, tm):
    n, t_len, _ = y.shape
    d = n * heads
    return pl.pallas_call(
        functools.partial(_from_planes_kernel, heads=heads),
        grid=(t_len // tm,),
        in_specs=[pl.BlockSpec((n, tm, LANES), lambda i: (0, i, 0))],
        out_specs=pl.BlockSpec((nseq, tm, d), lambda i: (0, i, 0)),
        out_shape=jax.ShapeDtypeStruct((nseq, t_len, d), F32),
        compiler_params=_cparams("arbitrary"),
        name="rwkv_from_planes",
    )(y)


def _rwkv_pre_kernel(pm_ref, ps_ref, qm_ref, qs_ref, stm_ref, sts_ref, mum_ref, mus_ref,
                     w0_ref, w2_ref, a0_ref, a2_ref, g2_ref, kk_ref, ka_ref,
                     r_o, w_o, k_o, v_o, kk_o, a_o, g_o, *, d, heads, per_row_state):
    nseq, tm, _ = pm_ref.shape

    def shifted(p, q, st):
        if per_row_state:
            return st
        first = pl.program_id(0) == 0
        prev_row = jnp.where(first, st, q[SUBLANES - 1:SUBLANES, :])
        row = lax.broadcasted_iota(jnp.int32, p.shape, 0)
        return jnp.where(row == 0, prev_row, pltpu.roll(p, 1, 0))

    def mix(p, q, st, mu):
        return p + (shifted(p, q, st) - p) * mu

    seqs = range(nseq)
    xs_s = jnp.concatenate([mix(ps_ref[b], qs_ref[b], sts_ref[b], mus_ref[...]) for b in seqs],
                           axis=0)
    wl = xs_s[:, 0:R_DECAY]
    al = xs_s[:, R_DECAY:R_DECAY + R_AAA]
    gl = xs_s[:, R_DECAY + R_AAA:SMALL]
    wlin = w0_ref[...] + jnp.dot(jnp.tanh(wl).astype(BF16), w2_ref[...].astype(BF16),
                                 preferred_element_type=F32)
    w = -jax.nn.softplus(-wlin) - 0.5
    decay = jnp.exp(-jnp.exp(w))
    a = jax.nn.sigmoid(a0_ref[...] + jnp.dot(al.astype(BF16), a2_ref[...].astype(BF16),
                                             preferred_element_type=F32))
    g = jnp.dot(jax.nn.sigmoid(gl).astype(BF16), g2_ref[...].astype(BF16),
                preferred_element_type=F32)

    def of_seq(x, b):
        return x[b * tm:(b + 1) * tm]

    def part(n, b):
        cols = slice(n * d, (n + 1) * d)
        return mix(pm_ref[b, :, cols], qm_ref[b, :, cols], stm_ref[b, :, cols], mum_ref[:, cols])

    def emit(o_ref, xs):
        if per_row_state:
            for b in seqs:
                o_ref[b] = xs[b]
        else:
            _to_planes(xs, o_ref, heads)

    for b in seqs:
        g_o[b] = of_seq(g, b)
    emit(r_o, [part(0, b) for b in seqs])
    emit(w_o, [of_seq(decay, b) for b in seqs])
    ks = [part(1, b) for b in seqs]
    emit(kk_o, [ks[b] * kk_ref[...] for b in seqs])
    emit(k_o, [ks[b] * (1.0 + (of_seq(a, b) - 1.0) * ka_ref[...]) for b in seqs])
    emit(v_o, [part(2, b) for b in seqs])
    emit(a_o, [of_seq(a, b) for b in seqs])


def _rwkv_pre(proj, state_main, state_small, lp, d, heads, per_row_state, tm):
    nseq, t_len, _ = proj.shape
    small_blk = (6 * d) // SMALL_PAD

    def prev_map(col):
        def f(i):
            return (0, jnp.maximum(i * (tm // SUBLANES) - 1, 0), col)
        return f

    if per_row_state:
        stm_spec = pl.BlockSpec((nseq, tm, 3 * d), lambda i: (0, i, 0))
        sts_spec = pl.BlockSpec((nseq, tm, SMALL_PAD), lambda i: (0, i, 0))
    else:
        stm_spec = pl.BlockSpec((nseq, 1, 3 * d), lambda i: (0, 0, 0))
        sts_spec = pl.BlockSpec((nseq, 1, SMALL_PAD), lambda i: (0, 0, 0))

    vec = lambda w: pl.BlockSpec((1, w), lambda i: (0, 0))
    full = lambda a: pl.BlockSpec(a.shape, lambda i: (0,) * a.ndim)
    rows = pl.BlockSpec((nseq, tm, d), lambda i: (0, i, 0))
    rows_shape = jax.ShapeDtypeStruct((nseq, t_len, d), F32)
    if per_row_state:
        vec_out, vec_shape = rows, rows_shape
    else:
        vec_out = pl.BlockSpec((HEAD_A, tm, LANES), lambda i: (0, i, 0))
        vec_shape = jax.ShapeDtypeStruct((HEAD_A, t_len, LANES), F32)
    kern = functools.partial(_rwkv_pre_kernel, d=d, heads=heads, per_row_state=per_row_state)
    in_specs = [pl.BlockSpec((nseq, tm, 3 * d), lambda i: (0, i, 0)),
                pl.BlockSpec((nseq, tm, SMALL_PAD), lambda i: (0, i, small_blk)),
                pl.BlockSpec((nseq, SUBLANES, 3 * d), prev_map(0)),
                pl.BlockSpec((nseq, SUBLANES, SMALL_PAD), prev_map(small_blk)),
                stm_spec, sts_spec, vec(3 * d), vec(SMALL_PAD), vec(d), full(lp["w2"]), vec(d),
                full(lp["a2"]), full(lp["g2"]), vec(d), vec(d)]
    return pl.pallas_call(
        kern,
        grid=(t_len // tm,),
        in_specs=in_specs,
        out_specs=[vec_out] * 6 + [rows],
        out_shape=[vec_shape] * 6 + [rows_shape],
        compiler_params=_cparams("arbitrary"),
        name="rwkv_pre",
    )(proj, proj, proj, proj, state_main, state_small, lp["mu_main"], lp["mu_small"], lp["w0"],
      lp["w2"], lp["a0"], lp["a2"], lp["g2"], lp["k_k"], lp["k_a"])


def _scan_kernel(r_ref, w_ref, k_ref, v_ref, kk_ref, a_ref, s0_ref, lng_ref, lnb_ref, rk_ref,
                 y_ref, st_ref, s_scr, aa_scr, bb_scr, bon_scr, yt_scr, *, tc):
    n = HEAD_A
    c = pl.program_id(1)

    @pl.when(c == 0)
    def _():
        s_scr[...] = s0_ref[...]

    kk = kk_ref[...]
    nrm = jnp.sqrt(jnp.sum(kk * kk, axis=0, keepdims=True))
    kkn = kk / jnp.maximum(nrm, 1e-12)
    aa_scr[...] = -kkn
    bb_scr[...] = kkn * a_ref[...]
    bon_scr[...] = jnp.sum(r_ref[...] * k_ref[...] * rk_ref[...], axis=0)

    zeros = jnp.zeros((n, LANES), F32)
    nblk = n // SUBLANES
    sub = lax.broadcasted_iota(jnp.int32, (SUBLANES, LANES), 0)

    def row(ref, i, t):
        return ref[i, pl.ds(t, 1), :]

    def tile_of_step(ref, t):
        blocks = []
        for m in range(nblk):
            acc = jnp.broadcast_to(row(ref, m * SUBLANES, t), (SUBLANES, LANES))
            for s in range(1, SUBLANES):
                acc = jnp.where(sub == s, row(ref, m * SUBLANES + s, t), acc)
            blocks.append(acc)
        return jnp.concatenate(blocks, axis=0)

    def sa_first(kb, acc):
        for j in range(SUBLANES):
            i = kb * SUBLANES + j
            acc = acc + s_scr[i] * row(aa_scr, i, 0)
        return acc

    sa0 = lax.fori_loop(0, nblk, sa_first, zeros)

    lng = lng_ref[...]
    lnb = lnb_ref[...]

    def step(t, sa):
        tn = jnp.minimum(t + 1, tc - 1)
        vt = tile_of_step(v_ref, t)

        def kblock(kb, carry):
            y, san = carry
            for j in range(SUBLANES):
                i = kb * SUBLANES + j
                sn = s_scr[i] * row(w_ref, i, t) + sa * row(bb_scr, i, t) + vt * row(k_ref, i, t)
                s_scr[i] = sn
                y = y + sn * row(r_ref, i, t)
                san = san + sn * row(aa_scr, i, tn)
            return y, san

        y, san = lax.fori_loop(0, nblk, kblock, (zeros, zeros))
        mu = jnp.mean(y, axis=0, keepdims=True)
        yc = y - mu
        var = jnp.mean(yc * yc, axis=0, keepdims=True)
        yn = yc * lax.rsqrt(var + GN_EPS) * lng + lnb
        return san, yn + bon_scr[pl.ds(t, 1), :] * vt

    grp = min(SUBLANES, tc)

    def group(gb, sa):
        t0 = pl.multiple_of(gb * grp, grp)

        def one(s, sa):
            sa, out = step(t0 + s, sa)
            yt_scr[s] = out
            return sa

        sa = lax.fori_loop(0, grp, one, sa)
        for i in range(n):
            acc = jnp.broadcast_to(yt_scr[0, i:i + 1, :], (grp, LANES))
            for s in range(1, grp):
                acc = jnp.where(sub[:grp] == s, yt_scr[s, i:i + 1, :], acc)
            y_ref[i, pl.ds(t0, grp), :] = acc
        return sa

    lax.fori_loop(0, tc // grp, group, sa0)

    @pl.when(c == pl.num_programs(1) - 1)
    def _():
        st_ref[...] = s_scr[...]


def _scan(seq, s0, lng, lnb, rk, tc):
    n, t_len, gl = seq[0].shape
    g = gl // LANES
    step_spec = pl.BlockSpec((n, tc, LANES), lambda gi, c: (0, c, gi))
    par_spec = pl.BlockSpec((n, LANES), lambda gi, c: (0, gi))
    rk_spec = pl.BlockSpec((n, 1, LANES), lambda gi, c: (0, 0, gi))
    st_spec = pl.BlockSpec((None, n, n, LANES), lambda gi, c: (gi, 0, 0, 0))
    return pl.pallas_call(
        functools.partial(_scan_kernel, tc=tc),
        grid=(g, t_len // tc),
        in_specs=[step_spec] * 6 + [st_spec, par_spec, par_spec, rk_spec],
        out_specs=[step_spec, st_spec],
        out_shape=[jax.ShapeDtypeStruct((n, t_len, gl), F32),
                   jax.ShapeDtypeStruct((g, n, n, LANES), F32)],
        scratch_shapes=[pltpu.VMEM((n, n, LANES), F32),
                        pltpu.VMEM((n, tc, LANES), F32),
                        pltpu.VMEM((n, tc, LANES), F32),
                        pltpu.VMEM((tc, LANES), F32),
                        pltpu.VMEM((min(SUBLANES, tc), n, LANES), F32)],
        compiler_params=_cparams("arbitrary", "arbitrary"),
        name="rwkv_scan",
    )(*seq, s0, lng, lnb, rk)


def _sgu_kernel(pu_ref, pv_ref, lg_ref, lb_ref, w_ref, b_ref, y_ref, *, chunks):
    ng = w_ref.shape[0]
    row = lax.broadcasted_iota(jnp.int32, (CHUNK, CHUNK), 0)
    col = lax.broadcasted_iota(jnp.int32, (CHUNK, CHUNK), 1)
    causal = col <= row
    for ci in range(chunks):
        rows = slice(ci * CHUNK, (ci + 1) * CHUNK)
        u = _gelu(pu_ref[rows, :])
        v = _layer_norm_rows(_gelu(pv_ref[rows, :]), lg_ref[...], lb_ref[...], LN_EPS)
        for g in range(ng):
            cols = slice(g * GROUP_B, (g + 1) * GROUP_B)
            ws = jnp.where(causal, w_ref[g], 0.0).astype(BF16)
            mixed = jnp.dot(ws, v[:, cols].astype(BF16), preferred_element_type=F32) + b_ref[g]
            y_ref[rows, cols] = (u[:, cols] * mixed).astype(y_ref.dtype)


def _sgu_prompt(proj, lp, d_b, u_blk, tm):
    m = proj.shape[0]
    ng = d_b // GROUP_B
    return pl.pallas_call(
        functools.partial(_sgu_kernel, chunks=tm // CHUNK),
        grid=(m // tm,),
        in_specs=[pl.BlockSpec((tm, d_b), lambda i: (i, u_blk)),
                  pl.BlockSpec((tm, d_b), lambda i: (i, u_blk + 1)),
                  pl.BlockSpec((1, d_b), lambda i: (0, 0)),
                  pl.BlockSpec((1, d_b), lambda i: (0, 0)),
                  pl.BlockSpec((ng, CHUNK, CHUNK), lambda i: (0, 0, 0)),
                  pl.BlockSpec((ng, CHUNK, GROUP_B), lambda i: (0, 0, 0))],
        out_specs=pl.BlockSpec((tm, d_b), lambda i: (i, 0)),
        out_shape=jax.ShapeDtypeStruct((m, d_b), BF16),
        compiler_params=_cparams("arbitrary"),
        name="sgu_chunked",
    )(proj, proj, lp["sgu_ln_g"], lp["sgu_ln_b"], lp["sgu_w"], lp["sgu_b_full"])


def _sgu_first_kernel(pu_ref, pv_ref, lg_ref, lb_ref, wd_ref, bd_ref, y_ref, v_ref):
    u = _gelu(pu_ref[...])
    v = _layer_norm_rows(_gelu(pv_ref[...]), lg_ref[...], lb_ref[...], LN_EPS)
    v_ref[...] = v
    vb = v.astype(BF16).astype(F32)
    wd = wd_ref[...].astype(BF16).astype(F32)
    y_ref[...] = (u * (vb * wd + bd_ref[...])).astype(y_ref.dtype)


def _sgu_single(proj, lp, d_b, u_blk):
    m = proj.shape[0]
    vec = pl.BlockSpec((1, d_b), lambda i: (0, 0))
    return pl.pallas_call(
        _sgu_first_kernel,
        grid=(1,),
        in_specs=[pl.BlockSpec((m, d_b), lambda i: (0, u_blk)),
                  pl.BlockSpec((m, d_b), lambda i: (0, u_blk + 1)), vec, vec, vec, vec],
        out_specs=[pl.BlockSpec((m, d_b), lambda i: (0, 0))] * 2,
        out_shape=[jax.ShapeDtypeStruct((m, d_b), BF16), jax.ShapeDtypeStruct((m, d_b), F32)],
        compiler_params=_cparams("arbitrary"),
        name="sgu_single",
    )(proj, proj, lp["sgu_ln_g"], lp["sgu_ln_b"], lp["sgu_w_first"], lp["sgu_b_first"])


def _merge_kernel(ya_ref, g_ref, yb_ref, ga_ref, gb_ref, wa_ref, wb_ref, o_ref, a_scr):
    @pl.when(pl.program_id(1) == 0)
    def _():
        a_scr[...] = (ya_ref[...] * g_ref[...]).astype(BF16)

    ta = jnp.dot(a_scr[...], wa_ref[...], preferred_element_type=F32)
    tb = jnp.dot(yb_ref[...], wb_ref[...], preferred_element_type=F32)
    o_ref[...] = (jax.nn.sigmoid(ga_ref[...]) * ta
                  + jax.nn.sigmoid(gb_ref[...]) * tb).astype(o_ref.dtype)


def _merge(ya, g, yb, proj, wa, wb, d, tm, tn):
    m = ya.shape[0]
    d_b = yb.shape[1]
    ga_blk = (3 * d) // tn
    gb_blk = (4 * d) // tn
    return pl.pallas_call(
        _merge_kernel,
        grid=(m // tm, d // tn),
        in_specs=[pl.BlockSpec((tm, d), lambda i, j: (i, 0)),
                  pl.BlockSpec((tm, d), lambda i, j: (i, 0)),
                  pl.BlockSpec((tm, d_b), lambda i, j: (i, 0)),
                  pl.BlockSpec((tm, tn), lambda i, j: (i, ga_blk + j)),
                  pl.BlockSpec((tm, tn), lambda i, j: (i, gb_blk + j)),
                  pl.BlockSpec((d, tn), lambda i, j: (0, j)),
                  pl.BlockSpec((d_b, tn), lambda i, j: (0, j))],
        out_specs=pl.BlockSpec((tm, tn), lambda i, j: (i, j)),
        out_shape=jax.ShapeDtypeStruct((m, d), BF16),
        scratch_shapes=[pltpu.VMEM((tm, d), BF16)],
        compiler_params=_cparams("arbitrary", "arbitrary"),
        name="branch_merge",
    )(ya, g, yb, proj, proj, wa, wb)


def _out_ln_kernel(m_ref, x_ref, gt_ref, sh_ref, sc_ref, wo_ref, wq_ref, g_ref, b_ref,
                   x1_ref, h2_ref, q_ref, *, alpha):
    z = jnp.dot(m_ref[...], wo_ref[...], preferred_element_type=F32)
    x1 = _layer_norm_rows(alpha * x_ref[...] + gt_ref[...] * z, g_ref[...], b_ref[...], LN_EPS)
    x1_ref[...] = x1
    h2 = (x1 * (1.0 + sc_ref[...]) + sh_ref[...]).astype(BF16)
    h2_ref[...] = h2
    q_ref[...] = jnp.dot(h2, wq_ref[...], preferred_element_type=F32).astype(q_ref.dtype)


def _out_ln(mrg, x, mod, wo, wq, ln_g, ln_b, per_row, seq_len, alpha, tm):
    m, d = x.shape
    nq = wq.shape[1]
    vec = pl.BlockSpec((1, d), lambda i: (0, 0))
    row = pl.BlockSpec((tm, d), lambda i: (i, 0))
    return pl.pallas_call(
        functools.partial(_out_ln_kernel, alpha=alpha),
        grid=(m // tm,),
        in_specs=[row, row,
                  _mod_spec(per_row, tm, d, 2, seq_len),
                  _mod_spec(per_row, tm, d, 3, seq_len),
                  _mod_spec(per_row, tm, d, 4, seq_len),
                  pl.BlockSpec((d, d), lambda i: (0, 0)),
                  pl.BlockSpec((d, nq), lambda i: (0, 0)), vec, vec],
        out_specs=[row, row, pl.BlockSpec((tm, nq), lambda i: (i, 0))],
        out_shape=[jax.ShapeDtypeStruct((m, d), F32), jax.ShapeDtypeStruct((m, d), BF16),
                   jax.ShapeDtypeStruct((m, nq), BF16)],
        compiler_params=_cparams("arbitrary"),
        name="out_proj_ln1",
    )(mrg, x, mod, mod, mod, wo, wq, ln_g, ln_b)


_PAIRS = [(i, j) for i in range(TOPK) for j in range(TOPK) if (i + 1) * (j + 1) <= TOPK]
_NPAIR_PAD = -(-len(_PAIRS) // SUBLANES) * SUBLANES


def _take_top(work, rounds, break_ties):
    rows = work.shape[0]
    iota = lax.broadcasted_iota(jnp.int32, work.shape, 0).astype(F32)
    rank = jnp.full(work.shape, float(rounds), F32)
    vals = []
    for r in range(rounds):
        mx = jnp.max(work, axis=0, keepdims=True)
        hit = work == mx
        if break_ties:
            first = jnp.min(jnp.where(hit, iota, float(rows)), axis=0, keepdims=True)
            hit = iota == first
        rank = jnp.where(hit, float(r), rank)
        work = jnp.where(hit, -jnp.inf, work)
        vals.append(mx)
    return vals, rank


def _exactly_k(rank, k):
    n = jnp.sum(jnp.where(rank < float(k), 1.0, 0.0), axis=0, keepdims=True)
    return jnp.where(n == float(k), 1.0, 0.0)


def _head_select(s1, s2, cnt_mat, cand_scr, break_ties):
    a1, rank1 = _take_top(s1, TOPK, break_ties)
    a2, rank2 = _take_top(s2, TOPK, break_ties)
    for pos, (i, j) in enumerate(_PAIRS):
        cand_scr[pos:pos + 1, :] = a1[i] + a2[j]
    cand = cand_scr[...]
    _, rank_c = _take_top(cand, TOPK, break_ties)
    sel = jnp.where(rank_c < float(TOPK), 1.0, 0.0)
    top = a1[0] + a2[0]
    z = jnp.sum(sel * jnp.exp(cand - top), axis=0, keepdims=True)
    cnt = jnp.dot(cnt_mat, sel.astype(BF16), preferred_element_type=F32)
    n1 = jnp.zeros(s1.shape, F32)
    for i in range(TOPK):
        n1 = jnp.where(rank1 == float(i), cnt[i:i + 1, :], n1)
    p1 = jnp.where(rank1 < float(TOPK), jnp.exp(s1 - a1[0]), 0.0)
    p2 = jnp.where(rank2 < float(TOPK), jnp.exp(s2 - a2[0]), 0.0) / z
    clean = _exactly_k(rank1, TOPK) * _exactly_k(rank2, TOPK) * _exactly_k(rank_c, TOPK)
    return n1, p1, rank2, p2, clean


def _peer_topk_kernel(q_ref, keys_ref, cnt_ref, n1_ref, p1_ref, r2_ref, p2_ref, cand_a, cand_b):
    nt = (((1,), (1,)), ((), ()))
    cand_a[...] = jnp.full(cand_a.shape, -jnp.inf, F32)
    cand_b[...] = jnp.full(cand_b.shape, -jnp.inf, F32)

    def scores(h):
        base = pl.multiple_of(h * 2 * DK_HALF, 2 * DK_HALF)
        q1 = q_ref[:, pl.ds(base, DK_HALF)]
        q2 = q_ref[:, pl.ds(base + DK_HALF, DK_HALF)]
        s1 = lax.dot_general(keys_ref[h, 0].astype(BF16), q1, nt, preferred_element_type=F32)
        s2 = lax.dot_general(keys_ref[h, 1].astype(BF16), q2, nt, preferred_element_type=F32)
        return s1, s2

    def write(h, res):
        n1_ref[h], p1_ref[h] = res[:2]
        r2_ref[h] = res[2].astype(r2_ref.dtype)
        p2_ref[h] = res[3].astype(p2_ref.dtype)

    def head_pair(hp, carry):
        heads = (2 * hp, 2 * hp + 1)
        scr = (cand_a, cand_b)
        sc = [scores(h) for h in heads]
        res = [_head_select(*sc[n], cnt_ref[...], scr[n], False) for n in range(2)]
        for n in range(2):
            write(heads[n], res[n])
        for n in range(2):
            @pl.when(jnp.min(res[n][4]) < 0.5)
            def _():
                write(heads[n], _head_select(*sc[n], cnt_ref[...], scr[n], True))
        return carry

    lax.fori_loop(0, PEER_HEADS // 2, head_pair, 0)


def _pair_count_matrix():
    return jnp.array([[1.0 if i == r else 0.0 for (i, _) in _PAIRS]
                      + [0.0] * (_NPAIR_PAD - len(_PAIRS)) for r in range(TOPK)], BF16)


def _peer_topk(q, keys, tt):
    m = q.shape[0]
    cnt = _pair_count_matrix()
    out = pl.BlockSpec((PEER_HEADS, N_KEYS, tt), lambda i: (0, 0, i))
    return pl.pallas_call(
        _peer_topk_kernel,
        grid=(m // tt,),
        in_specs=[pl.BlockSpec((tt, q.shape[1]), lambda i: (i, 0)),
                  pl.BlockSpec(keys.shape, lambda i: (0, 0, 0, 0)),
                  pl.BlockSpec(cnt.shape, lambda i: (0, 0))],
        out_specs=[out] * 4,
        out_shape=[jax.ShapeDtypeStruct((PEER_HEADS, N_KEYS, m), dt)
                   for dt in (F32, F32, BF16, BF16)],
        scratch_shapes=[pltpu.VMEM((_NPAIR_PAD, tt), F32)] * 2,
        compiler_params=_cparams("arbitrary"),
        name="peer_topk",
    )(q, keys, cnt)


def _peer_dense_kernel(h_ref, u_ref, vt_ref, n1_ref, p1_ref, r2_ref, p2_ref, o_ref, w_scr, g_scr,
                       *, et, n_et):
    n = pl.program_id(0)
    cur = jnp.minimum(n, pl.num_programs(0) - 2)
    e_cur = cur % n_et
    e_prev = jnp.maximum(n - 1, 0) % n_et
    nt = (((1,), (1,)), ((), ()))

    @pl.when(n == 0)
    def _():
        w_scr[1] = jnp.zeros(w_scr.shape[1:], BF16)

    @pl.when(e_prev == 0)
    def _():
        o_ref[...] = jnp.zeros(o_ref.shape, F32)

    zero = jnp.zeros((), BF16)
    for j in range(et // N_KEYS):
        e1 = e_cur * (et // N_KEYS) + j
        gate = jnp.zeros((N_KEYS, g_scr.shape[1]), BF16)
        for h in range(PEER_HEADS):
            n1 = n1_ref[h, pl.ds(e1, 1), :].astype(BF16)
            p1 = p1_ref[h, pl.ds(e1, 1), :].astype(BF16)
            gate = gate + jnp.where(r2_ref[h] < n1, p2_ref[h], zero) * p1
        g_scr[j * N_KEYS:(j + 1) * N_KEYS, :] = gate
    s = lax.dot_general(u_ref[...], h_ref[...], nt, preferred_element_type=F32)
    o_ref[...] += jnp.dot(vt_ref[...], w_scr[(n + 1) % 2], preferred_element_type=F32)
    w_scr[n % 2] = _gelu(s).astype(BF16) * g_scr[...]


def _peer_dense(h2, u_tab, vt_tab, n1, p1, r2, p2, tt, et):
    m, d = h2.shape
    n_et = u_tab.shape[0] // et
    steps = (m // tt) * n_et

    def cur(fn):
        return lambda n: fn(jnp.minimum(n, steps - 1))

    def prev(fn):
        return lambda n: fn(jnp.maximum(n - 1, 0))

    once = pl.Buffered(1)
    sel = pl.BlockSpec((PEER_HEADS, N_KEYS, tt), cur(lambda c: (0, 0, c // n_et)),
                       pipeline_mode=once)
    return pl.pallas_call(
        functools.partial(_peer_dense_kernel, et=et, n_et=n_et),
        grid=(steps + 1,),
        in_specs=[pl.BlockSpec((tt, d), cur(lambda c: (c // n_et, 0))),
                  pl.BlockSpec((et, d), cur(lambda c: (c % n_et, 0))),
                  pl.BlockSpec((d, et), prev(lambda c: (0, c % n_et))),
                  sel, sel, sel, sel],
        out_specs=pl.BlockSpec((d, tt), prev(lambda c: (0, c // n_et))),
        out_shape=jax.ShapeDtypeStruct((d, m), F32),
        scratch_shapes=[pltpu.VMEM((2, et, tt), BF16), pltpu.VMEM((et, tt), BF16)],
        compiler_params=_cparams("arbitrary"),
        name="peer_dense",
    )(h2, u_tab, vt_tab, n1, p1, r2, p2)


def _final_kernel(x1_ref, ft_ref, gt_ref, g_ref, b_ref, o_ref, *, alpha):
    f = ft_ref[...].T
    o_ref[...] = _layer_norm_rows(alpha * x1_ref[...] + gt_ref[...] * f,
                                  g_ref[...], b_ref[...], LN_EPS)


def _final_ln(x1, ft, first_token, mod, ln_g, ln_b, per_row, seq_len, alpha, tm):
    m, d = x1.shape
    col0 = first_token // tm
    row = pl.BlockSpec((tm, d), lambda i: (i, 0))
    vec = pl.BlockSpec((1, d), lambda i: (0, 0))
    return pl.pallas_call(
        functools.partial(_final_kernel, alpha=alpha),
        grid=(m // tm,),
        in_specs=[row, pl.BlockSpec((d, tm), lambda i: (0, col0 + i)),
                  _mod_spec(per_row, tm, d, 5, seq_len), vec, vec],
        out_specs=row,
        out_shape=jax.ShapeDtypeStruct((m, d), F32),
        compiler_params=_cparams("arbitrary"),
        name="final_ln2",
    )(x1, ft, mod, ln_g, ln_b)


def _pack_w_in(w_in, d, d_b, heads):
    o_small = 3 * d
    o_u = o_small + SMALL
    o_v = o_u + d_b
    o_ga = o_v + d_b
    o_gb = o_ga + d
    small = jnp.pad(w_in[:, o_small:o_u], ((0, 0), (0, SMALL_PAD - SMALL)))
    rkv = _chan_swap(w_in[:, :o_small].reshape(-1, 3, d), heads).reshape(-1, o_small)
    return jnp.concatenate([rkv, w_in[:, o_ga:o_gb], w_in[:, o_gb:o_gb + d],
                            w_in[:, o_u:o_v], w_in[:, o_v:o_ga], small], axis=1).astype(BF16)


def _dense_token_tile(m):
    for tt in (768, 512, 256):
        if (-m) % tt <= m // 20:
            return tt
    return LANES


def _chan_swap(x, heads, axis=-1):
    axis = axis % x.ndim
    shp = x.shape
    x = x.reshape(shp[:axis] + (heads, HEAD_A) + shp[axis + 1:])
    return jnp.swapaxes(x, axis, axis + 1).reshape(shp)


def _chan_unswap(x, heads, axis=-1):
    axis = axis % x.ndim
    shp = x.shape
    x = x.reshape(shp[:axis] + (HEAD_A, heads) + shp[axis + 1:])
    return jnp.swapaxes(x, axis, axis + 1).reshape(shp)


def _split_shift(s, d, heads):
    main = _chan_swap(s[:, :3 * d].reshape(-1, 3, d), heads).reshape(-1, 3 * d)
    return main, jnp.pad(s[:, 3 * d:], ((0, 0), (0, SMALL_PAD - SMALL)))


def _join_shift(proj_rows, d, heads):
    small_off = 6 * d
    main = _chan_unswap(proj_rows[:, :3 * d].reshape(-1, 3, d), heads).reshape(-1, 3 * d)
    return jnp.concatenate([main, proj_rows[:, small_off:small_off + SMALL]], axis=1)


def _rwkv_prompt(planes, bsz, heads, p, tc):
    tile = lambda x: jnp.tile(x.reshape(heads, HEAD_A).T, (1, bsz))
    s0 = jnp.zeros((1, HEAD_A, HEAD_A, LANES), F32)
    y, st = _scan(planes, s0, tile(p["lnx_g"]), tile(p["lnx_b"]), tile(p["r_k"])[:, None, :], tc)
    wkv = st[0].reshape(HEAD_A, HEAD_A, bsz, heads).transpose(2, 3, 1, 0)
    return y, wkv


def _rwkv_sample(pre, nb, heads, wkv0, p):
    to_l = lambda x: x.reshape(nb, HEAD_A, heads).transpose(1, 2, 0).reshape(HEAD_A, 1, heads * nb)
    rep = lambda x: jnp.repeat(x.reshape(heads, HEAD_A).T, nb, axis=1)
    s0 = wkv0.transpose(1, 3, 2, 0)
    y, st = _scan([to_l(x) for x in pre], s0,
                  rep(p["lnx_g"]), rep(p["lnx_b"]), rep(p["r_k"])[:, None, :], 1)
    y = y.reshape(HEAD_A, heads, nb).transpose(2, 0, 1).reshape(nb, heads * HEAD_A)
    return y, st.transpose(3, 0, 2, 1)


def _layer(x_p, x_s, c_p, c_s, wkv_s, shift_s, p):
    bsz, t_len, d = x_p.shape
    nb = x_s.shape[0]
    heads = d // HEAD_A
    d_b = d // 2
    alpha = p["alpha"]
    mp = bsz * t_len

    assert bsz * heads == LANES and nb == LANES, "RWKV-7 recurrence fills 128 lanes per group"
    swap = functools.partial(_chan_swap, heads=heads)
    mu_main, mu_small = _split_shift(p["mu_shift"][None], d, heads)
    lp = {
        "mu_main": mu_main, "mu_small": mu_small,
        "w0": swap(p["w0"])[None], "w2": swap(p["w2"]), "a0": swap(p["a0"])[None],
        "a2": swap(p["a2"]), "g2": swap(p["g2"]),
        "k_k": swap(p["k_k"])[None], "k_a": swap(p["k_a"])[None],
        "sgu_ln_g": p["sgu_ln_g"][None], "sgu_ln_b": p["sgu_ln_b"][None], "sgu_w": p["sgu_w"],
        "sgu_b_full": jnp.broadcast_to(p["sgu_b"][:, :, None], p["sgu_b"].shape + (GROUP_B,)),
        "sgu_w_first": jnp.repeat(p["sgu_w"][:, 0, 0], GROUP_B)[None],
        "sgu_b_first": jnp.repeat(p["sgu_b"][:, 0], GROUP_B)[None],
    }
    ln1 = (p["ln1_g"][None], p["ln1_b"][None])
    ln2 = (p["ln2_g"][None], p["ln2_b"][None])

    c_all = jnp.concatenate([c_p, c_s], axis=0)
    mod = _modulation(jnp.pad(c_all, ((0, (-c_all.shape[0]) % SUBLANES), (0, 0))),
                      p["w_ada"], p["b_ada"])
    mod_p = mod[:bsz].reshape(bsz, 1, 6 * d)
    mod_s = mod[bsz:bsz + nb]

    w_pack = _pack_w_in(p["w_in"], d, d_b, heads)
    n_pack = w_pack.shape[1]
    tn_in = _pick(n_pack, (1280, 512))
    xp2 = x_p.reshape(mp, d)
    xs2 = x_s.reshape(nb, d)
    tm_big = _pick(t_len, (1024, 512, 256, 128))
    tm_mid = _pick(t_len, (512, 256, 128))
    tm_small = _pick(t_len, (256, 128))
    proj_p = _inproj(xp2, mod_p, w_pack, False, t_len, tm_big, tn_in)
    proj_s = _inproj(xs2, mod_s, w_pack, True, 1, nb, tn_in)

    zm = jnp.zeros((bsz, 1, 3 * d), F32)
    zs = jnp.zeros((bsz, 1, SMALL_PAD), F32)
    *planes_p, g_p = _rwkv_pre(proj_p.reshape(bsz, t_len, n_pack), zm, zs, lp, d, heads, False,
                               _pick(t_len, (32, 16, 8)))
    sm, ss = _split_shift(shift_s, d, heads)
    *pre_s, g_s = _rwkv_pre(proj_s[None], sm[None], ss[None], lp, d, heads, True, nb)
    y_planes, wkv_p = _rwkv_prompt(planes_p, bsz, heads, p, _pick(t_len, (64, 32, 16, 8)))
    ya_p = _from_planes(y_planes, bsz, heads, _pick(t_len, (128, 64, 32, 16, 8))).reshape(mp, d)
    g_p = g_p.reshape(mp, d)
    ya_s, wkv_s_new = _rwkv_sample([x[0] for x in pre_s], nb, heads, wkv_s, p)
    g_s = g_s[0]

    u_blk = (5 * d) // d_b
    yb_p = _sgu_prompt(proj_p, lp, d_b, u_blk, tm_mid)
    yb_s, vrows_s = _sgu_single(proj_s, lp, d_b, u_blk)

    wa = _chan_swap(p["w_br_a"], heads, axis=0).astype(BF16)
    wb = p["w_br_b"].astype(BF16)
    wo = p["w_o"].astype(BF16)
    wq = p["peer_wq"].astype(BF16)
    tn_mrg = _pick(d, (512, 256, 128))
    mrg_p = _merge(ya_p, g_p, yb_p, proj_p, wa, wb, d, tm_mid, tn_mrg)
    mrg_s = _merge(ya_s, g_s, yb_s, proj_s, wa, wb, d, nb, tn_mrg)
    x1_p, h2_p, q_p = _out_ln(mrg_p, xp2, mod_p, wo, wq, *ln1, False, t_len, alpha, tm_small)
    x1_s, h2_s, q_s = _out_ln(mrg_s, xs2, mod_s, wo, wq, *ln1, True, 1, alpha, nb)

    m_all = mp + nb
    tt = _dense_token_tile(m_all)
    pad_rows = (-m_all) % tt
    h2 = jnp.concatenate([h2_p, h2_s, jnp.zeros((pad_rows, d), BF16)], axis=0)
    q = jnp.concatenate([q_p, q_s, jnp.zeros((pad_rows, q_p.shape[1]), BF16)], axis=0)
    n1, p1, r2, p2 = _peer_topk(q, p["peer_keys"], LANES)
    ft = _peer_dense(h2, p["peer_u"].astype(BF16), p["peer_v"].T.astype(BF16), n1, p1, r2, p2,
                     tt, 4 * N_KEYS)

    y_p = _final_ln(x1_p, ft, 0, mod_p, *ln2, False, t_len, alpha, tm_small)
    y_s = _final_ln(x1_s, ft, mp, mod_s, *ln2, True, 1, alpha, nb)

    last = proj_p.reshape(bsz, t_len, -1)[:, -1]
    return (y_p.reshape(bsz, t_len, d), y_s.reshape(nb, 1, d), wkv_p, _join_shift(last, d, heads),
            wkv_s_new, _join_shift(proj_s, d, heads), vrows_s.reshape(nb, 1, d_b))


def kernel(x_prompt, x_sample, c_prompt, c_sample, state_wkv, state_shift, w_ada, b_ada, w_in, mu_shift, w0, w2, a0, a2, g2, k_k, k_a, r_k, lnx_g, lnx_b, sgu_ln_g, sgu_ln_b, sgu_w, sgu_b, w_br_a, w_br_b, w_o, ln1_g, ln1_b, peer_wq, peer_keys, peer_u, peer_v, ln2_g, ln2_b):
    names = ("w_ada", "b_ada", "w_in", "mu_shift", "w0", "w2", "a0", "a2", "g2", "k_k", "k_a",
             "r_k", "lnx_g", "lnx_b", "sgu_ln_g", "sgu_ln_b", "sgu_w", "sgu_b", "w_br_a",
             "w_br_b", "w_o", "ln1_g", "ln1_b", "peer_wq", "peer_keys", "peer_u", "peer_v",
             "ln2_g", "ln2_b")
    stacked = (w_ada, b_ada, w_in, mu_shift, w0, w2, a0, a2, g2, k_k, k_a, r_k, lnx_g, lnx_b,
               sgu_ln_g, sgu_ln_b, sgu_w, sgu_b, w_br_a, w_br_b, w_o, ln1_g, ln1_b, peer_wq,
               peer_keys, peer_u, peer_v, ln2_g, ln2_b)
    depth = w_ada.shape[0]
    alpha = (2 * depth) ** 0.25
    y_p, y_s = x_prompt, x_sample
    outs = [[] for _ in range(5)]
    for l in range(depth):
        p = {n: a[l] for n, a in zip(names, stacked)}
        p["alpha"] = alpha
        y_p, y_s, *state = _layer(y_p, y_s, c_prompt, c_sample, state_wkv[l], state_shift[l], p)
        for acc, s in zip(outs, state):
            acc.append(s)
    return (y_p, y_s) + tuple(jnp.stack(o) for o in outs)
```

```python
import functools

import jax
import jax.numpy as jnp
from jax import lax
from jax.experimental import pallas as pl
from jax.experimental.pallas import tpu as pltpu

F32 = jnp.float32
BF16 = jnp.bfloat16

HEAD_A = 64
R_DECAY = 96
R_AAA = 96
R_GATE = 256
SMALL = R_DECAY + R_AAA + R_GATE
SMALL_PAD = 512
CHUNK = 128
GROUP_B = 128
PEER_HEADS = 8
N_KEYS = 128
DK_HALF = 128
TOPK = 16
LN_EPS = 1e-5
GN_EPS = 64e-5

LANES = 128
SUBLANES = 8
VMEM_LIMIT = 56 * 1024 * 1024


def _cparams(*sem):
    return pltpu.CompilerParams(dimension_semantics=sem, vmem_limit_bytes=VMEM_LIMIT)


def _gelu(x):
    return jax.nn.gelu(x, approximate=True)


def _layer_norm_rows(x, g, b, eps):
    mu = jnp.mean(x, axis=-1, keepdims=True)
    xc = x - mu
    var = jnp.mean(xc * xc, axis=-1, keepdims=True)
    return xc * lax.rsqrt(var + eps) * g + b


def _pick(n, prefs):
    for p in prefs:
        if n % p == 0:
            return p
    return n


def _mod_spec(per_row, tm, d, chunk, rows_per_seq):
    if per_row:
        return pl.BlockSpec((tm, d), lambda i, *_: (i, chunk))
    return pl.BlockSpec((None, 1, d), lambda i, *_: ((i * tm) // rows_per_seq, 0, chunk))


def _mod_kernel(c_ref, w_ref, b_ref, o_ref):
    c = c_ref[...]
    s = (c * jax.nn.sigmoid(c)).astype(BF16)
    o_ref[...] = jnp.dot(s, w_ref[...].astype(BF16), preferred_element_type=F32) + b_ref[...]


def _modulation(c_all, w_ada, b_ada):
    m, d = c_all.shape
    n = w_ada.shape[1]
    tn = _pick(n, (1536, 512))
    return pl.pallas_call(
        _mod_kernel,
        grid=(n // tn,),
        in_specs=[pl.BlockSpec((m, d), lambda j: (0, 0)),
                  pl.BlockSpec((d, tn), lambda j: (0, j)),
                  pl.BlockSpec((1, tn), lambda j: (0, j))],
        out_specs=pl.BlockSpec((m, tn), lambda j: (0, j)),
        out_shape=jax.ShapeDtypeStruct((m, n), F32),
        compiler_params=_cparams("arbitrary"),
        name="adaln_mod",
    )(c_all, w_ada, b_ada.reshape(1, n))


def _inproj_kernel(x_ref, sh_ref, sc_ref, w_ref, o_ref, h_scr):
    @pl.when(pl.program_id(1) == 0)
    def _():
        h_scr[...] = (x_ref[...] * (1.0 + sc_ref[...]) + sh_ref[...]).astype(BF16)

    o_ref[...] = jnp.dot(h_scr[...], w_ref[...], preferred_element_type=F32)


def _inproj(x, mod, w_pack, per_row, seq_len, tm, tn):
    m, d = x.shape
    n = w_pack.shape[1]
    return pl.pallas_call(
        _inproj_kernel,
        grid=(m // tm, n // tn),
        in_specs=[pl.BlockSpec((tm, d), lambda i, j: (i, 0)),
                  _mod_spec(per_row, tm, d, 0, seq_len),
                  _mod_spec(per_row, tm, d, 1, seq_len),
                  pl.BlockSpec((d, tn), lambda i, j: (0, j))],
        out_specs=pl.BlockSpec((tm, tn), lambda i, j: (i, j)),
        out_shape=jax.ShapeDtypeStruct((m, n), F32),
        scratch_shapes=[pltpu.VMEM((tm, d), BF16)],
        compiler_params=_cparams("arbitrary", "arbitrary"),
        name="in_proj",
    )(x, mod, mod, w_pack)


def _swap_lane_groups(xs, width):
    n = len(xs)
    grp = lax.broadcasted_iota(jnp.int32, xs[0].shape, 1) // width
    moved = []
    for s in range(n):
        acc = xs[0]
        for b in range(1, n):
            acc = jnp.where(grp == (b - s) % n, xs[b], acc)
        moved.append(acc if s == 0 else pltpu.roll(acc, s * width, 1))
    outs = []
    for q in range(n):
        acc = moved[0]
        for s in range(1, n):
            acc = jnp.where(grp == (q + s) % n, moved[s], acc)
        outs.append(acc)
    return outs


def _to_planes(xs, o_ref, heads):
    per = LANES // heads
    for j in range(xs[0].shape[1] // LANES):
        outs = _swap_lane_groups([x[:, j * LANES:(j + 1) * LANES] for x in xs], heads)
        for q in range(per):
            o_ref[j * per + q] = outs[q]


def _from_planes_kernel(y_ref, o_ref, *, heads):
    per = LANES // heads
    for j in range(y_ref.shape[0] // per):
        outs = _swap_lane_groups([y_ref[j * per + q] for q in range(per)], heads)
        for b in range(per):
            o_ref[b, :, j * LANES:(j + 1) * LANES] = outs[b]


def _from_planes(y, nseq, heads, tm):
    n, t_len, _ = y.shape
    d = n * heads
    return pl.pallas_call(
        functools.partial(_from_planes_kernel, heads=heads),
        grid=(t_len // tm,),
        in_specs=[pl.BlockSpec((n, tm, LANES), lambda i: (0, i, 0))],
        out_specs=pl.BlockSpec((nseq, tm, d), lambda i: (0, i, 0)),
        out_shape=jax.ShapeDtypeStruct((nseq, t_len, d), F32),
        compiler_params=_cparams("arbitrary"),
        name="rwkv_from_planes",
    )(y)


def _rwkv_pre_kernel(pm_ref, ps_ref, qm_ref, qs_ref, stm_ref, sts_ref, mum_ref, mus_ref,
                     w0_ref, w2_ref, a0_ref, a2_ref, g2_ref, kk_ref, ka_ref,
                     r_o, w_o, k_o, v_o, kk_o, a_o, g_o, *, d, heads, per_row_state):
    nseq, tm, _ = pm_ref.shape

    def shifted(p, q, st):
        if per_row_state:
            return st
        first = pl.program_id(0) == 0
        prev_row = jnp.where(first, st, q[SUBLANES - 1:SUBLANES, :])
        row = lax.broadcasted_iota(jnp.int32, p.shape, 0)
        return jnp.where(row == 0, prev_row, pltpu.roll(p, 1, 0))

    def mix(p, q, st, mu):
        return p + (shifted(p, q, st) - p) * mu

    seqs = range(nseq)
    xs_s = jnp.concatenate([mix(ps_ref[b], qs_ref[b], sts_ref[b], mus_ref[...]) for b in seqs],
                           axis=0)
    wl = xs_s[:, 0:R_DECAY]
    al = xs_s[:, R_DECAY:R_DECAY + R_AAA]
    gl = xs_s[:, R_DECAY + R_AAA:SMALL]
    wlin = w0_ref[...] + jnp.dot(jnp.tanh(wl).astype(BF16), w2_ref[...].astype(BF16),
                                 preferred_element_type=F32)
    w = -jax.nn.softplus(-wlin) - 0.5
    decay = jnp.exp(-jnp.exp(w))
    a = jax.nn.sigmoid(a0_ref[...] + jnp.dot(al.astype(BF16), a2_ref[...].astype(BF16),
                                             preferred_element_type=F32))
    g = jnp.dot(jax.nn.sigmoid(gl).astype(BF16), g2_ref[...].astype(BF16),
                preferred_element_type=F32)

    def of_seq(x, b):
        return x[b * tm:(b + 1) * tm]

    def part(n, b):
        cols = slice(n * d, (n + 1) * d)
        return mix(pm_ref[b, :, cols], qm_ref[b, :, cols], stm_ref[b, :, cols], mum_ref[:, cols])

    def emit(o_ref, xs):
        if per_row_state:
            for b in seqs:
                o_ref[b] = xs[b]
        else:
            _to_planes(xs, o_ref, heads)

    for b in seqs:
        g_o[b] = of_seq(g, b)
    emit(r_o, [part(0, b) for b in seqs])
    emit(w_o, [of_seq(decay, b) for b in seqs])
    ks = [part(1, b) for b in seqs]
    emit(kk_o, [ks[b] * kk_ref[...] for b in seqs])
    emit(k_o, [ks[b] * (1.0 + (of_seq(a, b) - 1.0) * ka_ref[...]) for b in seqs])
    emit(v_o, [part(2, b) for b in seqs])
    emit(a_o, [of_seq(a, b) for b in seqs])


def _rwkv_pre(proj, state_main, state_small, lp, d, heads, per_row_state, tm):
    nseq, t_len, _ = proj.shape
    small_blk = (6 * d) // SMALL_PAD

    def prev_map(col):
        def f(i):
            return (0, jnp.maximum(i * (tm // SUBLANES) - 1, 0), col)
        return f

    if per_row_state:
        stm_spec = pl.BlockSpec((nseq, tm, 3 * d), lambda i: (0, i, 0))
        sts_spec = pl.BlockSpec((nseq, tm, SMALL_PAD), lambda i: (0, i, 0))
    else:
        stm_spec = pl.BlockSpec((nseq, 1, 3 * d), lambda i: (0, 0, 0))
        sts_spec = pl.BlockSpec((nseq, 1, SMALL_PAD), lambda i: (0, 0, 0))

    vec = lambda w: pl.BlockSpec((1, w), lambda i: (0, 0))
    full = lambda a: pl.BlockSpec(a.shape, lambda i: (0,) * a.ndim)
    rows = pl.BlockSpec((nseq, tm, d), lambda i: (0, i, 0))
    rows_shape = jax.ShapeDtypeStruct((nseq, t_len, d), F32)
    if per_row_state:
        vec_out, vec_shape = rows, rows_shape
    else:
        vec_out = pl.BlockSpec((HEAD_A, tm, LANES), lambda i: (0, i, 0))
        vec_shape = jax.ShapeDtypeStruct((HEAD_A, t_len, LANES), F32)
    kern = functools.partial(_rwkv_pre_kernel, d=d, heads=heads, per_row_state=per_row_state)
    in_specs = [pl.BlockSpec((nseq, tm, 3 * d), lambda i: (0, i, 0)),
                pl.BlockSpec((nseq, tm, SMALL_PAD), lambda i: (0, i, small_blk)),
                pl.BlockSpec((nseq, SUBLANES, 3 * d), prev_map(0)),
                pl.BlockSpec((nseq, SUBLANES, SMALL_PAD), prev_map(small_blk)),
                stm_spec, sts_spec, vec(3 * d), vec(SMALL_PAD), vec(d), full(lp["w2"]), vec(d),
                full(lp["a2"]), full(lp["g2"]), vec(d), vec(d)]
    return pl.pallas_call(
        kern,
        grid=(t_len // tm,),
        in_specs=in_specs,
        out_specs=[vec_out] * 6 + [rows],
        out_shape=[vec_shape] * 6 + [rows_shape],
        compiler_params=_cparams("arbitrary"),
        name="rwkv_pre",
    )(proj, proj, proj, proj, state_main, state_small, lp["mu_main"], lp["mu_small"], lp["w0"],
      lp["w2"], lp["a0"], lp["a2"], lp["g2"], lp["k_k"], lp["k_a"])


def _scan_kernel(r_ref, w_ref, k_ref, v_ref, kk_ref, a_ref, s0_ref, lng_ref, lnb_ref, rk_ref,
                 y_ref, st_ref, s_scr, aa_scr, bb_scr, bon_scr, yt_scr, *, tc):
    n = HEAD_A
    c = pl.program_id(1)

    @pl.when(c == 0)
    def _():
        s_scr[...] = s0_ref[...]

    kk = kk_ref[...]
    nrm = jnp.sqrt(jnp.sum(kk * kk, axis=0, keepdims=True))
    kkn = kk / jnp.maximum(nrm, 1e-12)
    aa_scr[...] = -kkn
    bb_scr[...] = kkn * a_ref[...]
    bon_scr[...] = jnp.sum(r_ref[...] * k_ref[...] * rk_ref[...], axis=0)

    zeros = jnp.zeros((n, LANES), F32)
    nblk = n // SUBLANES
    sub = lax.broadcasted_iota(jnp.int32, (SUBLANES, LANES), 0)

    def row(ref, i, t):
        return ref[i, pl.ds(t, 1), :]

    def tile_of_step(ref, t):
        blocks = []
        for m in range(nblk):
            acc = jnp.broadcast_to(row(ref, m * SUBLANES, t), (SUBLANES, LANES))
            for s in range(1, SUBLANES):
                acc = jnp.where(sub == s, row(ref, m * SUBLANES + s, t), acc)
            blocks.append(acc)
        return jnp.concatenate(blocks, axis=0)

    def sa_first(kb, acc):
        for j in range(SUBLANES):
            i = kb * SUBLANES + j
            acc = acc + s_scr[i] * row(aa_scr, i, 0)
        return acc

    sa0 = lax.fori_loop(0, nblk, sa_first, zeros)

    def step(t, sa):
        tn = jnp.minimum(t + 1, tc - 1)
        vt = tile_of_step(v_ref, t)

        def kblock(kb, carry):
            y, san = carry
            for j in range(SUBLANES):
                i = kb * SUBLANES + j
                sn = s_scr[i] * row(w_ref, i, t) + sa * row(bb_scr, i, t) + vt * row(k_ref, i, t)
                s_scr[i] = sn
                y = y + sn * row(r_ref, i, t)
                san = san + sn * row(aa_scr, i, tn)
            return y, san

        y, san = lax.fori_loop(0, nblk, kblock, (zeros, zeros), unroll=True)
        return san, y

    grp = min(SUBLANES, tc)

    def group(gb, sa):
        t0 = pl.multiple_of(gb * grp, grp)
        rows = pl.ds(t0, grp)

        def one(s, sa):
            sa, y = step(t0 + s, sa)
            yt_scr[s] = y
            return sa

        sa = lax.fori_loop(0, grp, one, sa)
        total = jnp.zeros((grp, LANES), F32)
        for i in range(n):
            acc = jnp.broadcast_to(yt_scr[0, i:i + 1, :], (grp, LANES))
            for s in range(1, grp):
                acc = jnp.where(sub[:grp] == s, yt_scr[s, i:i + 1, :], acc)
            y_ref[i, rows, :] = acc
            total = total + acc
        mu = total * (1.0 / n)
        sq = jnp.zeros((grp, LANES), F32)
        for i in range(n):
            dlt = y_ref[i, rows, :] - mu
            sq = sq + dlt * dlt
        rstd = lax.rsqrt(sq * (1.0 / n) + GN_EPS)
        bon = bon_scr[rows, :]
        for i in range(n):
            yn = (y_ref[i, rows, :] - mu) * rstd * lng_ref[i] + lnb_ref[i]
            y_ref[i, rows, :] = yn + bon * v_ref[i, rows, :]
        return sa

    lax.fori_loop(0, tc // grp, group, sa0)

    @pl.when(c == pl.num_programs(1) - 1)
    def _():
        st_ref[...] = s_scr[...]


def _scan(seq, s0, lng, lnb, rk, tc):
    n, t_len, gl = seq[0].shape
    g = gl // LANES
    step_spec = pl.BlockSpec((n, tc, LANES), lambda gi, c: (0, c, gi))
    par_spec = pl.BlockSpec((n, 1, LANES), lambda gi, c: (0, 0, gi))
    st_spec = pl.BlockSpec((None, n, n, LANES), lambda gi, c: (gi, 0, 0, 0))
    return pl.pallas_call(
        functools.partial(_scan_kernel, tc=tc),
        grid=(g, t_len // tc),
        in_specs=[step_spec] * 6 + [st_spec, par_spec, par_spec, par_spec],
        out_specs=[step_spec, st_spec],
        out_shape=[jax.ShapeDtypeStruct((n, t_len, gl), F32),
                   jax.ShapeDtypeStruct((g, n, n, LANES), F32)],
        scratch_shapes=[pltpu.VMEM((n, n, LANES), F32),
                        pltpu.VMEM((n, tc, LANES), F32),
                        pltpu.VMEM((n, tc, LANES), F32),
                        pltpu.VMEM((tc, LANES), F32),
                        pltpu.VMEM((min(SUBLANES, tc), n, LANES), F32)],
        compiler_params=_cparams("arbitrary", "arbitrary"),
        name="rwkv_scan",
    )(*seq, s0, lng, lnb, rk)


def _sgu_kernel(pu_ref, pv_ref, lg_ref, lb_ref, w_ref, b_ref, y_ref, *, chunks):
    ng = w_ref.shape[0]
    row = lax.broadcasted_iota(jnp.int32, (CHUNK, CHUNK), 0)
    col = lax.broadcasted_iota(jnp.int32, (CHUNK, CHUNK), 1)
    causal = col <= row
    for ci in range(chunks):
        rows = slice(ci * CHUNK, (ci + 1) * CHUNK)
        u = _gelu(pu_ref[rows, :])
        v = _layer_norm_rows(_gelu(pv_ref[rows, :]), lg_ref[...], lb_ref[...], LN_EPS)
        for g in range(ng):
            cols = slice(g * GROUP_B, (g + 1) * GROUP_B)
            ws = jnp.where(causal, w_ref[g], 0.0).astype(BF16)
            mixed = jnp.dot(ws, v[:, cols].astype(BF16), preferred_element_type=F32) + b_ref[g]
            y_ref[rows, cols] = (u[:, cols] * mixed).astype(y_ref.dtype)


def _sgu_prompt(proj, lp, d_b, u_blk, tm):
    m = proj.shape[0]
    ng = d_b // GROUP_B
    return pl.pallas_call(
        functools.partial(_sgu_kernel, chunks=tm // CHUNK),
        grid=(m // tm,),
        in_specs=[pl.BlockSpec((tm, d_b), lambda i: (i, u_blk)),
                  pl.BlockSpec((tm, d_b), lambda i: (i, u_blk + 1)),
                  pl.BlockSpec((1, d_b), lambda i: (0, 0)),
                  pl.BlockSpec((1, d_b), lambda i: (0, 0)),
                  pl.BlockSpec((ng, CHUNK, CHUNK), lambda i: (0, 0, 0)),
                  pl.BlockSpec((ng, CHUNK, GROUP_B), lambda i: (0, 0, 0))],
        out_specs=pl.BlockSpec((tm, d_b), lambda i: (i, 0)),
        out_shape=jax.ShapeDtypeStruct((m, d_b), BF16),
        compiler_params=_cparams("arbitrary"),
        name="sgu_chunked",
    )(proj, proj, lp["sgu_ln_g"], lp["sgu_ln_b"], lp["sgu_w"], lp["sgu_b_full"])


def _sgu_first_kernel(pu_ref, pv_ref, lg_ref, lb_ref, wd_ref, bd_ref, y_ref, v_ref):
    u = _gelu(pu_ref[...])
    v = _layer_norm_rows(_gelu(pv_ref[...]), lg_ref[...], lb_ref[...], LN_EPS)
    v_ref[...] = v
    vb = v.astype(BF16).astype(F32)
    wd = wd_ref[...].astype(BF16).astype(F32)
    y_ref[...] = (u * (vb * wd + bd_ref[...])).astype(y_ref.dtype)


def _sgu_single(proj, lp, d_b, u_blk):
    m = proj.shape[0]
    vec = pl.BlockSpec((1, d_b), lambda i: (0, 0))
    return pl.pallas_call(
        _sgu_first_kernel,
        grid=(1,),
        in_specs=[pl.BlockSpec((m, d_b), lambda i: (0, u_blk)),
                  pl.BlockSpec((m, d_b), lambda i: (0, u_blk + 1)), vec, vec, vec, vec],
        out_specs=[pl.BlockSpec((m, d_b), lambda i: (0, 0))] * 2,
        out_shape=[jax.ShapeDtypeStruct((m, d_b), BF16), jax.ShapeDtypeStruct((m, d_b), F32)],
        compiler_params=_cparams("arbitrary"),
        name="sgu_single",
    )(proj, proj, lp["sgu_ln_g"], lp["sgu_ln_b"], lp["sgu_w_first"], lp["sgu_b_first"])


def _merge_kernel(ya_ref, g_ref, yb_ref, ga_ref, gb_ref, wa_ref, wb_ref, o_ref, a_scr):
    @pl.when(pl.program_id(1) == 0)
    def _():
        a_scr[...] = (ya_ref[...] * g_ref[...]).astype(BF16)

    ta = jnp.dot(a_scr[...], wa_ref[...], preferred_element_type=F32)
    tb = jnp.dot(yb_ref[...], wb_ref[...], preferred_element_type=F32)
    o_ref[...] = (jax.nn.sigmoid(ga_ref[...]) * ta
                  + jax.nn.sigmoid(gb_ref[...]) * tb).astype(o_ref.dtype)


def _merge(ya, g, yb, proj, wa, wb, d, tm, tn):
    m = ya.shape[0]
    d_b = yb.shape[1]
    ga_blk = (3 * d) // tn
    gb_blk = (4 * d) // tn
    return pl.pallas_call(
        _merge_kernel,
        grid=(m // tm, d // tn),
        in_specs=[pl.BlockSpec((tm, d), lambda i, j: (i, 0)),
                  pl.BlockSpec((tm, d), lambda i, j: (i, 0)),
                  pl.BlockSpec((tm, d_b), lambda i, j: (i, 0)),
                  pl.BlockSpec((tm, tn), lambda i, j: (i, ga_blk + j)),
                  pl.BlockSpec((tm, tn), lambda i, j: (i, gb_blk + j)),
                  pl.BlockSpec((d, tn), lambda i, j: (0, j)),
                  pl.BlockSpec((d_b, tn), lambda i, j: (0, j))],
        out_specs=pl.BlockSpec((tm, tn), lambda i, j: (i, j)),
        out_shape=jax.ShapeDtypeStruct((m, d), BF16),
        scratch_shapes=[pltpu.VMEM((tm, d), BF16)],
        compiler_params=_cparams("arbitrary", "arbitrary"),
        name="branch_merge",
    )(ya, g, yb, proj, proj, wa, wb)


def _out_ln_kernel(m_ref, x_ref, gt_ref, sh_ref, sc_ref, wo_ref, wq_ref, g_ref, b_ref,
                   x1_ref, h2_ref, q_ref, *, alpha):
    z = jnp.dot(m_ref[...], wo_ref[...], preferred_element_type=F32)
    x1 = _layer_norm_rows(alpha * x_ref[...] + gt_ref[...] * z, g_ref[...], b_ref[...], LN_EPS)
    x1_ref[...] = x1
    h2 = (x1 * (1.0 + sc_ref[...]) + sh_ref[...]).astype(BF16)
    h2_ref[...] = h2
    q_ref[...] = jnp.dot(h2, wq_ref[...], preferred_element_type=F32).astype(q_ref.dtype)


def _out_ln(mrg, x, mod, wo, wq, ln_g, ln_b, per_row, seq_len, alpha, tm):
    m, d = x.shape
    nq = wq.shape[1]
    vec = pl.BlockSpec((1, d), lambda i: (0, 0))
    row = pl.BlockSpec((tm, d), lambda i: (i, 0))
    return pl.pallas_call(
        functools.partial(_out_ln_kernel, alpha=alpha),
        grid=(m // tm,),
        in_specs=[row, row,
                  _mod_spec(per_row, tm, d, 2, seq_len),
                  _mod_spec(per_row, tm, d, 3, seq_len),
                  _mod_spec(per_row, tm, d, 4, seq_len),
                  pl.BlockSpec((d, d), lambda i: (0, 0)),
                  pl.BlockSpec((d, nq), lambda i: (0, 0)), vec, vec],
        out_specs=[row, row, pl.BlockSpec((tm, nq), lambda i: (i, 0))],
        out_shape=[jax.ShapeDtypeStruct((m, d), F32), jax.ShapeDtypeStruct((m, d), BF16),
                   jax.ShapeDtypeStruct((m, nq), BF16)],
        compiler_params=_cparams("arbitrary"),
        name="out_proj_ln1",
    )(mrg, x, mod, mod, mod, wo, wq, ln_g, ln_b)


_PAIRS = [(i, j) for i in range(TOPK) for j in range(TOPK) if (i + 1) * (j + 1) <= TOPK]
_NPAIR_PAD = -(-len(_PAIRS) // SUBLANES) * SUBLANES


def _take_top(work, rounds, break_ties):
    rows = work.shape[0]
    iota = lax.broadcasted_iota(jnp.int32, work.shape, 0).astype(F32)
    rank = jnp.full(work.shape, float(rounds), F32)
    vals = []
    for r in range(rounds):
        mx = jnp.max(work, axis=0, keepdims=True)
        hit = work == mx
        if break_ties:
            first = jnp.min(jnp.where(hit, iota, float(rows)), axis=0, keepdims=True)
            hit = iota == first
        rank = jnp.where(hit, float(r), rank)
        work = jnp.where(hit, -jnp.inf, work)
        vals.append(mx)
    return vals, rank


def _exactly_k(rank, k):
    n = jnp.sum(jnp.where(rank < float(k), 1.0, 0.0), axis=0, keepdims=True)
    return jnp.where(n == float(k), 1.0, 0.0)


def _head_select(s1, s2, cnt_mat, cand_scr, break_ties):
    a1, rank1 = _take_top(s1, TOPK, break_ties)
    a2, rank2 = _take_top(s2, TOPK, break_ties)
    for pos, (i, j) in enumerate(_PAIRS):
        cand_scr[pos:pos + 1, :] = a1[i] + a2[j]
    cand = cand_scr[...]
    _, rank_c = _take_top(cand, TOPK, break_ties)
    sel = jnp.where(rank_c < float(TOPK), 1.0, 0.0)
    top = a1[0] + a2[0]
    z = jnp.sum(sel * jnp.exp(cand - top), axis=0, keepdims=True)
    cnt = jnp.dot(cnt_mat, sel.astype(BF16), preferred_element_type=F32)
    n1 = jnp.zeros(s1.shape, F32)
    for i in range(TOPK):
        n1 = jnp.where(rank1 == float(i), cnt[i:i + 1, :], n1)
    p1 = jnp.where(rank1 < float(TOPK), jnp.exp(s1 - a1[0]), 0.0)
    p2 = jnp.where(rank2 < float(TOPK), jnp.exp(s2 - a2[0]), 0.0) / z
    clean = _exactly_k(rank1, TOPK) * _exactly_k(rank2, TOPK) * _exactly_k(rank_c, TOPK)
    return n1, p1, rank2, p2, clean


def _peer_topk_kernel(q_ref, keys_ref, cnt_ref, n1_ref, p1_ref, r2_ref, p2_ref, cand_a, cand_b):
    nt = (((1,), (1,)), ((), ()))
    cand_a[...] = jnp.full(cand_a.shape, -jnp.inf, F32)
    cand_b[...] = jnp.full(cand_b.shape, -jnp.inf, F32)

    def scores(h):
        base = pl.multiple_of(h * 2 * DK_HALF, 2 * DK_HALF)
        q1 = q_ref[:, pl.ds(base, DK_HALF)]
        q2 = q_ref[:, pl.ds(base + DK_HALF, DK_HALF)]
        s1 = lax.dot_general(keys_ref[h, 0].astype(BF16), q1, nt, preferred_element_type=F32)
        s2 = lax.dot_general(keys_ref[h, 1].astype(BF16), q2, nt, preferred_element_type=F32)
        return s1, s2

    def write(h, res):
        n1_ref[h], p1_ref[h] = res[:2]
        r2_ref[h] = res[2].astype(r2_ref.dtype)
        p2_ref[h] = res[3].astype(p2_ref.dtype)

    def head_pair(hp, carry):
        heads = (2 * hp, 2 * hp + 1)
        scr = (cand_a, cand_b)
        sc = [scores(h) for h in heads]
        res = [_head_select(*sc[n], cnt_ref[...], scr[n], False) for n in range(2)]
        for n in range(2):
            write(heads[n], res[n])
        for n in range(2):
            @pl.when(jnp.min(res[n][4]) < 0.5)
            def _():
                write(heads[n], _head_select(*sc[n], cnt_ref[...], scr[n], True))
        return carry

    lax.fori_loop(0, PEER_HEADS // 2, head_pair, 0)


def _pair_count_matrix():
    return jnp.array([[1.0 if i == r else 0.0 for (i, _) in _PAIRS]
                      + [0.0] * (_NPAIR_PAD - len(_PAIRS)) for r in range(TOPK)], BF16)


def _peer_topk(q, keys, tt):
    m = q.shape[0]
    cnt = _pair_count_matrix()
    out = pl.BlockSpec((PEER_HEADS, N_KEYS, tt), lambda i: (0, 0, i))
    return pl.pallas_call(
        _peer_topk_kernel,
        grid=(m // tt,),
        in_specs=[pl.BlockSpec((tt, q.shape[1]), lambda i: (i, 0)),
                  pl.BlockSpec(keys.shape, lambda i: (0, 0, 0, 0)),
                  pl.BlockSpec(cnt.shape, lambda i: (0, 0))],
        out_specs=[out] * 4,
        out_shape=[jax.ShapeDtypeStruct((PEER_HEADS, N_KEYS, m), dt)
                   for dt in (F32, F32, BF16, BF16)],
        scratch_shapes=[pltpu.VMEM((_NPAIR_PAD, tt), F32)] * 2,
        compiler_params=_cparams("arbitrary"),
        name="peer_topk",
    )(q, keys, cnt)


def _peer_dense_kernel(h_ref, u_ref, vt_ref, n1_ref, p1_ref, r2_ref, p2_ref, o_ref, w_scr, g_scr,
                       *, et, n_et):
    n = pl.program_id(0)
    cur = jnp.minimum(n, pl.num_programs(0) - 2)
    e_cur = cur % n_et
    e_prev = jnp.maximum(n - 1, 0) % n_et
    nt = (((1,), (1,)), ((), ()))

    @pl.when(n == 0)
    def _():
        w_scr[1] = jnp.zeros(w_scr.shape[1:], BF16)

    @pl.when(e_prev == 0)
    def _():
        o_ref[...] = jnp.zeros(o_ref.shape, F32)

    zero = jnp.zeros((), BF16)
    for j in range(et // N_KEYS):
        e1 = e_cur * (et // N_KEYS) + j
        gate = jnp.zeros((N_KEYS, g_scr.shape[1]), BF16)
        for h in range(PEER_HEADS):
            n1 = n1_ref[h, pl.ds(e1, 1), :].astype(BF16)
            p1 = p1_ref[h, pl.ds(e1, 1), :].astype(BF16)
            gate = gate + jnp.where(r2_ref[h] < n1, p2_ref[h], zero) * p1
        g_scr[j * N_KEYS:(j + 1) * N_KEYS, :] = gate
    s = lax.dot_general(u_ref[...], h_ref[...], nt, preferred_element_type=F32)
    o_ref[...] += jnp.dot(vt_ref[...], w_scr[(n + 1) % 2], preferred_element_type=F32)
    w_scr[n % 2] = _gelu(s).astype(BF16) * g_scr[...]


def _peer_dense(h2, u_tab, vt_tab, n1, p1, r2, p2, tt, et):
    m, d = h2.shape
    n_et = u_tab.shape[0] // et
    steps = (m // tt) * n_et

    def cur(fn):
        return lambda n: fn(jnp.minimum(n, steps - 1))

    def prev(fn):
        return lambda n: fn(jnp.maximum(n - 1, 0))

    once = pl.Buffered(1)
    sel = pl.BlockSpec((PEER_HEADS, N_KEYS, tt), cur(lambda c: (0, 0, c // n_et)),
                       pipeline_mode=once)
    return pl.pallas_call(
        functools.partial(_peer_dense_kernel, et=et, n_et=n_et),
        grid=(steps + 1,),
        in_specs=[pl.BlockSpec((tt, d), cur(lambda c: (c // n_et, 0))),
                  pl.BlockSpec((et, d), cur(lambda c: (c % n_et, 0))),
                  pl.BlockSpec((d, et), prev(lambda c: (0, c % n_et))),
                  sel, sel, sel, sel],
        out_specs=pl.BlockSpec((d, tt), prev(lambda c: (0, c // n_et))),
        out_shape=jax.ShapeDtypeStruct((d, m), F32),
        scratch_shapes=[pltpu.VMEM((2, et, tt), BF16), pltpu.VMEM((et, tt), BF16)],
        compiler_params=_cparams("arbitrary"),
        name="peer_dense",
    )(h2, u_tab, vt_tab, n1, p1, r2, p2)


def _final_kernel(x1_ref, ft_ref, gt_ref, g_ref, b_ref, o_ref, *, alpha):
    f = ft_ref[...].T
    o_ref[...] = _layer_norm_rows(alpha * x1_ref[...] + gt_ref[...] * f,
                                  g_ref[...], b_ref[...], LN_EPS)


def _final_ln(x1, ft, first_token, mod, ln_g, ln_b, per_row, seq_len, alpha, tm):
    m, d = x1.shape
    col0 = first_token // tm
    row = pl.BlockSpec((tm, d), lambda i: (i, 0))
    vec = pl.BlockSpec((1, d), lambda i: (0, 0))
    return pl.pallas_call(
        functools.partial(_final_kernel, alpha=alpha),
        grid=(m // tm,),
        in_specs=[row, pl.BlockSpec((d, tm), lambda i: (0, col0 + i)),
                  _mod_spec(per_row, tm, d, 5, seq_len), vec, vec],
        out_specs=row,
        out_shape=jax.ShapeDtypeStruct((m, d), F32),
        compiler_params=_cparams("arbitrary"),
        name="final_ln2",
    )(x1, ft, mod, ln_g, ln_b)


def _pack_w_in(w_in, d, d_b, heads):
    o_small = 3 * d
    o_u = o_small + SMALL
    o_v = o_u + d_b
    o_ga = o_v + d_b
    o_gb = o_ga + d
    small = jnp.pad(w_in[:, o_small:o_u], ((0, 0), (0, SMALL_PAD - SMALL)))
    rkv = _chan_swap(w_in[:, :o_small].reshape(-1, 3, d), heads).reshape(-1, o_small)
    return jnp.concatenate([rkv, w_in[:, o_ga:o_gb], w_in[:, o_gb:o_gb + d],
                            w_in[:, o_u:o_v], w_in[:, o_v:o_ga], small], axis=1).astype(BF16)


def _dense_token_tile(m):
    for tt in (768, 512, 256):
        if (-m) % tt <= m // 20:
            return tt
    return LANES


def _chan_swap(x, heads, axis=-1):
    axis = axis % x.ndim
    shp = x.shape
    x = x.reshape(shp[:axis] + (heads, HEAD_A) + shp[axis + 1:])
    return jnp.swapaxes(x, axis, axis + 1).reshape(shp)


def _chan_unswap(x, heads, axis=-1):
    axis = axis % x.ndim
    shp = x.shape
    x = x.reshape(shp[:axis] + (HEAD_A, heads) + shp[axis + 1:])
    return jnp.swapaxes(x, axis, axis + 1).reshape(shp)


def _split_shift(s, d, heads):
    main = _chan_swap(s[:, :3 * d].reshape(-1, 3, d), heads).reshape(-1, 3 * d)
    return main, jnp.pad(s[:, 3 * d:], ((0, 0), (0, SMALL_PAD - SMALL)))


def _join_shift(proj_rows, d, heads):
    small_off = 6 * d
    main = _chan_unswap(proj_rows[:, :3 * d].reshape(-1, 3, d), heads).reshape(-1, 3 * d)
    return jnp.concatenate([main, proj_rows[:, small_off:small_off + SMALL]], axis=1)


def _rwkv_prompt(planes, bsz, heads, p, tc):
    tile = lambda x: jnp.tile(x.reshape(heads, HEAD_A).T, (1, bsz))[:, None, :]
    s0 = jnp.zeros((1, HEAD_A, HEAD_A, LANES), F32)
    y, st = _scan(planes, s0, tile(p["lnx_g"]), tile(p["lnx_b"]), tile(p["r_k"]), tc)
    wkv = st[0].reshape(HEAD_A, HEAD_A, bsz, heads).transpose(2, 3, 1, 0)
    return y, wkv


def _rwkv_sample(pre, nb, heads, wkv0, p):
    to_l = lambda x: x.reshape(nb, HEAD_A, heads).transpose(1, 2, 0).reshape(HEAD_A, 1, heads * nb)
    rep = lambda x: jnp.repeat(x.reshape(heads, HEAD_A).T, nb, axis=1)[:, None, :]
    s0 = wkv0.transpose(1, 3, 2, 0)
    y, st = _scan([to_l(x) for x in pre], s0, rep(p["lnx_g"]), rep(p["lnx_b"]), rep(p["r_k"]), 1)
    y = y.reshape(HEAD_A, heads, nb).transpose(2, 0, 1).reshape(nb, heads * HEAD_A)
    return y, st.transpose(3, 0, 2, 1)


def _layer(x_p, x_s, c_p, c_s, wkv_s, shift_s, p):
    bsz, t_len, d = x_p.shape
    nb = x_s.shape[0]
    heads = d // HEAD_A
    d_b = d // 2
    alpha = p["alpha"]
    mp = bsz * t_len

    assert bsz * heads == LANES and nb == LANES, "RWKV-7 recurrence fills 128 lanes per group"
    swap = functools.partial(_chan_swap, heads=heads)
    mu_main, mu_small = _split_shift(p["mu_shift"][None], d, heads)
    lp = {
        "mu_main": mu_main, "mu_small": mu_small,
        "w0": swap(p["w0"])[None], "w2": swap(p["w2"]), "a0": swap(p["a0"])[None],
        "a2": swap(p["a2"]), "g2": swap(p["g2"]),
        "k_k": swap(p["k_k"])[None], "k_a": swap(p["k_a"])[None],
        "sgu_ln_g": p["sgu_ln_g"][None], "sgu_ln_b": p["sgu_ln_b"][None], "sgu_w": p["sgu_w"],
        "sgu_b_full": jnp.broadcast_to(p["sgu_b"][:, :, None], p["sgu_b"].shape + (GROUP_B,)),
        "sgu_w_first": jnp.repeat(p["sgu_w"][:, 0, 0], GROUP_B)[None],
        "sgu_b_first": jnp.repeat(p["sgu_b"][:, 0], GROUP_B)[None],
    }
    ln1 = (p["ln1_g"][None], p["ln1_b"][None])
    ln2 = (p["ln2_g"][None], p["ln2_b"][None])

    c_all = jnp.concatenate([c_p, c_s], axis=0)
    mod = _modulation(jnp.pad(c_all, ((0, (-c_all.shape[0]) % SUBLANES), (0, 0))),
                      p["w_ada"], p["b_ada"])
    mod_p = mod[:bsz].reshape(bsz, 1, 6 * d)
    mod_s = mod[bsz:bsz + nb]

    w_pack = _pack_w_in(p["w_in"], d, d_b, heads)
    n_pack = w_pack.shape[1]
    tn_in = _pick(n_pack, (1280, 512))
    xp2 = x_p.reshape(mp, d)
    xs2 = x_s.reshape(nb, d)
    tm_big = _pick(t_len, (1024, 512, 256, 128))
    tm_mid = _pick(t_len, (512, 256, 128))
    tm_small = _pick(t_len, (256, 128))
    proj_p = _inproj(xp2, mod_p, w_pack, False, t_len, tm_big, tn_in)
    proj_s = _inproj(xs2, mod_s, w_pack, True, 1, nb, tn_in)

    zm = jnp.zeros((bsz, 1, 3 * d), F32)
    zs = jnp.zeros((bsz, 1, SMALL_PAD), F32)
    *planes_p, g_p = _rwkv_pre(proj_p.reshape(bsz, t_len, n_pack), zm, zs, lp, d, heads, False,
                               _pick(t_len, (32, 16, 8)))
    sm, ss = _split_shift(shift_s, d, heads)
    *pre_s, g_s = _rwkv_pre(proj_s[None], sm[None], ss[None], lp, d, heads, True, nb)
    y_planes, wkv_p = _rwkv_prompt(planes_p, bsz, heads, p, _pick(t_len, (64, 32, 16, 8)))
    ya_p = _from_planes(y_planes, bsz, heads, _pick(t_len, (128, 64, 32, 16, 8))).reshape(mp, d)
    g_p = g_p.reshape(mp, d)
    ya_s, wkv_s_new = _rwkv_sample([x[0] for x in pre_s], nb, heads, wkv_s, p)
    g_s = g_s[0]

    u_blk = (5 * d) // d_b
    yb_p = _sgu_prompt(proj_p, lp, d_b, u_blk, tm_mid)
    yb_s, vrows_s = _sgu_single(proj_s, lp, d_b, u_blk)

    wa = _chan_swap(p["w_br_a"], heads, axis=0).astype(BF16)
    wb = p["w_br_b"].astype(BF16)
    wo = p["w_o"].astype(BF16)
    wq = p["peer_wq"].astype(BF16)
    tn_mrg = _pick(d, (512, 256, 128))
    mrg_p = _merge(ya_p, g_p, yb_p, proj_p, wa, wb, d, tm_mid, tn_mrg)
    mrg_s = _merge(ya_s, g_s, yb_s, proj_s, wa, wb, d, nb, tn_mrg)
    x1_p, h2_p, q_p = _out_ln(mrg_p, xp2, mod_p, wo, wq, *ln1, False, t_len, alpha, tm_small)
    x1_s, h2_s, q_s = _out_ln(mrg_s, xs2, mod_s, wo, wq, *ln1, True, 1, alpha, nb)

    m_all = mp + nb
    tt = _dense_token_tile(m_all)
    pad_rows = (-m_all) % tt
    h2 = jnp.concatenate([h2_p, h2_s, jnp.zeros((pad_rows, d), BF16)], axis=0)
    q = jnp.concatenate([q_p, q_s, jnp.zeros((pad_rows, q_p.shape[1]), BF16)], axis=0)
    n1, p1, r2, p2 = _peer_topk(q, p["peer_keys"], LANES)
    ft = _peer_dense(h2, p["peer_u"].astype(BF16), p["peer_v"].T.astype(BF16), n1, p1, r2, p2,
                     tt, 8 * N_KEYS)

    y_p = _final_ln(x1_p, ft, 0, mod_p, *ln2, False, t_len, alpha, tm_small)
    y_s = _final_ln(x1_s, ft, mp, mod_s, *ln2, True, 1, alpha, nb)

    last = proj_p.reshape(bsz, t_len, -1)[:, -1]
    return (y_p.reshape(bsz, t_len, d), y_s.reshape(nb, 1, d), wkv_p, _join_shift(last, d, heads),
            wkv_s_new, _join_shift(proj_s, d, heads), vrows_s.reshape(nb, 1, d_b))


def kernel(x_prompt, x_sample, c_prompt, c_sample, state_wkv, state_shift, w_ada, b_ada, w_in, mu_shift, w0, w2, a0, a2, g2, k_k, k_a, r_k, lnx_g, lnx_b, sgu_ln_g, sgu_ln_b, sgu_w, sgu_b, w_br_a, w_br_b, w_o, ln1_g, ln1_b, peer_wq, peer_keys, peer_u, peer_v, ln2_g, ln2_b):
    names = ("w_ada", "b_ada", "w_in", "mu_shift", "w0", "w2", "a0", "a2", "g2", "k_k", "k_a",
             "r_k", "lnx_g", "lnx_b", "sgu_ln_g", "sgu_ln_b", "sgu_w", "sgu_b", "w_br_a",
             "w_br_b", "w_o", "ln1_g", "ln1_b", "peer_wq", "peer_keys", "peer_u", "peer_v",
             "ln2_g", "ln2_b")
    stacked = (w_ada, b_ada, w_in, mu_shift, w0, w2, a0, a2, g2, k_k, k_a, r_k, lnx_g, lnx_b,
               sgu_ln_g, sgu_ln_b, sgu_w, sgu_b, w_br_a, w_br_b, w_o, ln1_g, ln1_b, peer_wq,
               peer_keys, peer_u, peer_v, ln2_g, ln2_b)
    depth = w_ada.shape[0]
    alpha = (2 * depth) ** 0.25
    y_p, y_s = x_prompt, x_sample
    outs = [[] for _ in range(5)]
    for l in range(depth):
        p = {n: a[l] for n, a in zip(names, stacked)}
        p["alpha"] = alpha
        y_p, y_s, *state = _layer(y_p, y_s, c_prompt, c_sample, state_wkv[l], state_shift[l], p)
        for acc, s in zip(outs, state):
            acc.append(s)
    return (y_p, y_s) + tuple(jnp.stack(o) for o in outs)
```

```python
import functools

import jax
import jax.numpy as jnp
from jax import lax
from jax.experimental import pallas as pl
from jax.experimental.pallas import tpu as pltpu

F32 = jnp.float32
BF16 = jnp.bfloat16

HEAD_A = 64
R_DECAY = 96
R_AAA = 96
R_GATE = 256
SMALL = R_DECAY + R_AAA + R_GATE
SMALL_PAD = 512
CHUNK = 128
GROUP_B = 128
PEER_HEADS = 8
N_KEYS = 128
DK_HALF = 128
TOPK = 16
LN_EPS = 1e-5
GN_EPS = 64e-5

LANES = 128
SUBLANES = 8
VMEM_LIMIT = 56 * 1024 * 1024


def _cparams(*sem):
    return pltpu.CompilerParams(dimension_semantics=sem, vmem_limit_bytes=VMEM_LIMIT)


def _gelu(x):
    return jax.nn.gelu(x, approximate=True)


def _layer_norm_rows(x, g, b, eps):
    mu = jnp.mean(x, axis=-1, keepdims=True)
    xc = x - mu
    var = jnp.mean(xc * xc, axis=-1, keepdims=True)
    return xc * lax.rsqrt(var + eps) * g + b


def _pick(n, prefs):
    for p in prefs:
        if n % p == 0:
            return p
    return n


def _mod_spec(per_row, tm, d, chunk, rows_per_seq):
    if per_row:
        return pl.BlockSpec((tm, d), lambda i, *_: (i, chunk))
    return pl.BlockSpec((None, 1, d), lambda i, *_: ((i * tm) // rows_per_seq, 0, chunk))


def _mod_kernel(c_ref, w_ref, b_ref, o_ref):
    c = c_ref[...]
    s = (c * jax.nn.sigmoid(c)).astype(BF16)
    o_ref[...] = jnp.dot(s, w_ref[...].astype(BF16), preferred_element_type=F32) + b_ref[...]


def _modulation(c_all, w_ada, b_ada):
    m, d = c_all.shape
    n = w_ada.shape[1]
    tn = _pick(n, (1536, 512))
    return pl.pallas_call(
        _mod_kernel,
        grid=(n // tn,),
        in_specs=[pl.BlockSpec((m, d), lambda j: (0, 0)),
                  pl.BlockSpec((d, tn), lambda j: (0, j)),
                  pl.BlockSpec((1, tn), lambda j: (0, j))],
        out_specs=pl.BlockSpec((m, tn), lambda j: (0, j)),
        out_shape=jax.ShapeDtypeStruct((m, n), F32),
        compiler_params=_cparams("arbitrary"),
        name="adaln_mod",
    )(c_all, w_ada, b_ada.reshape(1, n))


def _inproj_kernel(x_ref, sh_ref, sc_ref, w_ref, o_ref, h_scr):
    @pl.when(pl.program_id(1) == 0)
    def _():
        h_scr[...] = (x_ref[...] * (1.0 + sc_ref[...]) + sh_ref[...]).astype(BF16)

    o_ref[...] = jnp.dot(h_scr[...], w_ref[...], preferred_element_type=F32)


def _inproj(x, mod, w_pack, per_row, seq_len, tm, tn):
    m, d = x.shape
    n = w_pack.shape[1]
    return pl.pallas_call(
        _inproj_kernel,
        grid=(m // tm, n // tn),
        in_specs=[pl.BlockSpec((tm, d), lambda i, j: (i, 0)),
                  _mod_spec(per_row, tm, d, 0, seq_len),
                  _mod_spec(per_row, tm, d, 1, seq_len),
                  pl.BlockSpec((d, tn), lambda i, j: (0, j))],
        out_specs=pl.BlockSpec((tm, tn), lambda i, j: (i, j)),
        out_shape=jax.ShapeDtypeStruct((m, n), F32),
        scratch_shapes=[pltpu.VMEM((tm, d), BF16)],
        compiler_params=_cparams("arbitrary", "arbitrary"),
        name="in_proj",
    )(x, mod, mod, w_pack)


def _swap_lane_groups(xs, width):
    n = len(xs)
    grp = lax.broadcasted_iota(jnp.int32, xs[0].shape, 1) // width
    moved = []
    for s in range(n):
        acc = xs[0]
        for b in range(1, n):
            acc = jnp.where(grp == (b - s) % n, xs[b], acc)
        moved.append(acc if s == 0 else pltpu.roll(acc, s * width, 1))
    outs = []
    for q in range(n):
        acc = moved[0]
        for s in range(1, n):
            acc = jnp.where(grp == (q + s) % n, moved[s], acc)
        outs.append(acc)
    return outs


def _to_planes(xs, o_ref, heads):
    per = LANES // heads
    for j in range(xs[0].shape[1] // LANES):
        outs = _swap_lane_groups([x[:, j * LANES:(j + 1) * LANES] for x in xs], heads)
        for q in range(per):
            o_ref[j * per + q] = outs[q]


def _from_planes_kernel(y_ref, o_ref, *, heads):
    per = LANES // heads
    for j in range(y_ref.shape[0] // per):
        outs = _swap_lane_groups([y_ref[j * per + q] for q in range(per)], heads)
        for b in range(per):
            o_ref[b, :, j * LANES:(j + 1) * LANES] = outs[b]


def _from_planes(y, nseq, heads, tm):
    n, t_len, _ = y.shape
    d = n * heads
    return pl.pallas_call(
        functools.partial(_from_planes_kernel, heads=heads),
        grid=(t_len // tm,),
        in_specs=[pl.BlockSpec((n, tm, LANES), lambda i: (0, i, 0))],
        out_specs=pl.BlockSpec((nseq, tm, d), lambda i: (0, i, 0)),
        out_shape=jax.ShapeDtypeStruct((nseq, t_len, d), F32),
        compiler_params=_cparams("arbitrary"),
        name="rwkv_from_planes",
    )(y)


def _rwkv_pre_kernel(pm_ref, ps_ref, qm_ref, qs_ref, stm_ref, sts_ref, mum_ref, mus_ref,
                     w0_ref, w2_ref, a0_ref, a2_ref, g2_ref,
                     r_o, w_o, k_o, v_o, a_o, g_o, *, d, heads, per_row_state):
    nseq, tm, _ = pm_ref.shape

    def shifted(p, q, st):
        if per_row_state:
            return st
        first = pl.program_id(0) == 0
        prev_row = jnp.where(first, st, q[SUBLANES - 1:SUBLANES, :])
        row = lax.broadcasted_iota(jnp.int32, p.shape, 0)
        return jnp.where(row == 0, prev_row, pltpu.roll(p, 1, 0))

    def mix(p, q, st, mu):
        return p + (shifted(p, q, st) - p) * mu

    seqs = range(nseq)
    xs_s = jnp.concatenate([mix(ps_ref[b], qs_ref[b], sts_ref[b], mus_ref[...]) for b in seqs],
                           axis=0)
    wl = xs_s[:, 0:R_DECAY]
    al = xs_s[:, R_DECAY:R_DECAY + R_AAA]
    gl = xs_s[:, R_DECAY + R_AAA:SMALL]
    wlin = w0_ref[...] + jnp.dot(jnp.tanh(wl).astype(BF16), w2_ref[...].astype(BF16),
                                 preferred_element_type=F32)
    w = -jax.nn.softplus(-wlin) - 0.5
    decay = jnp.exp(-jnp.exp(w))
    a = jax.nn.sigmoid(a0_ref[...] + jnp.dot(al.astype(BF16), a2_ref[...].astype(BF16),
                                             preferred_element_type=F32))
    g = jnp.dot(jax.nn.sigmoid(gl).astype(BF16), g2_ref[...].astype(BF16),
                preferred_element_type=F32)

    def of_seq(x, b):
        return x[b * tm:(b + 1) * tm]

    def part(n, b):
        cols = slice(n * d, (n + 1) * d)
        return mix(pm_ref[b, :, cols], qm_ref[b, :, cols], stm_ref[b, :, cols], mum_ref[:, cols])

    def emit(o_ref, xs):
        if per_row_state:
            for b in seqs:
                o_ref[b] = xs[b]
        else:
            _to_planes(xs, o_ref, heads)

    for b in seqs:
        g_o[b] = of_seq(g, b)
    emit(r_o, [part(0, b) for b in seqs])
    emit(w_o, [of_seq(decay, b) for b in seqs])
    emit(k_o, [part(1, b) for b in seqs])
    emit(v_o, [part(2, b) for b in seqs])
    emit(a_o, [of_seq(a, b) for b in seqs])


def _rwkv_pre(proj, state_main, state_small, lp, d, heads, per_row_state, tm):
    nseq, t_len, _ = proj.shape
    small_blk = (6 * d) // SMALL_PAD

    def prev_map(col):
        def f(i):
            return (0, jnp.maximum(i * (tm // SUBLANES) - 1, 0), col)
        return f

    if per_row_state:
        stm_spec = pl.BlockSpec((nseq, tm, 3 * d), lambda i: (0, i, 0))
        sts_spec = pl.BlockSpec((nseq, tm, SMALL_PAD), lambda i: (0, i, 0))
    else:
        stm_spec = pl.BlockSpec((nseq, 1, 3 * d), lambda i: (0, 0, 0))
        sts_spec = pl.BlockSpec((nseq, 1, SMALL_PAD), lambda i: (0, 0, 0))

    vec = lambda w: pl.BlockSpec((1, w), lambda i: (0, 0))
    full = lambda a: pl.BlockSpec(a.shape, lambda i: (0,) * a.ndim)
    rows = pl.BlockSpec((nseq, tm, d), lambda i: (0, i, 0))
    rows_shape = jax.ShapeDtypeStruct((nseq, t_len, d), F32)
    if per_row_state:
        vec_out, vec_shape = rows, rows_shape
    else:
        vec_out = pl.BlockSpec((HEAD_A, tm, LANES), lambda i: (0, i, 0))
        vec_shape = jax.ShapeDtypeStruct((HEAD_A, t_len, LANES), F32)
    kern = functools.partial(_rwkv_pre_kernel, d=d, heads=heads, per_row_state=per_row_state)
    in_specs = [pl.BlockSpec((nseq, tm, 3 * d), lambda i: (0, i, 0)),
                pl.BlockSpec((nseq, tm, SMALL_PAD), lambda i: (0, i, small_blk)),
                pl.BlockSpec((nseq, SUBLANES, 3 * d), prev_map(0)),
                pl.BlockSpec((nseq, SUBLANES, SMALL_PAD), prev_map(small_blk)),
                stm_spec, sts_spec, vec(3 * d), vec(SMALL_PAD), vec(d), full(lp["w2"]), vec(d),
                full(lp["a2"]), full(lp["g2"])]
    return pl.pallas_call(
        kern,
        grid=(t_len // tm,),
        in_specs=in_specs,
        out_specs=[vec_out] * 5 + [rows],
        out_shape=[vec_shape] * 5 + [rows_shape],
        compiler_params=_cparams("arbitrary"),
        name="rwkv_pre",
    )(proj, proj, proj, proj, state_main, state_small, lp["mu_main"], lp["mu_small"], lp["w0"],
      lp["w2"], lp["a0"], lp["a2"], lp["g2"])


def _scan_kernel(r_ref, w_ref, kraw_ref, v_ref, a_ref, s0_ref, lng_ref, lnb_ref, rk_ref, kkp_ref,
                 kap_ref, y_ref, st_ref, s_scr, aa_scr, bb_scr, k_ref, bon_scr, yt_scr, *, tc):
    n = HEAD_A
    c = pl.program_id(1)

    @pl.when(c == 0)
    def _():
        s_scr[...] = s0_ref[...]

    plane0 = jnp.zeros((tc, LANES), F32)

    def norm_acc(i, acc):
        kk = kraw_ref[i] * kkp_ref[i]
        return acc + kk * kk

    nrm2 = lax.fori_loop(0, n, norm_acc, plane0, unroll=SUBLANES)
    inv = 1.0 / jnp.maximum(jnp.sqrt(nrm2), 1e-12)

    def fill(i, bon):
        kraw = kraw_ref[i]
        a = a_ref[i]
        kkn = kraw * kkp_ref[i] * inv
        aa_scr[i] = -kkn
        bb_scr[i] = kkn * a
        kmod = kraw * (1.0 + (a - 1.0) * kap_ref[i])
        k_ref[i] = kmod
        return bon + r_ref[i] * kmod * rk_ref[i]

    bon_scr[...] = lax.fori_loop(0, n, fill, plane0, unroll=SUBLANES)

    zeros = jnp.zeros((n, LANES), F32)
    nblk = n // SUBLANES
    sub = lax.broadcasted_iota(jnp.int32, (SUBLANES, LANES), 0)

    def row(ref, i, t):
        return ref[i, pl.ds(t, 1), :]

    def tile_of_step(ref, t):
        blocks = []
        for m in range(nblk):
            acc = jnp.broadcast_to(row(ref, m * SUBLANES, t), (SUBLANES, LANES))
            for s in range(1, SUBLANES):
                acc = jnp.where(sub == s, row(ref, m * SUBLANES + s, t), acc)
            blocks.append(acc)
        return jnp.concatenate(blocks, axis=0)

    def sa_first(kb, acc):
        for j in range(SUBLANES):
            i = kb * SUBLANES + j
            acc = acc + s_scr[i] * row(aa_scr, i, 0)
        return acc

    sa0 = lax.fori_loop(0, nblk, sa_first, zeros)

    def step(t, sa):
        tn = jnp.minimum(t + 1, tc - 1)
        vt = tile_of_step(v_ref, t)

        def kblock(kb, carry):
            y, san = carry
            for j in range(SUBLANES):
                i = kb * SUBLANES + j
                sn = s_scr[i] * row(w_ref, i, t) + sa * row(bb_scr, i, t) + vt * row(k_ref, i, t)
                s_scr[i] = sn
                y = y + sn * row(r_ref, i, t)
                san = san + sn * row(aa_scr, i, tn)
            return y, san

        y, san = lax.fori_loop(0, nblk, kblock, (zeros, zeros), unroll=True)
        return san, y

    grp = min(SUBLANES, tc)

    def group(gb, sa):
        t0 = pl.multiple_of(gb * grp, grp)
        rows = pl.ds(t0, grp)

        def one(s, sa):
            sa, y = step(t0 + s, sa)
            yt_scr[s] = y
            return sa

        sa = lax.fori_loop(0, grp, one, sa)
        total = jnp.zeros((grp, LANES), F32)
        for i in range(n):
            acc = jnp.broadcast_to(yt_scr[0, i:i + 1, :], (grp, LANES))
            for s in range(1, grp):
                acc = jnp.where(sub[:grp] == s, yt_scr[s, i:i + 1, :], acc)
            y_ref[i, rows, :] = acc
            total = total + acc
        mu = total * (1.0 / n)
        sq = jnp.zeros((grp, LANES), F32)
        for i in range(n):
            dlt = y_ref[i, rows, :] - mu
            sq = sq + dlt * dlt
        rstd = lax.rsqrt(sq * (1.0 / n) + GN_EPS)
        bon = bon_scr[rows, :]
        for i in range(n):
            yn = (y_ref[i, rows, :] - mu) * rstd * lng_ref[i] + lnb_ref[i]
            y_ref[i, rows, :] = yn + bon * v_ref[i, rows, :]
        return sa

    lax.fori_loop(0, tc // grp, group, sa0)

    @pl.when(c == pl.num_programs(1) - 1)
    def _():
        st_ref[...] = s_scr[...]


def _scan(seq, s0, pars, tc):
    n, t_len, gl = seq[0].shape
    g = gl // LANES
    step_spec = pl.BlockSpec((n, tc, LANES), lambda gi, c: (0, c, gi))
    par_spec = pl.BlockSpec((n, 1, LANES), lambda gi, c: (0, 0, gi))
    st_spec = pl.BlockSpec((None, n, n, LANES), lambda gi, c: (gi, 0, 0, 0))
    return pl.pallas_call(
        functools.partial(_scan_kernel, tc=tc),
        grid=(g, t_len // tc),
        in_specs=[step_spec] * 5 + [st_spec] + [par_spec] * 5,
        out_specs=[step_spec, st_spec],
        out_shape=[jax.ShapeDtypeStruct((n, t_len, gl), F32),
                   jax.ShapeDtypeStruct((g, n, n, LANES), F32)],
        scratch_shapes=[pltpu.VMEM((n, n, LANES), F32),
                        pltpu.VMEM((n, tc, LANES), F32),
                        pltpu.VMEM((n, tc, LANES), F32),
                        pltpu.VMEM((n, tc, LANES), F32),
                        pltpu.VMEM((tc, LANES), F32),
                        pltpu.VMEM((min(SUBLANES, tc), n, LANES), F32)],
        compiler_params=_cparams("arbitrary", "arbitrary"),
        name="rwkv_scan",
    )(*seq, s0, *pars)


def _sgu_kernel(pu_ref, pv_ref, lg_ref, lb_ref, w_ref, b_ref, y_ref, *, chunks):
    ng = w_ref.shape[0]
    row = lax.broadcasted_iota(jnp.int32, (CHUNK, CHUNK), 0)
    col = lax.broadcasted_iota(jnp.int32, (CHUNK, CHUNK), 1)
    causal = col <= row
    for ci in range(chunks):
        rows = slice(ci * CHUNK, (ci + 1) * CHUNK)
        u = _gelu(pu_ref[rows, :])
        v = _layer_norm_rows(_gelu(pv_ref[rows, :]), lg_ref[...], lb_ref[...], LN_EPS)
        for g in range(ng):
            cols = slice(g * GROUP_B, (g + 1) * GROUP_B)
            ws = jnp.where(causal, w_ref[g], 0.0).astype(BF16)
            mixed = jnp.dot(ws, v[:, cols].astype(BF16), preferred_element_type=F32) + b_ref[g]
            y_ref[rows, cols] = (u[:, cols] * mixed).astype(y_ref.dtype)


def _sgu_prompt(proj, lp, d_b, u_blk, tm):
    m = proj.shape[0]
    ng = d_b // GROUP_B
    return pl.pallas_call(
        functools.partial(_sgu_kernel, chunks=tm // CHUNK),
        grid=(m // tm,),
        in_specs=[pl.BlockSpec((tm, d_b), lambda i: (i, u_blk)),
                  pl.BlockSpec((tm, d_b), lambda i: (i, u_blk + 1)),
                  pl.BlockSpec((1, d_b), lambda i: (0, 0)),
                  pl.BlockSpec((1, d_b), lambda i: (0, 0)),
                  pl.BlockSpec((ng, CHUNK, CHUNK), lambda i: (0, 0, 0)),
                  pl.BlockSpec((ng, CHUNK, GROUP_B), lambda i: (0, 0, 0))],
        out_specs=pl.BlockSpec((tm, d_b), lambda i: (i, 0)),
        out_shape=jax.ShapeDtypeStruct((m, d_b), BF16),
        compiler_params=_cparams("arbitrary"),
        name="sgu_chunked",
    )(proj, proj, lp["sgu_ln_g"], lp["sgu_ln_b"], lp["sgu_w"], lp["sgu_b_full"])


def _sgu_first_kernel(pu_ref, pv_ref, lg_ref, lb_ref, wd_ref, bd_ref, y_ref, v_ref):
    u = _gelu(pu_ref[...])
    v = _layer_norm_rows(_gelu(pv_ref[...]), lg_ref[...], lb_ref[...], LN_EPS)
    v_ref[...] = v
    vb = v.astype(BF16).astype(F32)
    wd = wd_ref[...].astype(BF16).astype(F32)
    y_ref[...] = (u * (vb * wd + bd_ref[...])).astype(y_ref.dtype)


def _sgu_single(proj, lp, d_b, u_blk):
    m = proj.shape[0]
    vec = pl.BlockSpec((1, d_b), lambda i: (0, 0))
    return pl.pallas_call(
        _sgu_first_kernel,
        grid=(1,),
        in_specs=[pl.BlockSpec((m, d_b), lambda i: (0, u_blk)),
                  pl.BlockSpec((m, d_b), lambda i: (0, u_blk + 1)), vec, vec, vec, vec],
        out_specs=[pl.BlockSpec((m, d_b), lambda i: (0, 0))] * 2,
        out_shape=[jax.ShapeDtypeStruct((m, d_b), BF16), jax.ShapeDtypeStruct((m, d_b), F32)],
        compiler_params=_cparams("arbitrary"),
        name="sgu_single",
    )(proj, proj, lp["sgu_ln_g"], lp["sgu_ln_b"], lp["sgu_w_first"], lp["sgu_b_first"])


def _merge_kernel(ya_ref, g_ref, yb_ref, ga_ref, gb_ref, wa_ref, wb_ref, o_ref, a_scr):
    @pl.when(pl.program_id(1) == 0)
    def _():
        a_scr[...] = (ya_ref[...] * g_ref[...]).astype(BF16)

    ta = jnp.dot(a_scr[...], wa_ref[...], preferred_element_type=F32)
    tb = jnp.dot(yb_ref[...], wb_ref[...], preferred_element_type=F32)
    o_ref[...] = (jax.nn.sigmoid(ga_ref[...]) * ta
                  + jax.nn.sigmoid(gb_ref[...]) * tb).astype(o_ref.dtype)


def _merge(ya, g, yb, proj, wa, wb, d, tm, tn):
    m = ya.shape[0]
    d_b = yb.shape[1]
    ga_blk = (3 * d) // tn
    gb_blk = (4 * d) // tn
    return pl.pallas_call(
        _merge_kernel,
        grid=(m // tm, d // tn),
        in_specs=[pl.BlockSpec((tm, d), lambda i, j: (i, 0)),
                  pl.BlockSpec((tm, d), lambda i, j: (i, 0)),
                  pl.BlockSpec((tm, d_b), lambda i, j: (i, 0)),
                  pl.BlockSpec((tm, tn), lambda i, j: (i, ga_blk + j)),
                  pl.BlockSpec((tm, tn), lambda i, j: (i, gb_blk + j)),
                  pl.BlockSpec((d, tn), lambda i, j: (0, j)),
                  pl.BlockSpec((d_b, tn), lambda i, j: (0, j))],
        out_specs=pl.BlockSpec((tm, tn), lambda i, j: (i, j)),
        out_shape=jax.ShapeDtypeStruct((m, d), BF16),
        scratch_shapes=[pltpu.VMEM((tm, d), BF16)],
        compiler_params=_cparams("arbitrary", "arbitrary"),
        name="branch_merge",
    )(ya, g, yb, proj, proj, wa, wb)


def _out_ln_kernel(m_ref, x_ref, gt_ref, sh_ref, sc_ref, wo_ref, wq_ref, g_ref, b_ref,
                   x1_ref, h2_ref, q_ref, *, alpha):
    z = jnp.dot(m_ref[...], wo_ref[...], preferred_element_type=F32)
    x1 = _layer_norm_rows(alpha * x_ref[...] + gt_ref[...] * z, g_ref[...], b_ref[...], LN_EPS)
    x1_ref[...] = x1
    h2 = (x1 * (1.0 + sc_ref[...]) + sh_ref[...]).astype(BF16)
    h2_ref[...] = h2
    q_ref[...] = jnp.dot(h2, wq_ref[...], preferred_element_type=F32).astype(q_ref.dtype)


def _out_ln(mrg, x, mod, wo, wq, ln_g, ln_b, per_row, seq_len, alpha, tm):
    m, d = x.shape
    nq = wq.shape[1]
    vec = pl.BlockSpec((1, d), lambda i: (0, 0))
    row = pl.BlockSpec((tm, d), lambda i: (i, 0))
    return pl.pallas_call(
        functools.partial(_out_ln_kernel, alpha=alpha),
        grid=(m // tm,),
        in_specs=[row, row,
                  _mod_spec(per_row, tm, d, 2, seq_len),
                  _mod_spec(per_row, tm, d, 3, seq_len),
                  _mod_spec(per_row, tm, d, 4, seq_len),
                  pl.BlockSpec((d, d), lambda i: (0, 0)),
                  pl.BlockSpec((d, nq), lambda i: (0, 0)), vec, vec],
        out_specs=[row, row, pl.BlockSpec((tm, nq), lambda i: (i, 0))],
        out_shape=[jax.ShapeDtypeStruct((m, d), F32), jax.ShapeDtypeStruct((m, d), BF16),
                   jax.ShapeDtypeStruct((m, nq), BF16)],
        compiler_params=_cparams("arbitrary"),
        name="out_proj_ln1",
    )(mrg, x, mod, mod, mod, wo, wq, ln_g, ln_b)


_PAIRS = [(i, j) for i in range(TOPK) for j in range(TOPK) if (i + 1) * (j + 1) <= TOPK]
_NPAIR_PAD = -(-len(_PAIRS) // SUBLANES) * SUBLANES


def _take_top(work, rounds, break_ties):
    rows = work.shape[0]
    iota = lax.broadcasted_iota(jnp.int32, work.shape, 0).astype(F32)
    rank = jnp.full(work.shape, float(rounds), F32)
    vals = []
    for r in range(rounds):
        mx = jnp.max(work, axis=0, keepdims=True)
        hit = work == mx
        if break_ties:
            first = jnp.min(jnp.where(hit, iota, float(rows)), axis=0, keepdims=True)
            hit = iota == first
        rank = jnp.where(hit, float(r), rank)
        work = jnp.where(hit, -jnp.inf, work)
        vals.append(mx)
    return vals, rank


def _exactly_k(rank, k):
    n = jnp.sum(jnp.where(rank < float(k), 1.0, 0.0), axis=0, keepdims=True)
    return jnp.where(n == float(k), 1.0, 0.0)


def _head_select(s1, s2, cnt_mat, cand_scr, break_ties):
    a1, rank1 = _take_top(s1, TOPK, break_ties)
    a2, rank2 = _take_top(s2, TOPK, break_ties)
    for pos, (i, j) in enumerate(_PAIRS):
        cand_scr[pos:pos + 1, :] = a1[i] + a2[j]
    cand = cand_scr[...]
    _, rank_c = _take_top(cand, TOPK, break_ties)
    sel = jnp.where(rank_c < float(TOPK), 1.0, 0.0)
    top = a1[0] + a2[0]
    z = jnp.sum(sel * jnp.exp(cand - top), axis=0, keepdims=True)
    cnt = jnp.dot(cnt_mat, sel.astype(BF16), preferred_element_type=F32)
    n1 = jnp.zeros(s1.shape, F32)
    for i in range(TOPK):
        n1 = jnp.where(rank1 == float(i), cnt[i:i + 1, :], n1)
    p1 = jnp.where(rank1 < float(TOPK), jnp.exp(s1 - a1[0]), 0.0)
    p2 = jnp.where(rank2 < float(TOPK), jnp.exp(s2 - a2[0]), 0.0) / z
    clean = _exactly_k(rank1, TOPK) * _exactly_k(rank2, TOPK) * _exactly_k(rank_c, TOPK)
    return n1, p1, rank2, p2, clean


def _peer_topk_kernel(q_ref, keys_ref, cnt_ref, n1_ref, p1_ref, r2_ref, p2_ref, cand_a, cand_b):
    nt = (((1,), (1,)), ((), ()))
    cand_a[...] = jnp.full(cand_a.shape, -jnp.inf, F32)
    cand_b[...] = jnp.full(cand_b.shape, -jnp.inf, F32)

    def scores(h):
        base = pl.multiple_of(h * 2 * DK_HALF, 2 * DK_HALF)
        q1 = q_ref[:, pl.ds(base, DK_HALF)]
        q2 = q_ref[:, pl.ds(base + DK_HALF, DK_HALF)]
        s1 = lax.dot_general(keys_ref[h, 0].astype(BF16), q1, nt, preferred_element_type=F32)
        s2 = lax.dot_general(keys_ref[h, 1].astype(BF16), q2, nt, preferred_element_type=F32)
        return s1, s2

    def write(h, res):
        n1_ref[h], p1_ref[h] = res[:2]
        r2_ref[h] = res[2].astype(r2_ref.dtype)
        p2_ref[h] = res[3].astype(p2_ref.dtype)

    def head_pair(hp, carry):
        heads = (2 * hp, 2 * hp + 1)
        scr = (cand_a, cand_b)
        sc = [scores(h) for h in heads]
        res = [_head_select(*sc[n], cnt_ref[...], scr[n], False) for n in range(2)]
        for n in range(2):
            write(heads[n], res[n])
        for n in range(2):
            @pl.when(jnp.min(res[n][4]) < 0.5)
            def _():
                write(heads[n], _head_select(*sc[n], cnt_ref[...], scr[n], True))
        return carry

    lax.fori_loop(0, PEER_HEADS // 2, head_pair, 0)


def _pair_count_matrix():
    return jnp.array([[1.0 if i == r else 0.0 for (i, _) in _PAIRS]
                      + [0.0] * (_NPAIR_PAD - len(_PAIRS)) for r in range(TOPK)], BF16)


def _peer_topk(q, keys, tt):
    m = q.shape[0]
    cnt = _pair_count_matrix()
    out = pl.BlockSpec((PEER_HEADS, N_KEYS, tt), lambda i: (0, 0, i))
    return pl.pallas_call(
        _peer_topk_kernel,
        grid=(m // tt,),
        in_specs=[pl.BlockSpec((tt, q.shape[1]), lambda i: (i, 0)),
                  pl.BlockSpec(keys.shape, lambda i: (0, 0, 0, 0)),
                  pl.BlockSpec(cnt.shape, lambda i: (0, 0))],
        out_specs=[out] * 4,
        out_shape=[jax.ShapeDtypeStruct((PEER_HEADS, N_KEYS, m), dt)
                   for dt in (F32, F32, BF16, BF16)],
        scratch_shapes=[pltpu.VMEM((_NPAIR_PAD, tt), F32)] * 2,
        compiler_params=_cparams("arbitrary"),
        name="peer_topk",
    )(q, keys, cnt)


def _peer_dense_kernel(h_ref, u_ref, vt_ref, n1_ref, p1_ref, r2_ref, p2_ref, o_ref, w_scr, g_scr,
                       *, et, n_et):
    n = pl.program_id(0)
    cur = jnp.minimum(n, pl.num_programs(0) - 2)
    e_cur = cur % n_et
    e_prev = jnp.maximum(n - 1, 0) % n_et
    nt = (((1,), (1,)), ((), ()))

    @pl.when(n == 0)
    def _():
        w_scr[1] = jnp.zeros(w_scr.shape[1:], BF16)

    @pl.when(e_prev == 0)
    def _():
        o_ref[...] = jnp.zeros(o_ref.shape, F32)

    zero = jnp.zeros((), BF16)
    for j in range(et // N_KEYS):
        e1 = e_cur * (et // N_KEYS) + j
        gate = jnp.zeros((N_KEYS, g_scr.shape[1]), BF16)
        for h in range(PEER_HEADS):
            n1 = n1_ref[h, pl.ds(e1, 1), :].astype(BF16)
            p1 = p1_ref[h, pl.ds(e1, 1), :].astype(BF16)
            gate = gate + jnp.where(r2_ref[h] < n1, p2_ref[h], zero) * p1
        g_scr[j * N_KEYS:(j + 1) * N_KEYS, :] = gate
    s = lax.dot_general(u_ref[...], h_ref[...], nt, preferred_element_type=F32)
    o_ref[...] += jnp.dot(vt_ref[...], w_scr[(n + 1) % 2], preferred_element_type=F32)
    w_scr[n % 2] = _gelu(s).astype(BF16) * g_scr[...]


def _peer_dense(h2, u_tab, vt_tab, n1, p1, r2, p2, tt, et):
    m, d = h2.shape
    n_et = u_tab.shape[0] // et
    steps = (m // tt) * n_et

    def cur(fn):
        return lambda n: fn(jnp.minimum(n, steps - 1))

    def prev(fn):
        return lambda n: fn(jnp.maximum(n - 1, 0))

    once = pl.Buffered(1)
    sel = pl.BlockSpec((PEER_HEADS, N_KEYS, tt), cur(lambda c: (0, 0, c // n_et)),
                       pipeline_mode=once)
    return pl.pallas_call(
        functools.partial(_peer_dense_kernel, et=et, n_et=n_et),
        grid=(steps + 1,),
        in_specs=[pl.BlockSpec((tt, d), cur(lambda c: (c // n_et, 0))),
                  pl.BlockSpec((et, d), cur(lambda c: (c % n_et, 0))),
                  pl.BlockSpec((d, et), prev(lambda c: (0, c % n_et))),
                  sel, sel, sel, sel],
        out_specs=pl.BlockSpec((d, tt), prev(lambda c: (0, c // n_et))),
        out_shape=jax.ShapeDtypeStruct((d, m), F32),
        scratch_shapes=[pltpu.VMEM((2, et, tt), BF16), pltpu.VMEM((et, tt), BF16)],
        compiler_params=_cparams("arbitrary"),
        name="peer_dense",
    )(h2, u_tab, vt_tab, n1, p1, r2, p2)


def _final_kernel(x1_ref, ft_ref, gt_ref, g_ref, b_ref, o_ref, *, alpha):
    f = ft_ref[...].T
    o_ref[...] = _layer_norm_rows(alpha * x1_ref[...] + gt_ref[...] * f,
                                  g_ref[...], b_ref[...], LN_EPS)


def _final_ln(x1, ft, first_token, mod, ln_g, ln_b, per_row, seq_len, alpha, tm):
    m, d = x1.shape
    col0 = first_token // tm
    row = pl.BlockSpec((tm, d), lambda i: (i, 0))
    vec = pl.BlockSpec((1, d), lambda i: (0, 0))
    return pl.pallas_call(
        functools.partial(_final_kernel, alpha=alpha),
        grid=(m // tm,),
        in_specs=[row, pl.BlockSpec((d, tm), lambda i: (0, col0 + i)),
                  _mod_spec(per_row, tm, d, 5, seq_len), vec, vec],
        out_specs=row,
        out_shape=jax.ShapeDtypeStruct((m, d), F32),
        compiler_params=_cparams("arbitrary"),
        name="final_ln2",
    )(x1, ft, mod, ln_g, ln_b)


def _pack_w_in(w_in, d, d_b, heads):
    o_small = 3 * d
    o_u = o_small + SMALL
    o_v = o_u + d_b
    o_ga = o_v + d_b
    o_gb = o_ga + d
    small = jnp.pad(w_in[:, o_small:o_u], ((0, 0), (0, SMALL_PAD - SMALL)))
    rkv = _chan_swap(w_in[:, :o_small].reshape(-1, 3, d), heads).reshape(-1, o_small)
    return jnp.concatenate([rkv, w_in[:, o_ga:o_gb], w_in[:, o_gb:o_gb + d],
                            w_in[:, o_u:o_v], w_in[:, o_v:o_ga], small], axis=1).astype(BF16)


def _dense_token_tile(m):
    for tt in (768, 512, 256):
        if (-m) % tt <= m // 20:
            return tt
    return LANES


def _chan_swap(x, heads, axis=-1):
    axis = axis % x.ndim
    shp = x.shape
    x = x.reshape(shp[:axis] + (heads, HEAD_A) + shp[axis + 1:])
    return jnp.swapaxes(x, axis, axis + 1).reshape(shp)


def _chan_unswap(x, heads, axis=-1):
    axis = axis % x.ndim
    shp = x.shape
    x = x.reshape(shp[:axis] + (HEAD_A, heads) + shp[axis + 1:])
    return jnp.swapaxes(x, axis, axis + 1).reshape(shp)


def _split_shift(s, d, heads):
    main = _chan_swap(s[:, :3 * d].reshape(-1, 3, d), heads).reshape(-1, 3 * d)
    return main, jnp.pad(s[:, 3 * d:], ((0, 0), (0, SMALL_PAD - SMALL)))


def _join_shift(proj_rows, d, heads):
    small_off = 6 * d
    main = _chan_unswap(proj_rows[:, :3 * d].reshape(-1, 3, d), heads).reshape(-1, 3 * d)
    return jnp.concatenate([main, proj_rows[:, small_off:small_off + SMALL]], axis=1)


_SCAN_PARAMS = ("lnx_g", "lnx_b", "r_k", "k_k", "k_a")


def _rwkv_prompt(planes, bsz, heads, p, tc):
    tile = lambda x: jnp.tile(x.reshape(heads, HEAD_A).T, (1, bsz))[:, None, :]
    s0 = jnp.zeros((1, HEAD_A, HEAD_A, LANES), F32)
    y, st = _scan(planes, s0, [tile(p[n]) for n in _SCAN_PARAMS], tc)
    wkv = st[0].reshape(HEAD_A, HEAD_A, bsz, heads).transpose(2, 3, 1, 0)
    return y, wkv


def _rwkv_sample(pre, nb, heads, wkv0, p):
    to_l = lambda x: x.reshape(nb, HEAD_A, heads).transpose(1, 2, 0).reshape(HEAD_A, 1, heads * nb)
    rep = lambda x: jnp.repeat(x.reshape(heads, HEAD_A).T, nb, axis=1)[:, None, :]
    s0 = wkv0.transpose(1, 3, 2, 0)
    y, st = _scan([to_l(x) for x in pre], s0, [rep(p[n]) for n in _SCAN_PARAMS], 1)
    y = y.reshape(HEAD_A, heads, nb).transpose(2, 0, 1).reshape(nb, heads * HEAD_A)
    return y, st.transpose(3, 0, 2, 1)


def _layer(x_p, x_s, c_p, c_s, wkv_s, shift_s, p):
    bsz, t_len, d = x_p.shape
    nb = x_s.shape[0]
    heads = d // HEAD_A
    d_b = d // 2
    alpha = p["alpha"]
    mp = bsz * t_len

    assert bsz * heads == LANES and nb == LANES, "RWKV-7 recurrence fills 128 lanes per group"
    swap = functools.partial(_chan_swap, heads=heads)
    mu_main, mu_small = _split_shift(p["mu_shift"][None], d, heads)
    lp = {
        "mu_main": mu_main, "mu_small": mu_small,
        "w0": swap(p["w0"])[None], "w2": swap(p["w2"]), "a0": swap(p["a0"])[None],
        "a2": swap(p["a2"]), "g2": swap(p["g2"]),
        "sgu_ln_g": p["sgu_ln_g"][None], "sgu_ln_b": p["sgu_ln_b"][None], "sgu_w": p["sgu_w"],
        "sgu_b_full": jnp.broadcast_to(p["sgu_b"][:, :, None], p["sgu_b"].shape + (GROUP_B,)),
        "sgu_w_first": jnp.repeat(p["sgu_w"][:, 0, 0], GROUP_B)[None],
        "sgu_b_first": jnp.repeat(p["sgu_b"][:, 0], GROUP_B)[None],
    }
    ln1 = (p["ln1_g"][None], p["ln1_b"][None])
    ln2 = (p["ln2_g"][None], p["ln2_b"][None])

    c_all = jnp.concatenate([c_p, c_s], axis=0)
    mod = _modulation(jnp.pad(c_all, ((0, (-c_all.shape[0]) % SUBLANES), (0, 0))),
                      p["w_ada"], p["b_ada"])
    mod_p = mod[:bsz].reshape(bsz, 1, 6 * d)
    mod_s = mod[bsz:bsz + nb]

    w_pack = _pack_w_in(p["w_in"], d, d_b, heads)
    n_pack = w_pack.shape[1]
    tn_in = _pick(n_pack, (1280, 512))
    xp2 = x_p.reshape(mp, d)
    xs2 = x_s.reshape(nb, d)
    tm_big = _pick(t_len, (1024, 512, 256, 128))
    tm_mid = _pick(t_len, (512, 256, 128))
    tm_small = _pick(t_len, (256, 128))
    proj_p = _inproj(xp2, mod_p, w_pack, False, t_len, tm_big, tn_in)
    proj_s = _inproj(xs2, mod_s, w_pack, True, 1, nb, tn_in)

    zm = jnp.zeros((bsz, 1, 3 * d), F32)
    zs = jnp.zeros((bsz, 1, SMALL_PAD), F32)
    *planes_p, g_p = _rwkv_pre(proj_p.reshape(bsz, t_len, n_pack), zm, zs, lp, d, heads, False,
                               _pick(t_len, (32, 16, 8)))
    sm, ss = _split_shift(shift_s, d, heads)
    *pre_s, g_s = _rwkv_pre(proj_s[None], sm[None], ss[None], lp, d, heads, True, nb)
    y_planes, wkv_p = _rwkv_prompt(planes_p, bsz, heads, p, _pick(t_len, (64, 32, 16, 8)))
    ya_p = _from_planes(y_planes, bsz, heads, _pick(t_len, (128, 64, 32, 16, 8))).reshape(mp, d)
    g_p = g_p.reshape(mp, d)
    ya_s, wkv_s_new = _rwkv_sample([x[0] for x in pre_s], nb, heads, wkv_s, p)
    g_s = g_s[0]

    u_blk = (5 * d) // d_b
    yb_p = _sgu_prompt(proj_p, lp, d_b, u_blk, tm_mid)
    yb_s, vrows_s = _sgu_single(proj_s, lp, d_b, u_blk)

    wa = _chan_swap(p["w_br_a"], heads, axis=0).astype(BF16)
    wb = p["w_br_b"].astype(BF16)
    wo = p["w_o"].astype(BF16)
    wq = p["peer_wq"].astype(BF16)
    tn_mrg = _pick(d, (512, 256, 128))
    mrg_p = _merge(ya_p, g_p, yb_p, proj_p, wa, wb, d, tm_mid, tn_mrg)
    mrg_s = _merge(ya_s, g_s, yb_s, proj_s, wa, wb, d, nb, tn_mrg)
    x1_p, h2_p, q_p = _out_ln(mrg_p, xp2, mod_p, wo, wq, *ln1, False, t_len, alpha, tm_small)
    x1_s, h2_s, q_s = _out_ln(mrg_s, xs2, mod_s, wo, wq, *ln1, True, 1, alpha, nb)

    m_all = mp + nb
    tt = _dense_token_tile(m_all)
    pad_rows = (-m_all) % tt
    h2 = jnp.concatenate([h2_p, h2_s, jnp.zeros((pad_rows, d), BF16)], axis=0)
    q = jnp.concatenate([q_p, q_s, jnp.zeros((pad_rows, q_p.shape[1]), BF16)], axis=0)
    n1, p1, r2, p2 = _peer_topk(q, p["peer_keys"], LANES)
    ft = _peer_dense(h2, p["peer_u"].astype(BF16), p["peer_v"].T.astype(BF16), n1, p1, r2, p2,
                     tt, 8 * N_KEYS)

    y_p = _final_ln(x1_p, ft, 0, mod_p, *ln2, False, t_len, alpha, tm_small)
    y_s = _final_ln(x1_s, ft, mp, mod_s, *ln2, True, 1, alpha, nb)

    last = proj_p.reshape(bsz, t_len, -1)[:, -1]
    return (y_p.reshape(bsz, t_len, d), y_s.reshape(nb, 1, d), wkv_p, _join_shift(last, d, heads),
            wkv_s_new, _join_shift(proj_s, d, heads), vrows_s.reshape(nb, 1, d_b))


def kernel(x_prompt, x_sample, c_prompt, c_sample, state_wkv, state_shift, w_ada, b_ada, w_in, mu_shift, w0, w2, a0, a2, g2, k_k, k_a, r_k, lnx_g, lnx_b, sgu_ln_g, sgu_ln_b, sgu_w, sgu_b, w_br_a, w_br_b, w_o, ln1_g, ln1_b, peer_wq, peer_keys, peer_u, peer_v, ln2_g, ln2_b):
    names = ("w_ada", "b_ada", "w_in", "mu_shift", "w0", "w2", "a0", "a2", "g2", "k_k", "k_a",
             "r_k", "lnx_g", "lnx_b", "sgu_ln_g", "sgu_ln_b", "sgu_w", "sgu_b", "w_br_a",
             "w_br_b", "w_o", "ln1_g", "ln1_b", "peer_wq", "peer_keys", "peer_u", "peer_v",
             "ln2_g", "ln2_b")
    stacked = (w_ada, b_ada, w_in, mu_shift, w0, w2, a0, a2, g2, k_k, k_a, r_k, lnx_g, lnx_b,
               sgu_ln_g, sgu_ln_b, sgu_w, sgu_b, w_br_a, w_br_b, w_o, ln1_g, ln1_b, peer_wq,
               peer_keys, peer_u, peer_v, ln2_g, ln2_b)
    depth = w_ada.shape[0]
    alpha = (2 * depth) ** 0.25
    y_p, y_s = x_prompt, x_sample
    outs = [[] for _ in range(5)]
    for l in range(depth):
        p = {n: a[l] for n, a in zip(names, stacked)}
        p["alpha"] = alpha
        y_p, y_s, *state = _layer(y_p, y_s, c_prompt, c_sample, state_wkv[l], state_shift[l], p)
        for acc, s in zip(outs, state):
            acc.append(s)
    return (y_p, y_s) + tuple(jnp.stack(o) for o in outs)
```

```python
import functools

import jax
import jax.numpy as jnp
from jax import lax
from jax.experimental import pallas as pl
from jax.experimental.pallas import tpu as pltpu

F32 = jnp.float32
BF16 = jnp.bfloat16

HEAD_A = 64
R_DECAY = 96
R_AAA = 96
R_GATE = 256
SMALL = R_DECAY + R_AAA + R_GATE
SMALL_PAD = 512
CHUNK = 128
GROUP_B = 128
PEER_HEADS = 8
N_KEYS = 128
DK_HALF = 128
TOPK = 16
LN_EPS = 1e-5
GN_EPS = 64e-5

LANES = 128
SUBLANES = 8
VMEM_LIMIT = 56 * 1024 * 1024


def _cparams(*sem):
    return pltpu.CompilerParams(dimension_semantics=sem, vmem_limit_bytes=VMEM_LIMIT)


def _gelu(x):
    return jax.nn.gelu(x, approximate=True)


def _layer_norm_rows(x, g, b, eps):
    mu = jnp.mean(x, axis=-1, keepdims=True)
    xc = x - mu
    var = jnp.mean(xc * xc, axis=-1, keepdims=True)
    return xc * lax.rsqrt(var + eps) * g + b


def _pick(n, prefs):
    for p in prefs:
        if n % p == 0:
            return p
    return n


def _mod_spec(per_row, tm, d, chunk, rows_per_seq):
    if per_row:
        return pl.BlockSpec((tm, d), lambda i, *_: (i, chunk))
    return pl.BlockSpec((None, 1, d), lambda i, *_: ((i * tm) // rows_per_seq, 0, chunk))


def _mod_kernel(c_ref, w_ref, b_ref, o_ref):
    c = c_ref[...]
    s = (c * jax.nn.sigmoid(c)).astype(BF16)
    o_ref[...] = jnp.dot(s, w_ref[...].astype(BF16), preferred_element_type=F32) + b_ref[...]


def _modulation(c_all, w_ada, b_ada):
    m, d = c_all.shape
    n = w_ada.shape[1]
    tn = _pick(n, (1536, 512))
    return pl.pallas_call(
        _mod_kernel,
        grid=(n // tn,),
        in_specs=[pl.BlockSpec((m, d), lambda j: (0, 0)),
                  pl.BlockSpec((d, tn), lambda j: (0, j)),
                  pl.BlockSpec((1, tn), lambda j: (0, j))],
        out_specs=pl.BlockSpec((m, tn), lambda j: (0, j)),
        out_shape=jax.ShapeDtypeStruct((m, n), F32),
        compiler_params=_cparams("arbitrary"),
        name="adaln_mod",
    )(c_all, w_ada, b_ada.reshape(1, n))


def _inproj_kernel(x_ref, sh_ref, sc_ref, w_ref, o_ref, h_scr):
    @pl.when(pl.program_id(1) == 0)
    def _():
        h_scr[...] = (x_ref[...] * (1.0 + sc_ref[...]) + sh_ref[...]).astype(BF16)

    o_ref[...] = lax.dot_general(h_scr[...], w_ref[...], (((1,), (1,)), ((), ())),
                                 preferred_element_type=F32)


def _inproj(x, mod, w_pack, per_row, seq_len, tm, tn):
    m, d = x.shape
    n = w_pack.shape[0]
    return pl.pallas_call(
        _inproj_kernel,
        grid=(m // tm, n // tn),
        in_specs=[pl.BlockSpec((tm, d), lambda i, j: (i, 0)),
                  _mod_spec(per_row, tm, d, 0, seq_len),
                  _mod_spec(per_row, tm, d, 1, seq_len),
                  pl.BlockSpec((tn, d), lambda i, j: (j, 0))],
        out_specs=pl.BlockSpec((tm, tn), lambda i, j: (i, j)),
        out_shape=jax.ShapeDtypeStruct((m, n), F32),
        scratch_shapes=[pltpu.VMEM((tm, d), BF16)],
        compiler_params=_cparams("arbitrary", "arbitrary"),
        name="in_proj",
    )(x, mod, mod, w_pack)


def _swap_lane_groups(xs, width):
    n = len(xs)
    grp = lax.broadcasted_iota(jnp.int32, xs[0].shape, 1) // width
    moved = []
    for s in range(n):
        acc = xs[0]
        for b in range(1, n):
            acc = jnp.where(grp == (b - s) % n, xs[b], acc)
        moved.append(acc if s == 0 else pltpu.roll(acc, s * width, 1))
    outs = []
    for q in range(n):
        acc = moved[0]
        for s in range(1, n):
            acc = jnp.where(grp == (q + s) % n, moved[s], acc)
        outs.append(acc)
    return outs


def _to_planes(xs, o_ref, heads):
    per = LANES // heads
    for j in range(xs[0].shape[1] // LANES):
        outs = _swap_lane_groups([x[:, j * LANES:(j + 1) * LANES] for x in xs], heads)
        for q in range(per):
            o_ref[j * per + q] = outs[q]


def _from_planes_kernel(y_ref, o_ref, *, heads):
    per = LANES // heads
    for j in range(y_ref.shape[0] // per):
        outs = _swap_lane_groups([y_ref[j * per + q] for q in range(per)], heads)
        for b in range(per):
            o_ref[b, :, j * LANES:(j + 1) * LANES] = outs[b]


def _from_planes(y, nseq, heads, tm):
    n, t_len, _ = y.shape
    d = n * heads
    return pl.pallas_call(
        functools.partial(_from_planes_kernel, heads=heads),
        grid=(t_len // tm,),
        in_specs=[pl.BlockSpec((n, tm, LANES), lambda i: (0, i, 0))],
        out_specs=pl.BlockSpec((nseq, tm, d), lambda i: (0, i, 0)),
        out_shape=jax.ShapeDtypeStruct((nseq, t_len, d), F32),
        compiler_params=_cparams("arbitrary"),
        name="rwkv_from_planes",
    )(y)


def _rwkv_pre_kernel(pm_ref, ps_ref, qm_ref, qs_ref, stm_ref, sts_ref, mum_ref, mus_ref,
                     w0_ref, w2_ref, a0_ref, a2_ref, g2_ref,
                     r_o, w_o, k_o, v_o, a_o, g_o, *, d, heads, per_row_state):
    nseq, tm, _ = pm_ref.shape

    def shifted(p, q, st):
        if per_row_state:
            return st
        first = pl.program_id(0) == 0
        prev_row = jnp.where(first, st, q[SUBLANES - 1:SUBLANES, :])
        row = lax.broadcasted_iota(jnp.int32, p.shape, 0)
        return jnp.where(row == 0, prev_row, pltpu.roll(p, 1, 0))

    def mix(p, q, st, mu):
        return p + (shifted(p, q, st) - p) * mu

    seqs = range(nseq)
    xs_s = jnp.concatenate([mix(ps_ref[b], qs_ref[b], sts_ref[b], mus_ref[...]) for b in seqs],
                           axis=0)
    wl = xs_s[:, 0:R_DECAY]
    al = xs_s[:, R_DECAY:R_DECAY + R_AAA]
    gl = xs_s[:, R_DECAY + R_AAA:SMALL]
    wlin = w0_ref[...] + jnp.dot(jnp.tanh(wl).astype(BF16), w2_ref[...].astype(BF16),
                                 preferred_element_type=F32)
    w = -jax.nn.softplus(-wlin) - 0.5
    decay = jnp.exp(-jnp.exp(w))
    a = jax.nn.sigmoid(a0_ref[...] + jnp.dot(al.astype(BF16), a2_ref[...].astype(BF16),
                                             preferred_element_type=F32))
    g = jnp.dot(jax.nn.sigmoid(gl).astype(BF16), g2_ref[...].astype(BF16),
                preferred_element_type=F32)

    def of_seq(x, b):
        return x[b * tm:(b + 1) * tm]

    def part(n, b):
        cols = slice(n * d, (n + 1) * d)
        return mix(pm_ref[b, :, cols], qm_ref[b, :, cols], stm_ref[b, :, cols], mum_ref[:, cols])

    def emit(o_ref, xs):
        if per_row_state:
            for b in seqs:
                o_ref[b] = xs[b]
        else:
            _to_planes(xs, o_ref, heads)

    for b in seqs:
        g_o[b] = of_seq(g, b)
    emit(r_o, [part(0, b) for b in seqs])
    emit(w_o, [of_seq(decay, b) for b in seqs])
    emit(k_o, [part(1, b) for b in seqs])
    emit(v_o, [part(2, b) for b in seqs])
    emit(a_o, [of_seq(a, b) for b in seqs])


def _rwkv_pre(proj, state_main, state_small, lp, d, heads, per_row_state, tm):
    nseq, t_len, _ = proj.shape
    small_blk = (6 * d) // SMALL_PAD

    def prev_map(col):
        def f(i):
            return (0, jnp.maximum(i * (tm // SUBLANES) - 1, 0), col)
        return f

    if per_row_state:
        stm_spec = pl.BlockSpec((nseq, tm, 3 * d), lambda i: (0, i, 0))
        sts_spec = pl.BlockSpec((nseq, tm, SMALL_PAD), lambda i: (0, i, 0))
    else:
        stm_spec = pl.BlockSpec((nseq, 1, 3 * d), lambda i: (0, 0, 0))
        sts_spec = pl.BlockSpec((nseq, 1, SMALL_PAD), lambda i: (0, 0, 0))

    vec = lambda w: pl.BlockSpec((1, w), lambda i: (0, 0))
    full = lambda a: pl.BlockSpec(a.shape, lambda i: (0,) * a.ndim)
    rows = pl.BlockSpec((nseq, tm, d), lambda i: (0, i, 0))
    rows_shape = jax.ShapeDtypeStruct((nseq, t_len, d), F32)
    if per_row_state:
        vec_out, vec_shape = rows, rows_shape
    else:
        vec_out = pl.BlockSpec((HEAD_A, tm, LANES), lambda i: (0, i, 0))
        vec_shape = jax.ShapeDtypeStruct((HEAD_A, t_len, LANES), F32)
    kern = functools.partial(_rwkv_pre_kernel, d=d, heads=heads, per_row_state=per_row_state)
    in_specs = [pl.BlockSpec((nseq, tm, 3 * d), lambda i: (0, i, 0)),
                pl.BlockSpec((nseq, tm, SMALL_PAD), lambda i: (0, i, small_blk)),
                pl.BlockSpec((nseq, SUBLANES, 3 * d), prev_map(0)),
                pl.BlockSpec((nseq, SUBLANES, SMALL_PAD), prev_map(small_blk)),
                stm_spec, sts_spec, vec(3 * d), vec(SMALL_PAD), vec(d), full(lp["w2"]), vec(d),
                full(lp["a2"]), full(lp["g2"])]
    return pl.pallas_call(
        kern,
        grid=(t_len // tm,),
        in_specs=in_specs,
        out_specs=[vec_out] * 5 + [rows],
        out_shape=[vec_shape] * 5 + [rows_shape],
        compiler_params=_cparams("arbitrary"),
        name="rwkv_pre",
    )(proj, proj, proj, proj, state_main, state_small, lp["mu_main"], lp["mu_small"], lp["w0"],
      lp["w2"], lp["a0"], lp["a2"], lp["g2"])


def _scan_kernel(r_ref, w_ref, kraw_ref, v_ref, a_ref, s0_ref, lng_ref, lnb_ref, rk_ref, kkp_ref,
                 kap_ref, y_ref, st_ref, s_scr, aa_scr, bb_scr, k_ref, bon_scr, yt_scr, *, tc):
    n = HEAD_A
    c = pl.program_id(1)

    @pl.when(c == 0)
    def _():
        s_scr[...] = s0_ref[...]

    plane0 = jnp.zeros((tc, LANES), F32)

    def norm_acc(i, acc):
        kk = kraw_ref[i] * kkp_ref[i]
        return acc + kk * kk

    nrm2 = lax.fori_loop(0, n, norm_acc, plane0, unroll=SUBLANES)
    inv = 1.0 / jnp.maximum(jnp.sqrt(nrm2), 1e-12)

    def fill(i, bon):
        kraw = kraw_ref[i]
        a = a_ref[i]
        kkn = kraw * kkp_ref[i] * inv
        aa_scr[i] = -kkn
        bb_scr[i] = kkn * a
        kmod = kraw * (1.0 + (a - 1.0) * kap_ref[i])
        k_ref[i] = kmod
        return bon + r_ref[i] * kmod * rk_ref[i]

    bon_scr[...] = lax.fori_loop(0, n, fill, plane0, unroll=SUBLANES)

    zeros = jnp.zeros((n, LANES), F32)
    nblk = n // SUBLANES
    sub = lax.broadcasted_iota(jnp.int32, (SUBLANES, LANES), 0)

    def row(ref, i, t):
        return ref[i, pl.ds(t, 1), :]

    def tile_of_step(ref, t):
        blocks = []
        for m in range(nblk):
            acc = jnp.broadcast_to(row(ref, m * SUBLANES, t), (SUBLANES, LANES))
            for s in range(1, SUBLANES):
                acc = jnp.where(sub == s, row(ref, m * SUBLANES + s, t), acc)
            blocks.append(acc)
        return jnp.concatenate(blocks, axis=0)

    def sa_first(kb, acc):
        for j in range(SUBLANES):
            i = kb * SUBLANES + j
            acc = acc + s_scr[i] * row(aa_scr, i, 0)
        return acc

    sa0 = lax.fori_loop(0, nblk, sa_first, zeros)

    def step(t, sa):
        tn = jnp.minimum(t + 1, tc - 1)
        vt = tile_of_step(v_ref, t)

        def kblock(kb, carry):
            y, san = carry
            for j in range(SUBLANES):
                i = kb * SUBLANES + j
                sn = s_scr[i] * row(w_ref, i, t) + sa * row(bb_scr, i, t) + vt * row(k_ref, i, t)
                s_scr[i] = sn
                y = y + sn * row(r_ref, i, t)
                san = san + sn * row(aa_scr, i, tn)
            return y, san

        y, san = lax.fori_loop(0, nblk, kblock, (zeros, zeros), unroll=True)
        return san, y

    grp = min(SUBLANES, tc)

    def group(gb, sa):
        t0 = pl.multiple_of(gb * grp, grp)
        rows = pl.ds(t0, grp)

        def one(s, sa):
            sa, y = step(t0 + s, sa)
            yt_scr[s] = y
            return sa

        sa = lax.fori_loop(0, grp, one, sa)
        total = jnp.zeros((grp, LANES), F32)
        for i in range(n):
            acc = jnp.broadcast_to(yt_scr[0, i:i + 1, :], (grp, LANES))
            for s in range(1, grp):
                acc = jnp.where(sub[:grp] == s, yt_scr[s, i:i + 1, :], acc)
            y_ref[i, rows, :] = acc
            total = total + acc
        mu = total * (1.0 / n)
        sq = jnp.zeros((grp, LANES), F32)
        for i in range(n):
            dlt = y_ref[i, rows, :] - mu
            sq = sq + dlt * dlt
        rstd = lax.rsqrt(sq * (1.0 / n) + GN_EPS)
        bon = bon_scr[rows, :]
        for i in range(n):
            yn = (y_ref[i, rows, :] - mu) * rstd * lng_ref[i] + lnb_ref[i]
            y_ref[i, rows, :] = yn + bon * v_ref[i, rows, :]
        return sa

    lax.fori_loop(0, tc // grp, group, sa0)

    @pl.when(c == pl.num_programs(1) - 1)
    def _():
        st_ref[...] = s_scr[...]


def _scan(seq, s0, pars, tc):
    n, t_len, gl = seq[0].shape
    g = gl // LANES
    step_spec = pl.BlockSpec((n, tc, LANES), lambda gi, c: (0, c, gi))
    par_spec = pl.BlockSpec((n, 1, LANES), lambda gi, c: (0, 0, gi))
    st_spec = pl.BlockSpec((None, n, n, LANES), lambda gi, c: (gi, 0, 0, 0))
    return pl.pallas_call(
        functools.partial(_scan_kernel, tc=tc),
        grid=(g, t_len // tc),
        in_specs=[step_spec] * 5 + [st_spec] + [par_spec] * 5,
        out_specs=[step_spec, st_spec],
        out_shape=[jax.ShapeDtypeStruct((n, t_len, gl), F32),
                   jax.ShapeDtypeStruct((g, n, n, LANES), F32)],
        scratch_shapes=[pltpu.VMEM((n, n, LANES), F32),
                        pltpu.VMEM((n, tc, LANES), F32),
                        pltpu.VMEM((n, tc, LANES), F32),
                        pltpu.VMEM((n, tc, LANES), F32),
                        pltpu.VMEM((tc, LANES), F32),
                        pltpu.VMEM((min(SUBLANES, tc), n, LANES), F32)],
        compiler_params=_cparams("arbitrary", "arbitrary"),
        name="rwkv_scan",
    )(*seq, s0, *pars)


def _sgu_kernel(pu_ref, pv_ref, lg_ref, lb_ref, w_ref, b_ref, y_ref, *, chunks):
    ng = w_ref.shape[0]
    row = lax.broadcasted_iota(jnp.int32, (CHUNK, CHUNK), 0)
    col = lax.broadcasted_iota(jnp.int32, (CHUNK, CHUNK), 1)
    causal = col <= row
    for ci in range(chunks):
        rows = slice(ci * CHUNK, (ci + 1) * CHUNK)
        u = _gelu(pu_ref[rows, :])
        v = _layer_norm_rows(_gelu(pv_ref[rows, :]), lg_ref[...], lb_ref[...], LN_EPS)
        for g in range(ng):
            cols = slice(g * GROUP_B, (g + 1) * GROUP_B)
            ws = jnp.where(causal, w_ref[g], 0.0).astype(BF16)
            mixed = jnp.dot(ws, v[:, cols].astype(BF16), preferred_element_type=F32) + b_ref[g]
            y_ref[rows, cols] = (u[:, cols] * mixed).astype(y_ref.dtype)


def _sgu_prompt(proj, lp, d_b, u_blk, tm):
    m = proj.shape[0]
    ng = d_b // GROUP_B
    return pl.pallas_call(
        functools.partial(_sgu_kernel, chunks=tm // CHUNK),
        grid=(m // tm,),
        in_specs=[pl.BlockSpec((tm, d_b), lambda i: (i, u_blk)),
                  pl.BlockSpec((tm, d_b), lambda i: (i, u_blk + 1)),
                  pl.BlockSpec((1, d_b), lambda i: (0, 0)),
                  pl.BlockSpec((1, d_b), lambda i: (0, 0)),
                  pl.BlockSpec((ng, CHUNK, CHUNK), lambda i: (0, 0, 0)),
                  pl.BlockSpec((ng, CHUNK, GROUP_B), lambda i: (0, 0, 0))],
        out_specs=pl.BlockSpec((tm, d_b), lambda i: (i, 0)),
        out_shape=jax.ShapeDtypeStruct((m, d_b), BF16),
        compiler_params=_cparams("arbitrary"),
        name="sgu_chunked",
    )(proj, proj, lp["sgu_ln_g"], lp["sgu_ln_b"], lp["sgu_w"], lp["sgu_b_full"])


def _sgu_first_kernel(pu_ref, pv_ref, lg_ref, lb_ref, wd_ref, bd_ref, y_ref, v_ref):
    u = _gelu(pu_ref[...])
    v = _layer_norm_rows(_gelu(pv_ref[...]), lg_ref[...], lb_ref[...], LN_EPS)
    v_ref[...] = v
    vb = v.astype(BF16).astype(F32)
    wd = wd_ref[...].astype(BF16).astype(F32)
    y_ref[...] = (u * (vb * wd + bd_ref[...])).astype(y_ref.dtype)


def _sgu_single(proj, lp, d_b, u_blk):
    m = proj.shape[0]
    vec = pl.BlockSpec((1, d_b), lambda i: (0, 0))
    return pl.pallas_call(
        _sgu_first_kernel,
        grid=(1,),
        in_specs=[pl.BlockSpec((m, d_b), lambda i: (0, u_blk)),
                  pl.BlockSpec((m, d_b), lambda i: (0, u_blk + 1)), vec, vec, vec, vec],
        out_specs=[pl.BlockSpec((m, d_b), lambda i: (0, 0))] * 2,
        out_shape=[jax.ShapeDtypeStruct((m, d_b), BF16), jax.ShapeDtypeStruct((m, d_b), F32)],
        compiler_params=_cparams("arbitrary"),
        name="sgu_single",
    )(proj, proj, lp["sgu_ln_g"], lp["sgu_ln_b"], lp["sgu_w_first"], lp["sgu_b_first"])


def _merge_kernel(ya_ref, g_ref, yb_ref, ga_ref, gb_ref, wa_ref, wb_ref, o_ref, a_scr):
    @pl.when(pl.program_id(1) == 0)
    def _():
        a_scr[...] = (ya_ref[...] * g_ref[...]).astype(BF16)

    ta = jnp.dot(a_scr[...], wa_ref[...], preferred_element_type=F32)
    tb = jnp.dot(yb_ref[...], wb_ref[...], preferred_element_type=F32)
    o_ref[...] = (jax.nn.sigmoid(ga_ref[...]) * ta
                  + jax.nn.sigmoid(gb_ref[...]) * tb).astype(o_ref.dtype)


def _merge(ya, g, yb, proj, wa, wb, d, tm, tn):
    m = ya.shape[0]
    d_b = yb.shape[1]
    ga_blk = (3 * d) // tn
    gb_blk = (4 * d) // tn
    return pl.pallas_call(
        _merge_kernel,
        grid=(m // tm, d // tn),
        in_specs=[pl.BlockSpec((tm, d), lambda i, j: (i, 0)),
                  pl.BlockSpec((tm, d), lambda i, j: (i, 0)),
                  pl.BlockSpec((tm, d_b), lambda i, j: (i, 0)),
                  pl.BlockSpec((tm, tn), lambda i, j: (i, ga_blk + j)),
                  pl.BlockSpec((tm, tn), lambda i, j: (i, gb_blk + j)),
                  pl.BlockSpec((d, tn), lambda i, j: (0, j)),
                  pl.BlockSpec((d_b, tn), lambda i, j: (0, j))],
        out_specs=pl.BlockSpec((tm, tn), lambda i, j: (i, j)),
        out_shape=jax.ShapeDtypeStruct((m, d), BF16),
        scratch_shapes=[pltpu.VMEM((tm, d), BF16)],
        compiler_params=_cparams("arbitrary", "arbitrary"),
        name="branch_merge",
    )(ya, g, yb, proj, proj, wa, wb)


def _out_ln_kernel(m_ref, x_ref, gt_ref, sh_ref, sc_ref, wo_ref, wq_ref, g_ref, b_ref,
                   x1_ref, h2_ref, q_ref, *, alpha):
    z = jnp.dot(m_ref[...], wo_ref[...], preferred_element_type=F32)
    x1 = _layer_norm_rows(alpha * x_ref[...] + gt_ref[...] * z, g_ref[...], b_ref[...], LN_EPS)
    x1_ref[...] = x1
    h2 = (x1 * (1.0 + sc_ref[...]) + sh_ref[...]).astype(BF16)
    h2_ref[...] = h2
    q_ref[...] = jnp.dot(h2, wq_ref[...], preferred_element_type=F32).astype(q_ref.dtype)


def _out_ln(mrg, x, mod, wo, wq, ln_g, ln_b, per_row, seq_len, alpha, tm):
    m, d = x.shape
    nq = wq.shape[1]
    vec = pl.BlockSpec((1, d), lambda i: (0, 0))
    row = pl.BlockSpec((tm, d), lambda i: (i, 0))
    return pl.pallas_call(
        functools.partial(_out_ln_kernel, alpha=alpha),
        grid=(m // tm,),
        in_specs=[row, row,
                  _mod_spec(per_row, tm, d, 2, seq_len),
                  _mod_spec(per_row, tm, d, 3, seq_len),
                  _mod_spec(per_row, tm, d, 4, seq_len),
                  pl.BlockSpec((d, d), lambda i: (0, 0)),
                  pl.BlockSpec((d, nq), lambda i: (0, 0)), vec, vec],
        out_specs=[row, row, pl.BlockSpec((tm, nq), lambda i: (i, 0))],
        out_shape=[jax.ShapeDtypeStruct((m, d), F32), jax.ShapeDtypeStruct((m, d), BF16),
                   jax.ShapeDtypeStruct((m, nq), BF16)],
        compiler_params=_cparams("arbitrary"),
        name="out_proj_ln1",
    )(mrg, x, mod, mod, mod, wo, wq, ln_g, ln_b)


_PAIRS = [(i, j) for i in range(TOPK) for j in range(TOPK) if (i + 1) * (j + 1) <= TOPK]
_NPAIR_PAD = -(-len(_PAIRS) // SUBLANES) * SUBLANES


def _take_top(work, rounds, break_ties):
    rows = work.shape[0]
    iota = lax.broadcasted_iota(jnp.int32, work.shape, 0).astype(F32)
    rank = jnp.full(work.shape, float(rounds), F32)
    vals = []
    for r in range(rounds):
        mx = jnp.max(work, axis=0, keepdims=True)
        hit = work == mx
        if break_ties:
            first = jnp.min(jnp.where(hit, iota, float(rows)), axis=0, keepdims=True)
            hit = iota == first
        rank = jnp.where(hit, float(r), rank)
        work = jnp.where(hit, -jnp.inf, work)
        vals.append(mx)
    return vals, rank


def _exactly_k(rank, k):
    n = jnp.sum(jnp.where(rank < float(k), 1.0, 0.0), axis=0, keepdims=True)
    return jnp.where(n == float(k), 1.0, 0.0)


def _head_select(s1, s2, cnt_mat, cand_scr, break_ties):
    a1, rank1 = _take_top(s1, TOPK, break_ties)
    a2, rank2 = _take_top(s2, TOPK, break_ties)
    for pos, (i, j) in enumerate(_PAIRS):
        cand_scr[pos:pos + 1, :] = a1[i] + a2[j]
    cand = cand_scr[...]
    _, rank_c = _take_top(cand, TOPK, break_ties)
    sel = jnp.where(rank_c < float(TOPK), 1.0, 0.0)
    top = a1[0] + a2[0]
    z = jnp.sum(sel * jnp.exp(cand - top), axis=0, keepdims=True)
    cnt = jnp.dot(cnt_mat, sel.astype(BF16), preferred_element_type=F32)
    n1 = jnp.zeros(s1.shape, F32)
    for i in range(TOPK):
        n1 = jnp.where(rank1 == float(i), cnt[i:i + 1, :], n1)
    p1 = jnp.where(rank1 < float(TOPK), jnp.exp(s1 - a1[0]), 0.0)
    p2 = jnp.where(rank2 < float(TOPK), jnp.exp(s2 - a2[0]), 0.0) / z
    clean = _exactly_k(rank1, TOPK) * _exactly_k(rank2, TOPK) * _exactly_k(rank_c, TOPK)
    return n1, p1, rank2, p2, clean


def _peer_topk_kernel(q_ref, keys_ref, cnt_ref, n1_ref, p1_ref, r2_ref, p2_ref, cand_a, cand_b):
    nt = (((1,), (1,)), ((), ()))
    cand_a[...] = jnp.full(cand_a.shape, -jnp.inf, F32)
    cand_b[...] = jnp.full(cand_b.shape, -jnp.inf, F32)

    def scores(h):
        base = pl.multiple_of(h * 2 * DK_HALF, 2 * DK_HALF)
        q1 = q_ref[:, pl.ds(base, DK_HALF)]
        q2 = q_ref[:, pl.ds(base + DK_HALF, DK_HALF)]
        s1 = lax.dot_general(keys_ref[h, 0].astype(BF16), q1, nt, preferred_element_type=F32)
        s2 = lax.dot_general(keys_ref[h, 1].astype(BF16), q2, nt, preferred_element_type=F32)
        return s1, s2

    def write(h, res):
        n1_ref[h], p1_ref[h] = res[:2]
        r2_ref[h] = res[2].astype(r2_ref.dtype)
        p2_ref[h] = res[3].astype(p2_ref.dtype)

    def head_pair(hp, carry):
        heads = (2 * hp, 2 * hp + 1)
        scr = (cand_a, cand_b)
        sc = [scores(h) for h in heads]
        res = [_head_select(*sc[n], cnt_ref[...], scr[n], False) for n in range(2)]
        for n in range(2):
            write(heads[n], res[n])
        for n in range(2):
            @pl.when(jnp.min(res[n][4]) < 0.5)
            def _():
                write(heads[n], _head_select(*sc[n], cnt_ref[...], scr[n], True))
        return carry

    lax.fori_loop(0, PEER_HEADS // 2, head_pair, 0)


def _pair_count_matrix():
    return jnp.array([[1.0 if i == r else 0.0 for (i, _) in _PAIRS]
                      + [0.0] * (_NPAIR_PAD - len(_PAIRS)) for r in range(TOPK)], BF16)


def _peer_topk(q, keys, tt):
    m = q.shape[0]
    cnt = _pair_count_matrix()
    out = pl.BlockSpec((PEER_HEADS, N_KEYS, tt), lambda i: (0, 0, i))
    return pl.pallas_call(
        _peer_topk_kernel,
        grid=(m // tt,),
        in_specs=[pl.BlockSpec((tt, q.shape[1]), lambda i: (i, 0)),
                  pl.BlockSpec(keys.shape, lambda i: (0, 0, 0, 0)),
                  pl.BlockSpec(cnt.shape, lambda i: (0, 0))],
        out_specs=[out] * 4,
        out_shape=[jax.ShapeDtypeStruct((PEER_HEADS, N_KEYS, m), dt)
                   for dt in (F32, F32, BF16, BF16)],
        scratch_shapes=[pltpu.VMEM((_NPAIR_PAD, tt), F32)] * 2,
        compiler_params=_cparams("arbitrary"),
        name="peer_topk",
    )(q, keys, cnt)


def _peer_dense_kernel(h_ref, u_ref, vt_ref, n1_ref, p1_ref, r2_ref, p2_ref, o_ref, w_scr, g_scr,
                       *, et, n_et):
    n = pl.program_id(0)
    cur = jnp.minimum(n, pl.num_programs(0) - 2)
    e_cur = cur % n_et
    e_prev = jnp.maximum(n - 1, 0) % n_et
    nt = (((1,), (1,)), ((), ()))

    @pl.when(n == 0)
    def _():
        w_scr[1] = jnp.zeros(w_scr.shape[1:], BF16)

    @pl.when(e_prev == 0)
    def _():
        o_ref[...] = jnp.zeros(o_ref.shape, F32)

    zero = jnp.zeros((), BF16)
    for j in range(et // N_KEYS):
        e1 = e_cur * (et // N_KEYS) + j
        gate = jnp.zeros((N_KEYS, g_scr.shape[1]), BF16)
        for h in range(PEER_HEADS):
            n1 = n1_ref[h, pl.ds(e1, 1), :].astype(BF16)
            p1 = p1_ref[h, pl.ds(e1, 1), :].astype(BF16)
            gate = gate + jnp.where(r2_ref[h] < n1, p2_ref[h], zero) * p1
        g_scr[j * N_KEYS:(j + 1) * N_KEYS, :] = gate
    s = lax.dot_general(u_ref[...], h_ref[...], nt, preferred_element_type=F32)
    o_ref[...] += jnp.dot(vt_ref[...], w_scr[(n + 1) % 2], preferred_element_type=F32)
    w_scr[n % 2] = _gelu(s).astype(BF16) * g_scr[...]


def _peer_dense(h2, u_tab, vt_tab, n1, p1, r2, p2, tt, et):
    m, d = h2.shape
    n_et = u_tab.shape[0] // et
    steps = (m // tt) * n_et

    def cur(fn):
        return lambda n: fn(jnp.minimum(n, steps - 1))

    def prev(fn):
        return lambda n: fn(jnp.maximum(n - 1, 0))

    once = pl.Buffered(1)
    sel = pl.BlockSpec((PEER_HEADS, N_KEYS, tt), cur(lambda c: (0, 0, c // n_et)),
                       pipeline_mode=once)
    return pl.pallas_call(
        functools.partial(_peer_dense_kernel, et=et, n_et=n_et),
        grid=(steps + 1,),
        in_specs=[pl.BlockSpec((tt, d), cur(lambda c: (c // n_et, 0))),
                  pl.BlockSpec((et, d), cur(lambda c: (c % n_et, 0))),
                  pl.BlockSpec((d, et), prev(lambda c: (0, c % n_et))),
                  sel, sel, sel, sel],
        out_specs=pl.BlockSpec((d, tt), prev(lambda c: (0, c // n_et))),
        out_shape=jax.ShapeDtypeStruct((d, m), F32),
        scratch_shapes=[pltpu.VMEM((2, et, tt), BF16), pltpu.VMEM((et, tt), BF16)],
        compiler_params=_cparams("arbitrary"),
        name="peer_dense",
    )(h2, u_tab, vt_tab, n1, p1, r2, p2)


def _final_kernel(x1_ref, ft_ref, gt_ref, g_ref, b_ref, o_ref, *, alpha):
    f = ft_ref[...].T
    o_ref[...] = _layer_norm_rows(alpha * x1_ref[...] + gt_ref[...] * f,
                                  g_ref[...], b_ref[...], LN_EPS)


def _final_ln(x1, ft, first_token, mod, ln_g, ln_b, per_row, seq_len, alpha, tm):
    m, d = x1.shape
    col0 = first_token // tm
    row = pl.BlockSpec((tm, d), lambda i: (i, 0))
    vec = pl.BlockSpec((1, d), lambda i: (0, 0))
    return pl.pallas_call(
        functools.partial(_final_kernel, alpha=alpha),
        grid=(m // tm,),
        in_specs=[row, pl.BlockSpec((d, tm), lambda i: (0, col0 + i)),
                  _mod_spec(per_row, tm, d, 5, seq_len), vec, vec],
        out_specs=row,
        out_shape=jax.ShapeDtypeStruct((m, d), F32),
        compiler_params=_cparams("arbitrary"),
        name="final_ln2",
    )(x1, ft, mod, ln_g, ln_b)


def _pack_w_in(w_in, d, d_b, heads):
    o_small = 3 * d
    o_u = o_small + SMALL
    o_v = o_u + d_b
    o_ga = o_v + d_b
    o_gb = o_ga + d
    wt = w_in.T
    small = jnp.pad(wt[o_small:o_u], ((0, SMALL_PAD - SMALL), (0, 0)))
    rkv = _chan_swap(wt[:o_small].reshape(3, d, -1), heads, axis=1).reshape(o_small, -1)
    return jnp.concatenate([rkv, wt[o_ga:o_gb], wt[o_gb:o_gb + d], wt[o_u:o_v], wt[o_v:o_ga],
                            small], axis=0).astype(BF16)


def _dense_token_tile(m):
    for tt in (768, 512, 256):
        if (-m) % tt <= m // 20:
            return tt
    return LANES


def _chan_swap(x, heads, axis=-1):
    axis = axis % x.ndim
    shp = x.shape
    x = x.reshape(shp[:axis] + (heads, HEAD_A) + shp[axis + 1:])
    return jnp.swapaxes(x, axis, axis + 1).reshape(shp)


def _chan_unswap(x, heads, axis=-1):
    axis = axis % x.ndim
    shp = x.shape
    x = x.reshape(shp[:axis] + (HEAD_A, heads) + shp[axis + 1:])
    return jnp.swapaxes(x, axis, axis + 1).reshape(shp)


def _split_shift(s, d, heads):
    main = _chan_swap(s[:, :3 * d].reshape(-1, 3, d), heads).reshape(-1, 3 * d)
    return main, jnp.pad(s[:, 3 * d:], ((0, 0), (0, SMALL_PAD - SMALL)))


def _join_shift(proj_rows, d, heads):
    small_off = 6 * d
    main = _chan_unswap(proj_rows[:, :3 * d].reshape(-1, 3, d), heads).reshape(-1, 3 * d)
    return jnp.concatenate([main, proj_rows[:, small_off:small_off + SMALL]], axis=1)


_SCAN_PARAMS = ("lnx_g", "lnx_b", "r_k", "k_k", "k_a")


def _rwkv_prompt(planes, bsz, heads, p, tc):
    tile = lambda x: jnp.tile(x.reshape(heads, HEAD_A).T, (1, bsz))[:, None, :]
    s0 = jnp.zeros((1, HEAD_A, HEAD_A, LANES), F32)
    y, st = _scan(planes, s0, [tile(p[n]) for n in _SCAN_PARAMS], tc)
    wkv = st[0].reshape(HEAD_A, HEAD_A, bsz, heads).transpose(2, 3, 1, 0)
    return y, wkv


def _rwkv_sample(pre, nb, heads, wkv0, p):
    to_l = lambda x: x.reshape(nb, HEAD_A, heads).transpose(1, 2, 0).reshape(HEAD_A, 1, heads * nb)
    rep = lambda x: jnp.repeat(x.reshape(heads, HEAD_A).T, nb, axis=1)[:, None, :]
    s0 = wkv0.transpose(1, 3, 2, 0)
    y, st = _scan([to_l(x) for x in pre], s0, [rep(p[n]) for n in _SCAN_PARAMS], 1)
    y = y.reshape(HEAD_A, heads, nb).transpose(2, 0, 1).reshape(nb, heads * HEAD_A)
    return y, st.transpose(3, 0, 2, 1)


def _layer(x_p, x_s, c_p, c_s, wkv_s, shift_s, p):
    bsz, t_len, d = x_p.shape
    nb = x_s.shape[0]
    heads = d // HEAD_A
    d_b = d // 2
    alpha = p["alpha"]
    mp = bsz * t_len

    assert bsz * heads == LANES and nb == LANES, "RWKV-7 recurrence fills 128 lanes per group"
    swap = functools.partial(_chan_swap, heads=heads)
    mu_main, mu_small = _split_shift(p["mu_shift"][None], d, heads)
    lp = {
        "mu_main": mu_main, "mu_small": mu_small,
        "w0": swap(p["w0"])[None], "w2": swap(p["w2"]), "a0": swap(p["a0"])[None],
        "a2": swap(p["a2"]), "g2": swap(p["g2"]),
        "sgu_ln_g": p["sgu_ln_g"][None], "sgu_ln_b": p["sgu_ln_b"][None], "sgu_w": p["sgu_w"],
        "sgu_b_full": jnp.broadcast_to(p["sgu_b"][:, :, None], p["sgu_b"].shape + (GROUP_B,)),
        "sgu_w_first": jnp.repeat(p["sgu_w"][:, 0, 0], GROUP_B)[None],
        "sgu_b_first": jnp.repeat(p["sgu_b"][:, 0], GROUP_B)[None],
    }
    ln1 = (p["ln1_g"][None], p["ln1_b"][None])
    ln2 = (p["ln2_g"][None], p["ln2_b"][None])

    c_all = jnp.concatenate([c_p, c_s], axis=0)
    mod = _modulation(jnp.pad(c_all, ((0, (-c_all.shape[0]) % SUBLANES), (0, 0))),
                      p["w_ada"], p["b_ada"])
    mod_p = mod[:bsz].reshape(bsz, 1, 6 * d)
    mod_s = mod[bsz:bsz + nb]

    w_pack = _pack_w_in(p["w_in"], d, d_b, heads)
    n_pack = w_pack.shape[0]
    tn_in = _pick(n_pack, (1280, 512))
    xp2 = x_p.reshape(mp, d)
    xs2 = x_s.reshape(nb, d)
    tm_big = _pick(t_len, (1024, 512, 256, 128))
    tm_mid = _pick(t_len, (512, 256, 128))
    tm_small = _pick(t_len, (256, 128))
    proj_p = _inproj(xp2, mod_p, w_pack, False, t_len, tm_big, tn_in)
    proj_s = _inproj(xs2, mod_s, w_pack, True, 1, nb, tn_in)

    zm = jnp.zeros((bsz, 1, 3 * d), F32)
    zs = jnp.zeros((bsz, 1, SMALL_PAD), F32)
    *planes_p, g_p = _rwkv_pre(proj_p.reshape(bsz, t_len, n_pack), zm, zs, lp, d, heads, False,
                               _pick(t_len, (32, 16, 8)))
    sm, ss = _split_shift(shift_s, d, heads)
    *pre_s, g_s = _rwkv_pre(proj_s[None], sm[None], ss[None], lp, d, heads, True, nb)
    y_planes, wkv_p = _rwkv_prompt(planes_p, bsz, heads, p, _pick(t_len, (64, 32, 16, 8)))
    ya_p = _from_planes(y_planes, bsz, heads, _pick(t_len, (128, 64, 32, 16, 8))).reshape(mp, d)
    g_p = g_p.reshape(mp, d)
    ya_s, wkv_s_new = _rwkv_sample([x[0] for x in pre_s], nb, heads, wkv_s, p)
    g_s = g_s[0]

    u_blk = (5 * d) // d_b
    yb_p = _sgu_prompt(proj_p, lp, d_b, u_blk, tm_mid)
    yb_s, vrows_s = _sgu_single(proj_s, lp, d_b, u_blk)

    wa = _chan_swap(p["w_br_a"], heads, axis=0).astype(BF16)
    wb = p["w_br_b"].astype(BF16)
    wo = p["w_o"].astype(BF16)
    wq = p["peer_wq"].astype(BF16)
    tn_mrg = _pick(d, (512, 256, 128))
    mrg_p = _merge(ya_p, g_p, yb_p, proj_p, wa, wb, d, tm_mid, tn_mrg)
    mrg_s = _merge(ya_s, g_s, yb_s, proj_s, wa, wb, d, nb, tn_mrg)
    x1_p, h2_p, q_p = _out_ln(mrg_p, xp2, mod_p, wo, wq, *ln1, False, t_len, alpha, tm_small)
    x1_s, h2_s, q_s = _out_ln(mrg_s, xs2, mod_s, wo, wq, *ln1, True, 1, alpha, nb)

    m_all = mp + nb
    tt = _dense_token_tile(m_all)
    pad_rows = (-m_all) % tt
    h2 = jnp.concatenate([h2_p, h2_s, jnp.zeros((pad_rows, d), BF16)], axis=0)
    q = jnp.concatenate([q_p, q_s, jnp.zeros((pad_rows, q_p.shape[1]), BF16)], axis=0)
    n1, p1, r2, p2 = _peer_topk(q, p["peer_keys"], LANES)
    ft = _peer_dense(h2, p["peer_u"].astype(BF16), p["peer_v"].T.astype(BF16), n1, p1, r2, p2,
                     tt, 8 * N_KEYS)

    y_p = _final_ln(x1_p, ft, 0, mod_p, *ln2, False, t_len, alpha, tm_small)
    y_s = _final_ln(x1_s, ft, mp, mod_s, *ln2, True, 1, alpha, nb)

    last = proj_p.reshape(bsz, t_len, -1)[:, -1]
    return (y_p.reshape(bsz, t_len, d), y_s.reshape(nb, 1, d), wkv_p, _join_shift(last, d, heads),
            wkv_s_new, _join_shift(proj_s, d, heads), vrows_s.reshape(nb, 1, d_b))


def kernel(x_prompt, x_sample, c_prompt, c_sample, state_wkv, state_shift, w_ada, b_ada, w_in, mu_shift, w0, w2, a0, a2, g2, k_k, k_a, r_k, lnx_g, lnx_b, sgu_ln_g, sgu_ln_b, sgu_w, sgu_b, w_br_a, w_br_b, w_o, ln1_g, ln1_b, peer_wq, peer_keys, peer_u, peer_v, ln2_g, ln2_b):
    names = ("w_ada", "b_ada", "w_in", "mu_shift", "w0", "w2", "a0", "a2", "g2", "k_k", "k_a",
             "r_k", "lnx_g", "lnx_b", "sgu_ln_g", "sgu_ln_b", "sgu_w", "sgu_b", "w_br_a",
             "w_br_b", "w_o", "ln1_g", "ln1_b", "peer_wq", "peer_keys", "peer_u", "peer_v",
             "ln2_g", "ln2_b")
    stacked = (w_ada, b_ada, w_in, mu_shift, w0, w2, a0, a2, g2, k_k, k_a, r_k, lnx_g, lnx_b,
               sgu_ln_g, sgu_ln_b, sgu_w, sgu_b, w_br_a, w_br_b, w_o, ln1_g, ln1_b, peer_wq,
               peer_keys, peer_u, peer_v, ln2_g, ln2_b)
    depth = w_ada.shape[0]
    alpha = (2 * depth) ** 0.25
    y_p, y_s = x_prompt, x_sample
    outs = [[] for _ in range(5)]
    for l in range(depth):
        p = {n: a[l] for n, a in zip(names, stacked)}
        p["alpha"] = alpha
        y_p, y_s, *state = _layer(y_p, y_s, c_prompt, c_sample, state_wkv[l], state_shift[l], p)
        for acc, s in zip(outs, state):
            acc.append(s)
    return (y_p, y_s) + tuple(jnp.stack(o) for o in outs)
```

```python
import functools

import jax
import jax.numpy as jnp
from jax import lax
from jax.experimental import pallas as pl
from jax.experimental.pallas import tpu as pltpu

F32 = jnp.float32
BF16 = jnp.bfloat16

HEAD_A = 64
R_DECAY = 96
R_AAA = 96
R_GATE = 256
SMALL = R_DECAY + R_AAA + R_GATE
SMALL_PAD = 512
CHUNK = 128
GROUP_B = 128
PEER_HEADS = 8
N_KEYS = 128
DK_HALF = 128
TOPK = 16
LN_EPS = 1e-5
GN_EPS = 64e-5

LANES = 128
SUBLANES = 8
VMEM_LIMIT = 56 * 1024 * 1024


def _cparams(*sem):
    return pltpu.CompilerParams(dimension_semantics=sem, vmem_limit_bytes=VMEM_LIMIT)


def _gelu(x):
    return jax.nn.gelu(x, approximate=True)


def _layer_norm_rows(x, g, b, eps):
    mu = jnp.mean(x, axis=-1, keepdims=True)
    xc = x - mu
    var = jnp.mean(xc * xc, axis=-1, keepdims=True)
    return xc * lax.rsqrt(var + eps) * g + b


def _pick(n, prefs):
    for p in prefs:
        if n % p == 0:
            return p
    return n


def _mod_spec(per_row, tm, d, chunk, rows_per_seq):
    if per_row:
        return pl.BlockSpec((tm, d), lambda i, *_: (i, chunk))
    return pl.BlockSpec((None, 1, d), lambda i, *_: ((i * tm) // rows_per_seq, 0, chunk))


def _mod_kernel(c_ref, w_ref, b_ref, o_ref):
    c = c_ref[...]
    s = (c * jax.nn.sigmoid(c)).astype(BF16)
    o_ref[...] = jnp.dot(s, w_ref[...].astype(BF16), preferred_element_type=F32) + b_ref[...]


def _modulation(c_all, w_ada, b_ada):
    m, d = c_all.shape
    n = w_ada.shape[1]
    tn = _pick(n, (1536, 512))
    return pl.pallas_call(
        _mod_kernel,
        grid=(n // tn,),
        in_specs=[pl.BlockSpec((m, d), lambda j: (0, 0)),
                  pl.BlockSpec((d, tn), lambda j: (0, j)),
                  pl.BlockSpec((1, tn), lambda j: (0, j))],
        out_specs=pl.BlockSpec((m, tn), lambda j: (0, j)),
        out_shape=jax.ShapeDtypeStruct((m, n), F32),
        compiler_params=_cparams("arbitrary"),
        name="adaln_mod",
    )(c_all, w_ada, b_ada.reshape(1, n))


def _inproj_kernel(x_ref, sh_ref, sc_ref, w_ref, o_ref, h_scr):
    @pl.when(pl.program_id(1) == 0)
    def _():
        h_scr[...] = (x_ref[...] * (1.0 + sc_ref[...]) + sh_ref[...]).astype(BF16)

    o_ref[...] = lax.dot_general(h_scr[...], w_ref[...], (((1,), (1,)), ((), ())),
                                 preferred_element_type=F32)


def _inproj(x, mod, w_pack, per_row, seq_len, tm, tn):
    m, d = x.shape
    n = w_pack.shape[0]
    return pl.pallas_call(
        _inproj_kernel,
        grid=(m // tm, n // tn),
        in_specs=[pl.BlockSpec((tm, d), lambda i, j: (i, 0)),
                  _mod_spec(per_row, tm, d, 0, seq_len),
                  _mod_spec(per_row, tm, d, 1, seq_len),
                  pl.BlockSpec((tn, d), lambda i, j: (j, 0))],
        out_specs=pl.BlockSpec((tm, tn), lambda i, j: (i, j)),
        out_shape=jax.ShapeDtypeStruct((m, n), F32),
        scratch_shapes=[pltpu.VMEM((tm, d), BF16)],
        compiler_params=_cparams("arbitrary", "arbitrary"),
        name="in_proj",
    )(x, mod, mod, w_pack)


def _swap_lane_groups(xs, width):
    n = len(xs)
    grp = lax.broadcasted_iota(jnp.int32, xs[0].shape, 1) // width
    moved = []
    for s in range(n):
        acc = xs[0]
        for b in range(1, n):
            acc = jnp.where(grp == (b - s) % n, xs[b], acc)
        moved.append(acc if s == 0 else pltpu.roll(acc, s * width, 1))
    outs = []
    for q in range(n):
        acc = moved[0]
        for s in range(1, n):
            acc = jnp.where(grp == (q + s) % n, moved[s], acc)
        outs.append(acc)
    return outs


def _to_planes(xs, o_ref, heads):
    per = LANES // heads
    for j in range(xs[0].shape[1] // LANES):
        outs = _swap_lane_groups([x[:, j * LANES:(j + 1) * LANES] for x in xs], heads)
        for q in range(per):
            o_ref[j * per + q] = outs[q]


def _from_planes_kernel(y_ref, o_ref, *, heads):
    per = LANES // heads
    for j in range(y_ref.shape[0] // per):
        outs = _swap_lane_groups([y_ref[j * per + q] for q in range(per)], heads)
        for b in range(per):
            o_ref[b, :, j * LANES:(j + 1) * LANES] = outs[b]


def _from_planes(y, nseq, heads, tm):
    n, t_len, _ = y.shape
    d = n * heads
    return pl.pallas_call(
        functools.partial(_from_planes_kernel, heads=heads),
        grid=(t_len // tm,),
        in_specs=[pl.BlockSpec((n, tm, LANES), lambda i: (0, i, 0))],
        out_specs=pl.BlockSpec((nseq, tm, d), lambda i: (0, i, 0)),
        out_shape=jax.ShapeDtypeStruct((nseq, t_len, d), F32),
        compiler_params=_cparams("arbitrary"),
        name="rwkv_from_planes",
    )(y)


def _rwkv_pre_kernel(pm_ref, ps_ref, qm_ref, qs_ref, stm_ref, sts_ref, mum_ref, mus_ref,
                     w0_ref, w2_ref, a0_ref, a2_ref, g2_ref,
                     r_o, w_o, k_o, v_o, a_o, g_o, *, d, heads, per_row_state):
    nseq, tm, _ = pm_ref.shape

    def shifted(p, q, st):
        if per_row_state:
            return st
        first = pl.program_id(0) == 0
        prev_row = jnp.where(first, st, q[SUBLANES - 1:SUBLANES, :])
        row = lax.broadcasted_iota(jnp.int32, p.shape, 0)
        return jnp.where(row == 0, prev_row, pltpu.roll(p, 1, 0))

    def mix(p, q, st, mu):
        return p + (shifted(p, q, st) - p) * mu

    seqs = range(nseq)
    xs_s = jnp.concatenate([mix(ps_ref[b], qs_ref[b], sts_ref[b], mus_ref[...]) for b in seqs],
                           axis=0)
    wl = xs_s[:, 0:R_DECAY]
    al = xs_s[:, R_DECAY:R_DECAY + R_AAA]
    gl = xs_s[:, R_DECAY + R_AAA:SMALL]
    wlin = w0_ref[...] + jnp.dot(jnp.tanh(wl).astype(BF16), w2_ref[...].astype(BF16),
                                 preferred_element_type=F32)
    w = -jax.nn.softplus(-wlin) - 0.5
    decay = jnp.exp(-jnp.exp(w))
    a = jax.nn.sigmoid(a0_ref[...] + jnp.dot(al.astype(BF16), a2_ref[...].astype(BF16),
                                             preferred_element_type=F32))
    g = jnp.dot(jax.nn.sigmoid(gl).astype(BF16), g2_ref[...].astype(BF16),
                preferred_element_type=F32)

    def of_seq(x, b):
        return x[b * tm:(b + 1) * tm]

    def part(n, b):
        cols = slice(n * d, (n + 1) * d)
        return mix(pm_ref[b, :, cols], qm_ref[b, :, cols], stm_ref[b, :, cols], mum_ref[:, cols])

    def emit(o_ref, xs):
        if per_row_state:
            for b in seqs:
                o_ref[b] = xs[b]
        else:
            _to_planes(xs, o_ref, heads)

    for b in seqs:
        g_o[b] = of_seq(g, b)
    emit(r_o, [part(0, b) for b in seqs])
    emit(w_o, [of_seq(decay, b) for b in seqs])
    emit(k_o, [part(1, b) for b in seqs])
    emit(v_o, [part(2, b) for b in seqs])
    emit(a_o, [of_seq(a, b) for b in seqs])


def _rwkv_pre(proj, state_main, state_small, lp, d, heads, per_row_state, tm):
    nseq, t_len, _ = proj.shape
    small_blk = (6 * d) // SMALL_PAD

    def prev_map(col):
        def f(i):
            return (0, jnp.maximum(i * (tm // SUBLANES) - 1, 0), col)
        return f

    if per_row_state:
        stm_spec = pl.BlockSpec((nseq, tm, 3 * d), lambda i: (0, i, 0))
        sts_spec = pl.BlockSpec((nseq, tm, SMALL_PAD), lambda i: (0, i, 0))
    else:
        stm_spec = pl.BlockSpec((nseq, 1, 3 * d), lambda i: (0, 0, 0))
        sts_spec = pl.BlockSpec((nseq, 1, SMALL_PAD), lambda i: (0, 0, 0))

    vec = lambda w: pl.BlockSpec((1, w), lambda i: (0, 0))
    full = lambda a: pl.BlockSpec(a.shape, lambda i: (0,) * a.ndim)
    rows = pl.BlockSpec((nseq, tm, d), lambda i: (0, i, 0))
    rows_shape = jax.ShapeDtypeStruct((nseq, t_len, d), F32)
    if per_row_state:
        vec_out, vec_shape = rows, rows_shape
    else:
        vec_out = pl.BlockSpec((HEAD_A, tm, LANES), lambda i: (0, i, 0))
        vec_shape = jax.ShapeDtypeStruct((HEAD_A, t_len, LANES), F32)
    kern = functools.partial(_rwkv_pre_kernel, d=d, heads=heads, per_row_state=per_row_state)
    in_specs = [pl.BlockSpec((nseq, tm, 3 * d), lambda i: (0, i, 0)),
                pl.BlockSpec((nseq, tm, SMALL_PAD), lambda i: (0, i, small_blk)),
                pl.BlockSpec((nseq, SUBLANES, 3 * d), prev_map(0)),
                pl.BlockSpec((nseq, SUBLANES, SMALL_PAD), prev_map(small_blk)),
                stm_spec, sts_spec, vec(3 * d), vec(SMALL_PAD), vec(d), full(lp["w2"]), vec(d),
                full(lp["a2"]), full(lp["g2"])]
    return pl.pallas_call(
        kern,
        grid=(t_len // tm,),
        in_specs=in_specs,
        out_specs=[vec_out] * 5 + [rows],
        out_shape=[vec_shape] * 5 + [rows_shape],
        compiler_params=_cparams("arbitrary"),
        name="rwkv_pre",
    )(proj, proj, proj, proj, state_main, state_small, lp["mu_main"], lp["mu_small"], lp["w0"],
      lp["w2"], lp["a0"], lp["a2"], lp["g2"])


def _scan_kernel(r_ref, w_ref, kraw_ref, v_ref, a_ref, s0_ref, lng_ref, lnb_ref, rk_ref, kkp_ref,
                 kap_ref, y_ref, st_ref, s_scr, aa_scr, bb_scr, k_ref, bon_scr, yt_scr, *, tc):
    n = HEAD_A
    c = pl.program_id(1)

    @pl.when(c == 0)
    def _():
        s_scr[...] = s0_ref[...]

    plane0 = jnp.zeros((tc, LANES), F32)

    def norm_acc(i, acc):
        kk = kraw_ref[i] * kkp_ref[i]
        return acc + kk * kk

    nrm2 = lax.fori_loop(0, n, norm_acc, plane0, unroll=SUBLANES)
    inv = 1.0 / jnp.maximum(jnp.sqrt(nrm2), 1e-12)

    def fill(i, bon):
        kraw = kraw_ref[i]
        a = a_ref[i]
        kkn = kraw * kkp_ref[i] * inv
        aa_scr[i] = -kkn
        bb_scr[i] = kkn * a
        kmod = kraw * (1.0 + (a - 1.0) * kap_ref[i])
        k_ref[i] = kmod
        return bon + r_ref[i] * kmod * rk_ref[i]

    bon_scr[...] = lax.fori_loop(0, n, fill, plane0, unroll=SUBLANES)

    zeros = jnp.zeros((n, LANES), F32)
    nblk = n // SUBLANES
    sub = lax.broadcasted_iota(jnp.int32, (SUBLANES, LANES), 0)

    def row(ref, i, t):
        return ref[i, pl.ds(t, 1), :]

    def tile_of_step(ref, t):
        blocks = []
        for m in range(nblk):
            acc = jnp.broadcast_to(row(ref, m * SUBLANES, t), (SUBLANES, LANES))
            for s in range(1, SUBLANES):
                acc = jnp.where(sub == s, row(ref, m * SUBLANES + s, t), acc)
            blocks.append(acc)
        return jnp.concatenate(blocks, axis=0)

    def sa_first(kb, acc):
        for j in range(SUBLANES):
            i = kb * SUBLANES + j
            acc = acc + s_scr[i] * row(aa_scr, i, 0)
        return acc

    sa0 = lax.fori_loop(0, nblk, sa_first, zeros)

    def step(t, sa):
        tn = jnp.minimum(t + 1, tc - 1)
        vt = tile_of_step(v_ref, t)

        def kblock(kb, carry):
            y, san = carry
            for j in range(SUBLANES):
                i = kb * SUBLANES + j
                sn = s_scr[i] * row(w_ref, i, t) + sa * row(bb_scr, i, t) + vt * row(k_ref, i, t)
                s_scr[i] = sn
                y = y + sn * row(r_ref, i, t)
                san = san + sn * row(aa_scr, i, tn)
            return y, san

        y, san = lax.fori_loop(0, nblk, kblock, (zeros, zeros), unroll=True)
        return san, y

    grp = min(SUBLANES, tc)

    def group(gb, sa):
        t0 = pl.multiple_of(gb * grp, grp)
        rows = pl.ds(t0, grp)

        def one(s, sa):
            sa, y = step(t0 + s, sa)
            yt_scr[s] = y
            return sa

        sa = lax.fori_loop(0, grp, one, sa)
        total = jnp.zeros((grp, LANES), F32)
        for i in range(n):
            acc = jnp.broadcast_to(yt_scr[0, i:i + 1, :], (grp, LANES))
            for s in range(1, grp):
                acc = jnp.where(sub[:grp] == s, yt_scr[s, i:i + 1, :], acc)
            y_ref[i, rows, :] = acc
            total = total + acc
        mu = total * (1.0 / n)
        sq = jnp.zeros((grp, LANES), F32)
        for i in range(n):
            dlt = y_ref[i, rows, :] - mu
            sq = sq + dlt * dlt
        rstd = lax.rsqrt(sq * (1.0 / n) + GN_EPS)
        bon = bon_scr[rows, :]
        for i in range(n):
            yn = (y_ref[i, rows, :] - mu) * rstd * lng_ref[i] + lnb_ref[i]
            y_ref[i, rows, :] = yn + bon * v_ref[i, rows, :]
        return sa

    lax.fori_loop(0, tc // grp, group, sa0)

    @pl.when(c == pl.num_programs(1) - 1)
    def _():
        st_ref[...] = s_scr[...]


def _scan(seq, s0, pars, tc):
    n, t_len, gl = seq[0].shape
    g = gl // LANES
    step_spec = pl.BlockSpec((n, tc, LANES), lambda gi, c: (0, c, gi))
    par_spec = pl.BlockSpec((n, 1, LANES), lambda gi, c: (0, 0, gi))
    st_spec = pl.BlockSpec((None, n, n, LANES), lambda gi, c: (gi, 0, 0, 0))
    return pl.pallas_call(
        functools.partial(_scan_kernel, tc=tc),
        grid=(g, t_len // tc),
        in_specs=[step_spec] * 5 + [st_spec] + [par_spec] * 5,
        out_specs=[step_spec, st_spec],
        out_shape=[jax.ShapeDtypeStruct((n, t_len, gl), F32),
                   jax.ShapeDtypeStruct((g, n, n, LANES), F32)],
        scratch_shapes=[pltpu.VMEM((n, n, LANES), F32),
                        pltpu.VMEM((n, tc, LANES), F32),
                        pltpu.VMEM((n, tc, LANES), F32),
                        pltpu.VMEM((n, tc, LANES), F32),
                        pltpu.VMEM((tc, LANES), F32),
                        pltpu.VMEM((min(SUBLANES, tc), n, LANES), F32)],
        compiler_params=_cparams("arbitrary", "arbitrary"),
        name="rwkv_scan",
    )(*seq, s0, *pars)


def _sgu_kernel(pu_ref, pv_ref, lg_ref, lb_ref, w_ref, b_ref, y_ref, *, chunks):
    ng = w_ref.shape[0]
    row = lax.broadcasted_iota(jnp.int32, (CHUNK, CHUNK), 0)
    col = lax.broadcasted_iota(jnp.int32, (CHUNK, CHUNK), 1)
    causal = col <= row
    for ci in range(chunks):
        rows = slice(ci * CHUNK, (ci + 1) * CHUNK)
        u = _gelu(pu_ref[rows, :])
        v = _layer_norm_rows(_gelu(pv_ref[rows, :]), lg_ref[...], lb_ref[...], LN_EPS)
        for g in range(ng):
            cols = slice(g * GROUP_B, (g + 1) * GROUP_B)
            ws = jnp.where(causal, w_ref[g], 0.0).astype(BF16)
            mixed = jnp.dot(ws, v[:, cols].astype(BF16), preferred_element_type=F32) + b_ref[g]
            y_ref[rows, cols] = (u[:, cols] * mixed).astype(y_ref.dtype)


def _sgu_prompt(proj, lp, d_b, u_blk, tm):
    m = proj.shape[0]
    ng = d_b // GROUP_B
    return pl.pallas_call(
        functools.partial(_sgu_kernel, chunks=tm // CHUNK),
        grid=(m // tm,),
        in_specs=[pl.BlockSpec((tm, d_b), lambda i: (i, u_blk)),
                  pl.BlockSpec((tm, d_b), lambda i: (i, u_blk + 1)),
                  pl.BlockSpec((1, d_b), lambda i: (0, 0)),
                  pl.BlockSpec((1, d_b), lambda i: (0, 0)),
                  pl.BlockSpec((ng, CHUNK, CHUNK), lambda i: (0, 0, 0)),
                  pl.BlockSpec((ng, CHUNK, GROUP_B), lambda i: (0, 0, 0))],
        out_specs=pl.BlockSpec((tm, d_b), lambda i: (i, 0)),
        out_shape=jax.ShapeDtypeStruct((m, d_b), BF16),
        compiler_params=_cparams("arbitrary"),
        name="sgu_chunked",
    )(proj, proj, lp["sgu_ln_g"], lp["sgu_ln_b"], lp["sgu_w"], lp["sgu_b_full"])


def _sgu_first_kernel(pu_ref, pv_ref, lg_ref, lb_ref, wd_ref, bd_ref, y_ref, v_ref):
    u = _gelu(pu_ref[...])
    v = _layer_norm_rows(_gelu(pv_ref[...]), lg_ref[...], lb_ref[...], LN_EPS)
    v_ref[...] = v
    vb = v.astype(BF16).astype(F32)
    wd = wd_ref[...].astype(BF16).astype(F32)
    y_ref[...] = (u * (vb * wd + bd_ref[...])).astype(y_ref.dtype)


def _sgu_single(proj, lp, d_b, u_blk):
    m = proj.shape[0]
    vec = pl.BlockSpec((1, d_b), lambda i: (0, 0))
    return pl.pallas_call(
        _sgu_first_kernel,
        grid=(1,),
        in_specs=[pl.BlockSpec((m, d_b), lambda i: (0, u_blk)),
                  pl.BlockSpec((m, d_b), lambda i: (0, u_blk + 1)), vec, vec, vec, vec],
        out_specs=[pl.BlockSpec((m, d_b), lambda i: (0, 0))] * 2,
        out_shape=[jax.ShapeDtypeStruct((m, d_b), BF16), jax.ShapeDtypeStruct((m, d_b), F32)],
        compiler_params=_cparams("arbitrary"),
        name="sgu_single",
    )(proj, proj, lp["sgu_ln_g"], lp["sgu_ln_b"], lp["sgu_w_first"], lp["sgu_b_first"])


def _merge_kernel(ya_ref, g_ref, yb_ref, ga_ref, gb_ref, wa_ref, wb_ref, o_ref, a_scr):
    @pl.when(pl.program_id(1) == 0)
    def _():
        a_scr[...] = (ya_ref[...] * g_ref[...]).astype(BF16)

    ta = jnp.dot(a_scr[...], wa_ref[...], preferred_element_type=F32)
    tb = jnp.dot(yb_ref[...], wb_ref[...], preferred_element_type=F32)
    o_ref[...] = (jax.nn.sigmoid(ga_ref[...]) * ta
                  + jax.nn.sigmoid(gb_ref[...]) * tb).astype(o_ref.dtype)


def _merge(ya, g, yb, proj, wa, wb, d, tm, tn):
    m = ya.shape[0]
    d_b = yb.shape[1]
    ga_blk = (3 * d) // tn
    gb_blk = (4 * d) // tn
    return pl.pallas_call(
        _merge_kernel,
        grid=(m // tm, d // tn),
        in_specs=[pl.BlockSpec((tm, d), lambda i, j: (i, 0)),
                  pl.BlockSpec((tm, d), lambda i, j: (i, 0)),
                  pl.BlockSpec((tm, d_b), lambda i, j: (i, 0)),
                  pl.BlockSpec((tm, tn), lambda i, j: (i, ga_blk + j)),
                  pl.BlockSpec((tm, tn), lambda i, j: (i, gb_blk + j)),
                  pl.BlockSpec((d, tn), lambda i, j: (0, j)),
                  pl.BlockSpec((d_b, tn), lambda i, j: (0, j))],
        out_specs=pl.BlockSpec((tm, tn), lambda i, j: (i, j)),
        out_shape=jax.ShapeDtypeStruct((m, d), BF16),
        scratch_shapes=[pltpu.VMEM((tm, d), BF16)],
        compiler_params=_cparams("arbitrary", "arbitrary"),
        name="branch_merge",
    )(ya, g, yb, proj, proj, wa, wb)


def _out_ln_kernel(m_ref, x_ref, gt_ref, sh_ref, sc_ref, wo_ref, wq_ref, g_ref, b_ref,
                   x1_ref, h2_ref, q_ref, *, alpha):
    z = jnp.dot(m_ref[...], wo_ref[...], preferred_element_type=F32)
    x1 = _layer_norm_rows(alpha * x_ref[...] + gt_ref[...] * z, g_ref[...], b_ref[...], LN_EPS)
    x1_ref[...] = x1
    h2 = (x1 * (1.0 + sc_ref[...]) + sh_ref[...]).astype(BF16)
    h2_ref[...] = h2
    q_ref[...] = jnp.dot(h2, wq_ref[...], preferred_element_type=F32).astype(q_ref.dtype)


def _out_ln(mrg, x, mod, wo, wq, ln_g, ln_b, per_row, seq_len, alpha, tm):
    m, d = x.shape
    nq = wq.shape[1]
    vec = pl.BlockSpec((1, d), lambda i: (0, 0))
    row = pl.BlockSpec((tm, d), lambda i: (i, 0))
    return pl.pallas_call(
        functools.partial(_out_ln_kernel, alpha=alpha),
        grid=(m // tm,),
        in_specs=[row, row,
                  _mod_spec(per_row, tm, d, 2, seq_len),
                  _mod_spec(per_row, tm, d, 3, seq_len),
                  _mod_spec(per_row, tm, d, 4, seq_len),
                  pl.BlockSpec((d, d), lambda i: (0, 0)),
                  pl.BlockSpec((d, nq), lambda i: (0, 0)), vec, vec],
        out_specs=[row, row, pl.BlockSpec((tm, nq), lambda i: (i, 0))],
        out_shape=[jax.ShapeDtypeStruct((m, d), F32), jax.ShapeDtypeStruct((m, d), BF16),
                   jax.ShapeDtypeStruct((m, nq), BF16)],
        compiler_params=_cparams("arbitrary"),
        name="out_proj_ln1",
    )(mrg, x, mod, mod, mod, wo, wq, ln_g, ln_b)


_PAIRS = [(i, j) for i in range(TOPK) for j in range(TOPK) if (i + 1) * (j + 1) <= TOPK]
_NPAIR_PAD = -(-len(_PAIRS) // SUBLANES) * SUBLANES


def _take_top(work, rounds, break_ties, want_rank=True):
    rows = work.shape[0]
    iota = lax.broadcasted_iota(jnp.int32, work.shape, 0).astype(F32)
    rank = jnp.full(work.shape, float(rounds), F32) if want_rank else None
    vals = []
    for r in range(rounds):
        mx = jnp.max(work, axis=0, keepdims=True)
        hit = work == mx
        if break_ties:
            first = jnp.min(jnp.where(hit, iota, float(rows)), axis=0, keepdims=True)
            hit = iota == first
        if want_rank:
            rank = jnp.where(hit, float(r), rank)
        work = jnp.where(hit, -jnp.inf, work)
        vals.append(mx)
    return vals, rank


def _exactly_k(selected, k):
    n = jnp.sum(jnp.where(selected, 1.0, 0.0), axis=0, keepdims=True)
    return jnp.where(n == float(k), 1.0, 0.0)


def _head_select(s1, s2, cnt_mat, cand_scr, break_ties):
    a1, rank1 = _take_top(s1, TOPK, break_ties, want_rank=break_ties)
    a2, rank2 = _take_top(s2, TOPK, break_ties)
    for pos, (i, j) in enumerate(_PAIRS):
        cand_scr[pos:pos + 1, :] = a1[i] + a2[j]
    cand = cand_scr[...]
    _, rank_c = _take_top(cand, TOPK, break_ties)
    sel = jnp.where(rank_c < float(TOPK), 1.0, 0.0)
    top = a1[0] + a2[0]
    z = jnp.sum(sel * jnp.exp(cand - top), axis=0, keepdims=True)
    cnt = jnp.dot(cnt_mat, sel.astype(BF16), preferred_element_type=F32)
    n1 = jnp.zeros(s1.shape, F32)
    for i in range(TOPK):
        is_i = rank1 == float(i) if break_ties else s1 == a1[i]
        n1 = jnp.where(is_i, cnt[i:i + 1, :], n1)
    sel1 = rank1 < float(TOPK) if break_ties else s1 >= a1[TOPK - 1]
    sel2 = rank2 < float(TOPK)
    p1 = jnp.where(sel1, jnp.exp(s1 - a1[0]), 0.0)
    p2 = jnp.where(sel2, jnp.exp(s2 - a2[0]), 0.0) / z
    clean = _exactly_k(sel1, TOPK) * _exactly_k(sel2, TOPK) * _exactly_k(rank_c < float(TOPK), TOPK)
    return n1, p1, rank2, p2, clean


def _peer_topk_kernel(q_ref, keys_ref, cnt_ref, n1_ref, p1_ref, r2_ref, p2_ref, cand_a, cand_b):
    nt = (((1,), (1,)), ((), ()))
    cand_a[...] = jnp.full(cand_a.shape, -jnp.inf, F32)
    cand_b[...] = jnp.full(cand_b.shape, -jnp.inf, F32)

    def scores(h):
        base = pl.multiple_of(h * 2 * DK_HALF, 2 * DK_HALF)
        q1 = q_ref[:, pl.ds(base, DK_HALF)]
        q2 = q_ref[:, pl.ds(base + DK_HALF, DK_HALF)]
        s1 = lax.dot_general(keys_ref[h, 0].astype(BF16), q1, nt, preferred_element_type=F32)
        s2 = lax.dot_general(keys_ref[h, 1].astype(BF16), q2, nt, preferred_element_type=F32)
        return s1, s2

    def write(h, res):
        n1_ref[h], p1_ref[h] = res[:2]
        r2_ref[h] = res[2].astype(r2_ref.dtype)
        p2_ref[h] = res[3].astype(p2_ref.dtype)

    def head_pair(hp, carry):
        heads = (2 * hp, 2 * hp + 1)
        scr = (cand_a, cand_b)
        sc = [scores(h) for h in heads]
        res = [_head_select(*sc[n], cnt_ref[...], scr[n], False) for n in range(2)]
        for n in range(2):
            write(heads[n], res[n])
        for n in range(2):
            @pl.when(jnp.min(res[n][4]) < 0.5)
            def _():
                write(heads[n], _head_select(*sc[n], cnt_ref[...], scr[n], True))
        return carry

    lax.fori_loop(0, PEER_HEADS // 2, head_pair, 0)


def _pair_count_matrix():
    return jnp.array([[1.0 if i == r else 0.0 for (i, _) in _PAIRS]
                      + [0.0] * (_NPAIR_PAD - len(_PAIRS)) for r in range(TOPK)], BF16)


def _peer_topk(q, keys, tt):
    m = q.shape[0]
    cnt = _pair_count_matrix()
    out = pl.BlockSpec((PEER_HEADS, N_KEYS, tt), lambda i: (0, 0, i))
    return pl.pallas_call(
        _peer_topk_kernel,
        grid=(m // tt,),
        in_specs=[pl.BlockSpec((tt, q.shape[1]), lambda i: (i, 0)),
                  pl.BlockSpec(keys.shape, lambda i: (0, 0, 0, 0)),
                  pl.BlockSpec(cnt.shape, lambda i: (0, 0))],
        out_specs=[out] * 4,
        out_shape=[jax.ShapeDtypeStruct((PEER_HEADS, N_KEYS, m), dt)
                   for dt in (F32, F32, BF16, BF16)],
        scratch_shapes=[pltpu.VMEM((_NPAIR_PAD, tt), F32)] * 2,
        compiler_params=_cparams("arbitrary"),
        name="peer_topk",
    )(q, keys, cnt)


def _peer_dense_kernel(h_ref, u_ref, vt_ref, n1_ref, p1_ref, r2_ref, p2_ref, o_ref, w_scr, g_scr,
                       *, et, n_et):
    n = pl.program_id(0)
    cur = jnp.minimum(n, pl.num_programs(0) - 2)
    e_cur = cur % n_et
    e_prev = jnp.maximum(n - 1, 0) % n_et
    nt = (((1,), (1,)), ((), ()))

    @pl.when(n == 0)
    def _():
        w_scr[1] = jnp.zeros(w_scr.shape[1:], BF16)

    @pl.when(e_prev == 0)
    def _():
        o_ref[...] = jnp.zeros(o_ref.shape, F32)

    zero = jnp.zeros((), BF16)
    for j in range(et // N_KEYS):
        e1 = e_cur * (et // N_KEYS) + j
        gate = jnp.zeros((N_KEYS, g_scr.shape[1]), BF16)
        for h in range(PEER_HEADS):
            n1 = n1_ref[h, pl.ds(e1, 1), :].astype(BF16)
            p1 = p1_ref[h, pl.ds(e1, 1), :].astype(BF16)
            gate = gate + jnp.where(r2_ref[h] < n1, p2_ref[h], zero) * p1
        g_scr[j * N_KEYS:(j + 1) * N_KEYS, :] = gate
    s = lax.dot_general(u_ref[...], h_ref[...], nt, preferred_element_type=F32)
    o_ref[...] += jnp.dot(vt_ref[...], w_scr[(n + 1) % 2], preferred_element_type=F32)
    w_scr[n % 2] = _gelu(s).astype(BF16) * g_scr[...]


def _peer_dense(h2, u_tab, vt_tab, n1, p1, r2, p2, tt, et):
    m, d = h2.shape
    n_et = u_tab.shape[0] // et
    steps = (m // tt) * n_et

    def cur(fn):
        return lambda n: fn(jnp.minimum(n, steps - 1))

    def prev(fn):
        return lambda n: fn(jnp.maximum(n - 1, 0))

    once = pl.Buffered(1)
    sel = pl.BlockSpec((PEER_HEADS, N_KEYS, tt), cur(lambda c: (0, 0, c // n_et)),
                       pipeline_mode=once)
    return pl.pallas_call(
        functools.partial(_peer_dense_kernel, et=et, n_et=n_et),
        grid=(steps + 1,),
        in_specs=[pl.BlockSpec((tt, d), cur(lambda c: (c // n_et, 0))),
                  pl.BlockSpec((et, d), cur(lambda c: (c % n_et, 0))),
                  pl.BlockSpec((d, et), prev(lambda c: (0, c % n_et))),
                  sel, sel, sel, sel],
        out_specs=pl.BlockSpec((d, tt), prev(lambda c: (0, c // n_et))),
        out_shape=jax.ShapeDtypeStruct((d, m), F32),
        scratch_shapes=[pltpu.VMEM((2, et, tt), BF16), pltpu.VMEM((et, tt), BF16)],
        compiler_params=_cparams("arbitrary"),
        name="peer_dense",
    )(h2, u_tab, vt_tab, n1, p1, r2, p2)


def _final_kernel(x1_ref, ft_ref, gt_ref, g_ref, b_ref, o_ref, *, alpha):
    f = ft_ref[...].T
    o_ref[...] = _layer_norm_rows(alpha * x1_ref[...] + gt_ref[...] * f,
                                  g_ref[...], b_ref[...], LN_EPS)


def _final_ln(x1, ft, first_token, mod, ln_g, ln_b, per_row, seq_len, alpha, tm):
    m, d = x1.shape
    col0 = first_token // tm
    row = pl.BlockSpec((tm, d), lambda i: (i, 0))
    vec = pl.BlockSpec((1, d), lambda i: (0, 0))
    return pl.pallas_call(
        functools.partial(_final_kernel, alpha=alpha),
        grid=(m // tm,),
        in_specs=[row, pl.BlockSpec((d, tm), lambda i: (0, col0 + i)),
                  _mod_spec(per_row, tm, d, 5, seq_len), vec, vec],
        out_specs=row,
        out_shape=jax.ShapeDtypeStruct((m, d), F32),
        compiler_params=_cparams("arbitrary"),
        name="final_ln2",
    )(x1, ft, mod, ln_g, ln_b)


def _pack_w_in(w_in, d, d_b, heads):
    o_small = 3 * d
    o_u = o_small + SMALL
    o_v = o_u + d_b
    o_ga = o_v + d_b
    o_gb = o_ga + d
    wt = w_in.T
    small = jnp.pad(wt[o_small:o_u], ((0, SMALL_PAD - SMALL), (0, 0)))
    rkv = _chan_swap(wt[:o_small].reshape(3, d, -1), heads, axis=1).reshape(o_small, -1)
    return jnp.concatenate([rkv, wt[o_ga:o_gb], wt[o_gb:o_gb + d], wt[o_u:o_v], wt[o_v:o_ga],
                            small], axis=0).astype(BF16)


def _dense_token_tile(m):
    for tt in (768, 512, 256):
        if (-m) % tt <= m // 20:
            return tt
    return LANES


def _chan_swap(x, heads, axis=-1):
    axis = axis % x.ndim
    shp = x.shape
    x = x.reshape(shp[:axis] + (heads, HEAD_A) + shp[axis + 1:])
    return jnp.swapaxes(x, axis, axis + 1).reshape(shp)


def _chan_unswap(x, heads, axis=-1):
    axis = axis % x.ndim
    shp = x.shape
    x = x.reshape(shp[:axis] + (HEAD_A, heads) + shp[axis + 1:])
    return jnp.swapaxes(x, axis, axis + 1).reshape(shp)


def _split_shift(s, d, heads):
    main = _chan_swap(s[:, :3 * d].reshape(-1, 3, d), heads).reshape(-1, 3 * d)
    return main, jnp.pad(s[:, 3 * d:], ((0, 0), (0, SMALL_PAD - SMALL)))


def _join_shift(proj_rows, d, heads):
    small_off = 6 * d
    main = _chan_unswap(proj_rows[:, :3 * d].reshape(-1, 3, d), heads).reshape(-1, 3 * d)
    return jnp.concatenate([main, proj_rows[:, small_off:small_off + SMALL]], axis=1)


_SCAN_PARAMS = ("lnx_g", "lnx_b", "r_k", "k_k", "k_a")


def _rwkv_prompt(planes, bsz, heads, p, tc):
    tile = lambda x: jnp.tile(x.reshape(heads, HEAD_A).T, (1, bsz))[:, None, :]
    s0 = jnp.zeros((1, HEAD_A, HEAD_A, LANES), F32)
    y, st = _scan(planes, s0, [tile(p[n]) for n in _SCAN_PARAMS], tc)
    wkv = st[0].reshape(HEAD_A, HEAD_A, bsz, heads).transpose(2, 3, 1, 0)
    return y, wkv


def _rwkv_sample(pre, nb, heads, wkv0, p):
    to_l = lambda x: x.reshape(nb, HEAD_A, heads).transpose(1, 2, 0).reshape(HEAD_A, 1, heads * nb)
    rep = lambda x: jnp.repeat(x.reshape(heads, HEAD_A).T, nb, axis=1)[:, None, :]
    s0 = wkv0.transpose(1, 3, 2, 0)
    y, st = _scan([to_l(x) for x in pre], s0, [rep(p[n]) for n in _SCAN_PARAMS], 1)
    y = y.reshape(HEAD_A, heads, nb).transpose(2, 0, 1).reshape(nb, heads * HEAD_A)
    return y, st.transpose(3, 0, 2, 1)


def _layer(x_p, x_s, c_p, c_s, wkv_s, shift_s, p):
    bsz, t_len, d = x_p.shape
    nb = x_s.shape[0]
    heads = d // HEAD_A
    d_b = d // 2
    alpha = p["alpha"]
    mp = bsz * t_len

    assert bsz * heads == LANES and nb == LANES, "RWKV-7 recurrence fills 128 lanes per group"
    swap = functools.partial(_chan_swap, heads=heads)
    mu_main, mu_small = _split_shift(p["mu_shift"][None], d, heads)
    lp = {
        "mu_main": mu_main, "mu_small": mu_small,
        "w0": swap(p["w0"])[None], "w2": swap(p["w2"]), "a0": swap(p["a0"])[None],
        "a2": swap(p["a2"]), "g2": swap(p["g2"]),
        "sgu_ln_g": p["sgu_ln_g"][None], "sgu_ln_b": p["sgu_ln_b"][None], "sgu_w": p["sgu_w"],
        "sgu_b_full": jnp.broadcast_to(p["sgu_b"][:, :, None], p["sgu_b"].shape + (GROUP_B,)),
        "sgu_w_first": jnp.repeat(p["sgu_w"][:, 0, 0], GROUP_B)[None],
        "sgu_b_first": jnp.repeat(p["sgu_b"][:, 0], GROUP_B)[None],
    }
    ln1 = (p["ln1_g"][None], p["ln1_b"][None])
    ln2 = (p["ln2_g"][None], p["ln2_b"][None])

    c_all = jnp.concatenate([c_p, c_s], axis=0)
    mod = _modulation(jnp.pad(c_all, ((0, (-c_all.shape[0]) % SUBLANES), (0, 0))),
                      p["w_ada"], p["b_ada"])
    mod_p = mod[:bsz].reshape(bsz, 1, 6 * d)
    mod_s = mod[bsz:bsz + nb]

    w_pack = _pack_w_in(p["w_in"], d, d_b, heads)
    n_pack = w_pack.shape[0]
    tn_in = _pick(n_pack, (1280, 512))
    xp2 = x_p.reshape(mp, d)
    xs2 = x_s.reshape(nb, d)
    tm_big = _pick(t_len, (1024, 512, 256, 128))
    tm_mid = _pick(t_len, (512, 256, 128))
    tm_small = _pick(t_len, (256, 128))
    proj_p = _inproj(xp2, mod_p, w_pack, False, t_len, tm_big, tn_in)
    proj_s = _inproj(xs2, mod_s, w_pack, True, 1, nb, tn_in)

    zm = jnp.zeros((bsz, 1, 3 * d), F32)
    zs = jnp.zeros((bsz, 1, SMALL_PAD), F32)
    *planes_p, g_p = _rwkv_pre(proj_p.reshape(bsz, t_len, n_pack), zm, zs, lp, d, heads, False,
                               _pick(t_len, (32, 16, 8)))
    sm, ss = _split_shift(shift_s, d, heads)
    *pre_s, g_s = _rwkv_pre(proj_s[None], sm[None], ss[None], lp, d, heads, True, nb)
    y_planes, wkv_p = _rwkv_prompt(planes_p, bsz, heads, p, _pick(t_len, (64, 32, 16, 8)))
    ya_p = _from_planes(y_planes, bsz, heads, _pick(t_len, (128, 64, 32, 16, 8))).reshape(mp, d)
    g_p = g_p.reshape(mp, d)
    ya_s, wkv_s_new = _rwkv_sample([x[0] for x in pre_s], nb, heads, wkv_s, p)
    g_s = g_s[0]

    u_blk = (5 * d) // d_b
    yb_p = _sgu_prompt(proj_p, lp, d_b, u_blk, tm_mid)
    yb_s, vrows_s = _sgu_single(proj_s, lp, d_b, u_blk)

    wa = _chan_swap(p["w_br_a"], heads, axis=0).astype(BF16)
    wb = p["w_br_b"].astype(BF16)
    wo = p["w_o"].astype(BF16)
    wq = p["peer_wq"].astype(BF16)
    tn_mrg = _pick(d, (512, 256, 128))
    mrg_p = _merge(ya_p, g_p, yb_p, proj_p, wa, wb, d, tm_mid, tn_mrg)
    mrg_s = _merge(ya_s, g_s, yb_s, proj_s, wa, wb, d, nb, tn_mrg)
    x1_p, h2_p, q_p = _out_ln(mrg_p, xp2, mod_p, wo, wq, *ln1, False, t_len, alpha, tm_small)
    x1_s, h2_s, q_s = _out_ln(mrg_s, xs2, mod_s, wo, wq, *ln1, True, 1, alpha, nb)

    m_all = mp + nb
    tt = _dense_token_tile(m_all)
    pad_rows = (-m_all) % tt
    h2 = jnp.concatenate([h2_p, h2_s, jnp.zeros((pad_rows, d), BF16)], axis=0)
    q = jnp.concatenate([q_p, q_s, jnp.zeros((pad_rows, q_p.shape[1]), BF16)], axis=0)
    n1, p1, r2, p2 = _peer_topk(q, p["peer_keys"], LANES)
    ft = _peer_dense(h2, p["peer_u"].astype(BF16), p["peer_v"].T.astype(BF16), n1, p1, r2, p2,
                     tt, 8 * N_KEYS)

    y_p = _final_ln(x1_p, ft, 0, mod_p, *ln2, False, t_len, alpha, tm_small)
    y_s = _final_ln(x1_s, ft, mp, mod_s, *ln2, True, 1, alpha, nb)

    last = proj_p.reshape(bsz, t_len, -1)[:, -1]
    return (y_p.reshape(bsz, t_len, d), y_s.reshape(nb, 1, d), wkv_p, _join_shift(last, d, heads),
            wkv_s_new, _join_shift(proj_s, d, heads), vrows_s.reshape(nb, 1, d_b))


def kernel(x_prompt, x_sample, c_prompt, c_sample, state_wkv, state_shift, w_ada, b_ada, w_in, mu_shift, w0, w2, a0, a2, g2, k_k, k_a, r_k, lnx_g, lnx_b, sgu_ln_g, sgu_ln_b, sgu_w, sgu_b, w_br_a, w_br_b, w_o, ln1_g, ln1_b, peer_wq, peer_keys, peer_u, peer_v, ln2_g, ln2_b):
    names = ("w_ada", "b_ada", "w_in", "mu_shift", "w0", "w2", "a0", "a2", "g2", "k_k", "k_a",
             "r_k", "lnx_g", "lnx_b", "sgu_ln_g", "sgu_ln_b", "sgu_w", "sgu_b", "w_br_a",
             "w_br_b", "w_o", "ln1_g", "ln1_b", "peer_wq", "peer_keys", "peer_u", "peer_v",
             "ln2_g", "ln2_b")
    stacked = (w_ada, b_ada, w_in, mu_shift, w0, w2, a0, a2, g2, k_k, k_a, r_k, lnx_g, lnx_b,
               sgu_ln_g, sgu_ln_b, sgu_w, sgu_b, w_br_a, w_br_b, w_o, ln1_g, ln1_b, peer_wq,
               peer_keys, peer_u, peer_v, ln2_g, ln2_b)
    depth = w_ada.shape[0]
    alpha = (2 * depth) ** 0.25
    y_p, y_s = x_prompt, x_sample
    outs = [[] for _ in range(5)]
    for l in range(depth):
        p = {n: a[l] for n, a in zip(names, stacked)}
        p["alpha"] = alpha
        y_p, y_s, *state = _layer(y_p, y_s, c_prompt, c_sample, state_wkv[l], state_shift[l], p)
        for acc, s in zip(outs, state):
            acc.append(s)
    return (y_p, y_s) + tuple(jnp.stack(o) for o in outs)
```

```python
import functools

import jax
import jax.numpy as jnp
from jax import lax
from jax.experimental import pallas as pl
from jax.experimental.pallas import tpu as pltpu

F32 = jnp.float32
BF16 = jnp.bfloat16

HEAD_A = 64
R_DECAY = 96
R_AAA = 96
R_GATE = 256
SMALL = R_DECAY + R_AAA + R_GATE
SMALL_PAD = 512
CHUNK = 128
GROUP_B = 128
PEER_HEADS = 8
N_KEYS = 128
DK_HALF = 128
TOPK = 16
LN_EPS = 1e-5
GN_EPS = 64e-5

LANES = 128
SUBLANES = 8
VMEM_LIMIT = 56 * 1024 * 1024


def _cparams(*sem):
    return pltpu.CompilerParams(dimension_semantics=sem, vmem_limit_bytes=VMEM_LIMIT)


def _gelu(x):
    return jax.nn.gelu(x, approximate=True)


def _layer_norm_rows(x, g, b, eps):
    mu = jnp.mean(x, axis=-1, keepdims=True)
    xc = x - mu
    var = jnp.mean(xc * xc, axis=-1, keepdims=True)
    return xc * lax.rsqrt(var + eps) * g + b


def _pick(n, prefs):
    for p in prefs:
        if n % p == 0:
            return p
    return n


def _mod_spec(per_row, tm, d, chunk, rows_per_seq):
    if per_row:
        return pl.BlockSpec((tm, d), lambda i, *_: (i, chunk))
    return pl.BlockSpec((None, 1, d), lambda i, *_: ((i * tm) // rows_per_seq, 0, chunk))


def _mod_kernel(c_ref, w_ref, b_ref, o_ref):
    c = c_ref[...]
    s = (c * jax.nn.sigmoid(c)).astype(BF16)
    o_ref[...] = jnp.dot(s, w_ref[...].astype(BF16), preferred_element_type=F32) + b_ref[...]


def _modulation(c_all, w_ada, b_ada):
    m, d = c_all.shape
    n = w_ada.shape[1]
    tn = _pick(n, (1536, 512))
    return pl.pallas_call(
        _mod_kernel,
        grid=(n // tn,),
        in_specs=[pl.BlockSpec((m, d), lambda j: (0, 0)),
                  pl.BlockSpec((d, tn), lambda j: (0, j)),
                  pl.BlockSpec((1, tn), lambda j: (0, j))],
        out_specs=pl.BlockSpec((m, tn), lambda j: (0, j)),
        out_shape=jax.ShapeDtypeStruct((m, n), F32),
        compiler_params=_cparams("arbitrary"),
        name="adaln_mod",
    )(c_all, w_ada, b_ada.reshape(1, n))


def _inproj_kernel(x_ref, sh_ref, sc_ref, w_ref, o_ref, h_scr):
    @pl.when(pl.program_id(1) == 0)
    def _():
        h_scr[...] = (x_ref[...] * (1.0 + sc_ref[...]) + sh_ref[...]).astype(BF16)

    o_ref[...] = lax.dot_general(h_scr[...], w_ref[...], (((1,), (1,)), ((), ())),
                                 preferred_element_type=F32)


def _inproj(x, mod, w_pack, per_row, seq_len, tm, tn):
    m, d = x.shape
    n = w_pack.shape[0]
    return pl.pallas_call(
        _inproj_kernel,
        grid=(m // tm, n // tn),
        in_specs=[pl.BlockSpec((tm, d), lambda i, j: (i, 0)),
                  _mod_spec(per_row, tm, d, 0, seq_len),
                  _mod_spec(per_row, tm, d, 1, seq_len),
                  pl.BlockSpec((tn, d), lambda i, j: (j, 0))],
        out_specs=pl.BlockSpec((tm, tn), lambda i, j: (i, j)),
        out_shape=jax.ShapeDtypeStruct((m, n), F32),
        scratch_shapes=[pltpu.VMEM((tm, d), BF16)],
        compiler_params=_cparams("arbitrary", "arbitrary"),
        name="in_proj",
    )(x, mod, mod, w_pack)


def _swap_lane_groups(xs, width):
    n = len(xs)
    grp = lax.broadcasted_iota(jnp.int32, xs[0].shape, 1) // width
    moved = []
    for s in range(n):
        acc = xs[0]
        for b in range(1, n):
            acc = jnp.where(grp == (b - s) % n, xs[b], acc)
        moved.append(acc if s == 0 else pltpu.roll(acc, s * width, 1))
    outs = []
    for q in range(n):
        acc = moved[0]
        for s in range(1, n):
            acc = jnp.where(grp == (q + s) % n, moved[s], acc)
        outs.append(acc)
    return outs


def _to_planes(xs, o_ref, heads):
    per = LANES // heads
    for j in range(xs[0].shape[1] // LANES):
        outs = _swap_lane_groups([x[:, j * LANES:(j + 1) * LANES] for x in xs], heads)
        for q in range(per):
            o_ref[j * per + q] = outs[q]


def _from_planes_kernel(y_ref, o_ref, *, heads):
    per = LANES // heads
    for j in range(y_ref.shape[0] // per):
        outs = _swap_lane_groups([y_ref[j * per + q] for q in range(per)], heads)
        for b in range(per):
            o_ref[b, :, j * LANES:(j + 1) * LANES] = outs[b]


def _from_planes(y, nseq, heads, tm):
    n, t_len, _ = y.shape
    d = n * heads
    return pl.pallas_call(
        functools.partial(_from_planes_kernel, heads=heads),
        grid=(t_len // tm,),
        in_specs=[pl.BlockSpec((n, tm, LANES), lambda i: (0, i, 0))],
        out_specs=pl.BlockSpec((nseq, tm, d), lambda i: (0, i, 0)),
        out_shape=jax.ShapeDtypeStruct((nseq, t_len, d), F32),
        compiler_params=_cparams("arbitrary"),
        name="rwkv_from_planes",
    )(y)


def _rwkv_pre_kernel(pm_ref, ps_ref, qm_ref, qs_ref, stm_ref, sts_ref, mum_ref, mus_ref,
                     w0_ref, w2_ref, a0_ref, a2_ref, g2_ref,
                     r_o, w_o, k_o, v_o, a_o, g_o, *, d, heads, per_row_state):
    nseq, tm, _ = pm_ref.shape

    def shifted(p, q, st):
        if per_row_state:
            return st
        first = pl.program_id(0) == 0
        prev_row = jnp.where(first, st, q[SUBLANES - 1:SUBLANES, :])
        row = lax.broadcasted_iota(jnp.int32, p.shape, 0)
        return jnp.where(row == 0, prev_row, pltpu.roll(p, 1, 0))

    def mix(p, q, st, mu):
        return p + (shifted(p, q, st) - p) * mu

    seqs = range(nseq)
    xs_s = jnp.concatenate([mix(ps_ref[b], qs_ref[b], sts_ref[b], mus_ref[...]) for b in seqs],
                           axis=0)
    wl = xs_s[:, 0:R_DECAY]
    al = xs_s[:, R_DECAY:R_DECAY + R_AAA]
    gl = xs_s[:, R_DECAY + R_AAA:SMALL]
    wlin = w0_ref[...] + jnp.dot(jnp.tanh(wl).astype(BF16), w2_ref[...].astype(BF16),
                                 preferred_element_type=F32)
    w = -jax.nn.softplus(-wlin) - 0.5
    decay = jnp.exp(-jnp.exp(w))
    a = jax.nn.sigmoid(a0_ref[...] + jnp.dot(al.astype(BF16), a2_ref[...].astype(BF16),
                                             preferred_element_type=F32))
    g = jnp.dot(jax.nn.sigmoid(gl).astype(BF16), g2_ref[...].astype(BF16),
                preferred_element_type=F32)

    def of_seq(x, b):
        return x[b * tm:(b + 1) * tm]

    def part(n, b):
        cols = slice(n * d, (n + 1) * d)
        return mix(pm_ref[b, :, cols], qm_ref[b, :, cols], stm_ref[b, :, cols], mum_ref[:, cols])

    def emit(o_ref, xs):
        if per_row_state:
            for b in seqs:
                o_ref[b] = xs[b]
        else:
            _to_planes(xs, o_ref, heads)

    for b in seqs:
        g_o[b] = of_seq(g, b)
    emit(r_o, [part(0, b) for b in seqs])
    emit(w_o, [of_seq(decay, b) for b in seqs])
    emit(k_o, [part(1, b) for b in seqs])
    emit(v_o, [part(2, b) for b in seqs])
    emit(a_o, [of_seq(a, b) for b in seqs])


def _rwkv_pre(proj, state_main, state_small, lp, d, heads, per_row_state, tm):
    nseq, t_len, _ = proj.shape
    small_blk = (6 * d) // SMALL_PAD

    def prev_map(col):
        def f(i):
            return (0, jnp.maximum(i * (tm // SUBLANES) - 1, 0), col)
        return f

    if per_row_state:
        stm_spec = pl.BlockSpec((nseq, tm, 3 * d), lambda i: (0, i, 0))
        sts_spec = pl.BlockSpec((nseq, tm, SMALL_PAD), lambda i: (0, i, 0))
    else:
        stm_spec = pl.BlockSpec((nseq, 1, 3 * d), lambda i: (0, 0, 0))
        sts_spec = pl.BlockSpec((nseq, 1, SMALL_PAD), lambda i: (0, 0, 0))

    vec = lambda w: pl.BlockSpec((1, w), lambda i: (0, 0))
    full = lambda a: pl.BlockSpec(a.shape, lambda i: (0,) * a.ndim)
    rows = pl.BlockSpec((nseq, tm, d), lambda i: (0, i, 0))
    rows_shape = jax.ShapeDtypeStruct((nseq, t_len, d), F32)
    if per_row_state:
        vec_out, vec_shape = rows, rows_shape
    else:
        vec_out = pl.BlockSpec((HEAD_A, tm, LANES), lambda i: (0, i, 0))
        vec_shape = jax.ShapeDtypeStruct((HEAD_A, t_len, LANES), F32)
    kern = functools.partial(_rwkv_pre_kernel, d=d, heads=heads, per_row_state=per_row_state)
    in_specs = [pl.BlockSpec((nseq, tm, 3 * d), lambda i: (0, i, 0)),
                pl.BlockSpec((nseq, tm, SMALL_PAD), lambda i: (0, i, small_blk)),
                pl.BlockSpec((nseq, SUBLANES, 3 * d), prev_map(0)),
                pl.BlockSpec((nseq, SUBLANES, SMALL_PAD), prev_map(small_blk)),
                stm_spec, sts_spec, vec(3 * d), vec(SMALL_PAD), vec(d), full(lp["w2"]), vec(d),
                full(lp["a2"]), full(lp["g2"])]
    return pl.pallas_call(
        kern,
        grid=(t_len // tm,),
        in_specs=in_specs,
        out_specs=[vec_out] * 5 + [rows],
        out_shape=[vec_shape] * 5 + [rows_shape],
        compiler_params=_cparams("arbitrary"),
        name="rwkv_pre",
    )(proj, proj, proj, proj, state_main, state_small, lp["mu_main"], lp["mu_small"], lp["w0"],
      lp["w2"], lp["a0"], lp["a2"], lp["g2"])


def _scan_kernel(r_ref, w_ref, kraw_ref, v_ref, a_ref, s0_ref, lng_ref, lnb_ref, rk_ref, kkp_ref,
                 kap_ref, y_ref, st_ref, s_scr, aa_scr, bb_scr, k_ref, bon_scr, yt_scr, *, tc):
    n = HEAD_A
    c = pl.program_id(1)

    @pl.when(c == 0)
    def _():
        s_scr[...] = s0_ref[...]

    plane0 = jnp.zeros((tc, LANES), F32)

    def norm_acc(i, acc):
        kk = kraw_ref[i] * kkp_ref[i]
        return acc + kk * kk

    nrm2 = lax.fori_loop(0, n, norm_acc, plane0, unroll=SUBLANES)
    inv = 1.0 / jnp.maximum(jnp.sqrt(nrm2), 1e-12)

    def fill(i, bon):
        kraw = kraw_ref[i]
        a = a_ref[i]
        kkn = kraw * kkp_ref[i] * inv
        aa_scr[i] = -kkn
        bb_scr[i] = kkn * a
        kmod = kraw * (1.0 + (a - 1.0) * kap_ref[i])
        k_ref[i] = kmod
        return bon + r_ref[i] * kmod * rk_ref[i]

    bon_scr[...] = lax.fori_loop(0, n, fill, plane0, unroll=SUBLANES)

    zeros = jnp.zeros((n, LANES), F32)
    nblk = n // SUBLANES
    sub = lax.broadcasted_iota(jnp.int32, (SUBLANES, LANES), 0)

    def row(ref, i, t):
        return ref[i, pl.ds(t, 1), :]

    def tile_of_step(ref, t):
        blocks = []
        for m in range(nblk):
            acc = jnp.broadcast_to(row(ref, m * SUBLANES, t), (SUBLANES, LANES))
            for s in range(1, SUBLANES):
                acc = jnp.where(sub == s, row(ref, m * SUBLANES + s, t), acc)
            blocks.append(acc)
        return jnp.concatenate(blocks, axis=0)

    def sa_first(kb, acc):
        for j in range(SUBLANES):
            i = kb * SUBLANES + j
            acc = acc + s_scr[i] * row(aa_scr, i, 0)
        return acc

    sa0 = lax.fori_loop(0, nblk, sa_first, zeros)

    def step(t, sa):
        tn = jnp.minimum(t + 1, tc - 1)
        vt = tile_of_step(v_ref, t)

        def kblock(kb, carry):
            y, san = carry
            for j in range(SUBLANES):
                i = kb * SUBLANES + j
                sn = s_scr[i] * row(w_ref, i, t) + sa * row(bb_scr, i, t) + vt * row(k_ref, i, t)
                s_scr[i] = sn
                y = y + sn * row(r_ref, i, t)
                san = san + sn * row(aa_scr, i, tn)
            return y, san

        y, san = lax.fori_loop(0, nblk, kblock, (zeros, zeros), unroll=True)
        return san, y

    grp = min(SUBLANES, tc)

    def group(gb, sa):
        t0 = pl.multiple_of(gb * grp, grp)
        rows = pl.ds(t0, grp)

        def one(s, sa):
            sa, y = step(t0 + s, sa)
            yt_scr[s] = y
            return sa

        sa = lax.fori_loop(0, grp, one, sa)
        total = jnp.zeros((grp, LANES), F32)
        for i in range(n):
            acc = jnp.broadcast_to(yt_scr[0, i:i + 1, :], (grp, LANES))
            for s in range(1, grp):
                acc = jnp.where(sub[:grp] == s, yt_scr[s, i:i + 1, :], acc)
            y_ref[i, rows, :] = acc
            total = total + acc
        mu = total * (1.0 / n)
        sq = jnp.zeros((grp, LANES), F32)
        for i in range(n):
            dlt = y_ref[i, rows, :] - mu
            sq = sq + dlt * dlt
        rstd = lax.rsqrt(sq * (1.0 / n) + GN_EPS)
        bon = bon_scr[rows, :]
        for i in range(n):
            yn = (y_ref[i, rows, :] - mu) * rstd * lng_ref[i] + lnb_ref[i]
            y_ref[i, rows, :] = yn + bon * v_ref[i, rows, :]
        return sa

    lax.fori_loop(0, tc // grp, group, sa0)

    @pl.when(c == pl.num_programs(1) - 1)
    def _():
        st_ref[...] = s_scr[...]


def _scan(seq, s0, pars, tc):
    n, t_len, gl = seq[0].shape
    g = gl // LANES
    step_spec = pl.BlockSpec((n, tc, LANES), lambda gi, c: (0, c, gi))
    par_spec = pl.BlockSpec((n, 1, LANES), lambda gi, c: (0, 0, gi))
    st_spec = pl.BlockSpec((None, n, n, LANES), lambda gi, c: (gi, 0, 0, 0))
    return pl.pallas_call(
        functools.partial(_scan_kernel, tc=tc),
        grid=(g, t_len // tc),
        in_specs=[step_spec] * 5 + [st_spec] + [par_spec] * 5,
        out_specs=[step_spec, st_spec],
        out_shape=[jax.ShapeDtypeStruct((n, t_len, gl), F32),
                   jax.ShapeDtypeStruct((g, n, n, LANES), F32)],
        scratch_shapes=[pltpu.VMEM((n, n, LANES), F32),
                        pltpu.VMEM((n, tc, LANES), F32),
                        pltpu.VMEM((n, tc, LANES), F32),
                        pltpu.VMEM((n, tc, LANES), F32),
                        pltpu.VMEM((tc, LANES), F32),
                        pltpu.VMEM((min(SUBLANES, tc), n, LANES), F32)],
        compiler_params=_cparams("arbitrary", "arbitrary"),
        name="rwkv_scan",
    )(*seq, s0, *pars)


def _sgu_kernel(pu_ref, pv_ref, lg_ref, lb_ref, w_ref, b_ref, y_ref, *, chunks):
    ng = w_ref.shape[0]
    row = lax.broadcasted_iota(jnp.int32, (CHUNK, CHUNK), 0)
    col = lax.broadcasted_iota(jnp.int32, (CHUNK, CHUNK), 1)
    causal = col <= row
    for ci in range(chunks):
        rows = slice(ci * CHUNK, (ci + 1) * CHUNK)
        u = _gelu(pu_ref[rows, :])
        v = _layer_norm_rows(_gelu(pv_ref[rows, :]), lg_ref[...], lb_ref[...], LN_EPS)
        for g in range(ng):
            cols = slice(g * GROUP_B, (g + 1) * GROUP_B)
            ws = jnp.where(causal, w_ref[g], 0.0).astype(BF16)
            mixed = jnp.dot(ws, v[:, cols].astype(BF16), preferred_element_type=F32) + b_ref[g]
            y_ref[rows, cols] = (u[:, cols] * mixed).astype(y_ref.dtype)


def _sgu_prompt(proj, lp, d_b, u_blk, tm):
    m = proj.shape[0]
    ng = d_b // GROUP_B
    return pl.pallas_call(
        functools.partial(_sgu_kernel, chunks=tm // CHUNK),
        grid=(m // tm,),
        in_specs=[pl.BlockSpec((tm, d_b), lambda i: (i, u_blk)),
                  pl.BlockSpec((tm, d_b), lambda i: (i, u_blk + 1)),
                  pl.BlockSpec((1, d_b), lambda i: (0, 0)),
                  pl.BlockSpec((1, d_b), lambda i: (0, 0)),
                  pl.BlockSpec((ng, CHUNK, CHUNK), lambda i: (0, 0, 0)),
                  pl.BlockSpec((ng, CHUNK, GROUP_B), lambda i: (0, 0, 0))],
        out_specs=pl.BlockSpec((tm, d_b), lambda i: (i, 0)),
        out_shape=jax.ShapeDtypeStruct((m, d_b), BF16),
        compiler_params=_cparams("arbitrary"),
        name="sgu_chunked",
    )(proj, proj, lp["sgu_ln_g"], lp["sgu_ln_b"], lp["sgu_w"], lp["sgu_b_full"])


def _sgu_first_kernel(pu_ref, pv_ref, lg_ref, lb_ref, wd_ref, bd_ref, y_ref, v_ref):
    u = _gelu(pu_ref[...])
    v = _layer_norm_rows(_gelu(pv_ref[...]), lg_ref[...], lb_ref[...], LN_EPS)
    v_ref[...] = v
    vb = v.astype(BF16).astype(F32)
    wd = wd_ref[...].astype(BF16).astype(F32)
    y_ref[...] = (u * (vb * wd + bd_ref[...])).astype(y_ref.dtype)


def _sgu_single(proj, lp, d_b, u_blk):
    m = proj.shape[0]
    vec = pl.BlockSpec((1, d_b), lambda i: (0, 0))
    return pl.pallas_call(
        _sgu_first_kernel,
        grid=(1,),
        in_specs=[pl.BlockSpec((m, d_b), lambda i: (0, u_blk)),
                  pl.BlockSpec((m, d_b), lambda i: (0, u_blk + 1)), vec, vec, vec, vec],
        out_specs=[pl.BlockSpec((m, d_b), lambda i: (0, 0))] * 2,
        out_shape=[jax.ShapeDtypeStruct((m, d_b), BF16), jax.ShapeDtypeStruct((m, d_b), F32)],
        compiler_params=_cparams("arbitrary"),
        name="sgu_single",
    )(proj, proj, lp["sgu_ln_g"], lp["sgu_ln_b"], lp["sgu_w_first"], lp["sgu_b_first"])


def _merge_kernel(ya_ref, g_ref, yb_ref, ga_ref, gb_ref, wa_ref, wb_ref, o_ref, a_scr):
    @pl.when(pl.program_id(1) == 0)
    def _():
        a_scr[...] = (ya_ref[...] * g_ref[...]).astype(BF16)

    ta = jnp.dot(a_scr[...], wa_ref[...], preferred_element_type=F32)
    tb = jnp.dot(yb_ref[...], wb_ref[...], preferred_element_type=F32)
    o_ref[...] = (jax.nn.sigmoid(ga_ref[...]) * ta
                  + jax.nn.sigmoid(gb_ref[...]) * tb).astype(o_ref.dtype)


def _merge(ya, g, yb, proj, wa, wb, d, tm, tn):
    m = ya.shape[0]
    d_b = yb.shape[1]
    ga_blk = (3 * d) // tn
    gb_blk = (4 * d) // tn
    return pl.pallas_call(
        _merge_kernel,
        grid=(m // tm, d // tn),
        in_specs=[pl.BlockSpec((tm, d), lambda i, j: (i, 0)),
                  pl.BlockSpec((tm, d), lambda i, j: (i, 0)),
                  pl.BlockSpec((tm, d_b), lambda i, j: (i, 0)),
                  pl.BlockSpec((tm, tn), lambda i, j: (i, ga_blk + j)),
                  pl.BlockSpec((tm, tn), lambda i, j: (i, gb_blk + j)),
                  pl.BlockSpec((d, tn), lambda i, j: (0, j)),
                  pl.BlockSpec((d_b, tn), lambda i, j: (0, j))],
        out_specs=pl.BlockSpec((tm, tn), lambda i, j: (i, j)),
        out_shape=jax.ShapeDtypeStruct((m, d), BF16),
        scratch_shapes=[pltpu.VMEM((tm, d), BF16)],
        compiler_params=_cparams("arbitrary", "arbitrary"),
        name="branch_merge",
    )(ya, g, yb, proj, proj, wa, wb)


def _out_ln_kernel(m_ref, x_ref, gt_ref, sh_ref, sc_ref, wo_ref, wq_ref, g_ref, b_ref,
                   x1_ref, h2_ref, q_ref, *, alpha):
    z = jnp.dot(m_ref[...], wo_ref[...], preferred_element_type=F32)
    x1 = _layer_norm_rows(alpha * x_ref[...] + gt_ref[...] * z, g_ref[...], b_ref[...], LN_EPS)
    x1_ref[...] = x1
    h2 = (x1 * (1.0 + sc_ref[...]) + sh_ref[...]).astype(BF16)
    h2_ref[...] = h2
    q_ref[...] = jnp.dot(h2, wq_ref[...], preferred_element_type=F32).astype(q_ref.dtype)


def _out_ln(mrg, x, mod, wo, wq, ln_g, ln_b, per_row, seq_len, alpha, tm):
    m, d = x.shape
    nq = wq.shape[1]
    vec = pl.BlockSpec((1, d), lambda i: (0, 0))
    row = pl.BlockSpec((tm, d), lambda i: (i, 0))
    return pl.pallas_call(
        functools.partial(_out_ln_kernel, alpha=alpha),
        grid=(m // tm,),
        in_specs=[row, row,
                  _mod_spec(per_row, tm, d, 2, seq_len),
                  _mod_spec(per_row, tm, d, 3, seq_len),
                  _mod_spec(per_row, tm, d, 4, seq_len),
                  pl.BlockSpec((d, d), lambda i: (0, 0)),
                  pl.BlockSpec((d, nq), lambda i: (0, 0)), vec, vec],
        out_specs=[row, row, pl.BlockSpec((tm, nq), lambda i: (i, 0))],
        out_shape=[jax.ShapeDtypeStruct((m, d), F32), jax.ShapeDtypeStruct((m, d), BF16),
                   jax.ShapeDtypeStruct((m, nq), BF16)],
        compiler_params=_cparams("arbitrary"),
        name="out_proj_ln1",
    )(mrg, x, mod, mod, mod, wo, wq, ln_g, ln_b)


_PAIRS = [(i, j) for i in range(TOPK) for j in range(TOPK) if (i + 1) * (j + 1) <= TOPK]
_NPAIR_PAD = -(-len(_PAIRS) // SUBLANES) * SUBLANES


def _take_top(work, rounds, break_ties, want_rank=True):
    rows = work.shape[0]
    iota = lax.broadcasted_iota(jnp.int32, work.shape, 0).astype(F32)
    rank = jnp.full(work.shape, float(rounds), F32) if want_rank else None
    vals = []
    for r in range(rounds):
        mx = jnp.max(work, axis=0, keepdims=True)
        hit = work == mx
        if break_ties:
            first = jnp.min(jnp.where(hit, iota, float(rows)), axis=0, keepdims=True)
            hit = iota == first
        if want_rank:
            rank = jnp.where(hit, float(r), rank)
        work = jnp.where(hit, -jnp.inf, work)
        vals.append(mx)
    return vals, rank


def _exactly_k(selected, k):
    n = jnp.sum(jnp.where(selected, 1.0, 0.0), axis=0, keepdims=True)
    return jnp.where(n == float(k), 1.0, 0.0)


def _head_select(s1, s2, cnt_mat, cand_scr, break_ties):
    a1, rank1 = _take_top(s1, TOPK, break_ties, want_rank=break_ties)
    a2, rank2 = _take_top(s2, TOPK, break_ties)
    for pos, (i, j) in enumerate(_PAIRS):
        cand_scr[pos:pos + 1, :] = a1[i] + a2[j]
    cand = cand_scr[...]
    _, rank_c = _take_top(cand, TOPK, break_ties)
    sel = jnp.where(rank_c < float(TOPK), 1.0, 0.0)
    top = a1[0] + a2[0]
    z = jnp.sum(sel * jnp.exp(cand - top), axis=0, keepdims=True)
    cnt = jnp.dot(cnt_mat, sel.astype(BF16), preferred_element_type=F32)
    n1 = jnp.zeros(s1.shape, F32)
    for i in range(TOPK):
        is_i = rank1 == float(i) if break_ties else s1 == a1[i]
        n1 = jnp.where(is_i, cnt[i:i + 1, :], n1)
    sel1 = rank1 < float(TOPK) if break_ties else s1 >= a1[TOPK - 1]
    sel2 = rank2 < float(TOPK)
    p1 = jnp.where(sel1, jnp.exp(s1 - a1[0]), 0.0)
    p2 = jnp.where(sel2, jnp.exp(s2 - a2[0]), 0.0) / z
    clean = _exactly_k(sel1, TOPK) * _exactly_k(sel2, TOPK) * _exactly_k(rank_c < float(TOPK), TOPK)
    return n1, p1, rank2, p2, clean


def _peer_topk_kernel(q_ref, keys_ref, cnt_ref, n1_ref, p1_ref, r2_ref, p2_ref, cand_a, cand_b):
    nt = (((1,), (1,)), ((), ()))
    cand_a[...] = jnp.full(cand_a.shape, -jnp.inf, F32)
    cand_b[...] = jnp.full(cand_b.shape, -jnp.inf, F32)

    def scores(h):
        base = pl.multiple_of(h * 2 * DK_HALF, 2 * DK_HALF)
        q1 = q_ref[:, pl.ds(base, DK_HALF)]
        q2 = q_ref[:, pl.ds(base + DK_HALF, DK_HALF)]
        s1 = lax.dot_general(keys_ref[h, 0].astype(BF16), q1, nt, preferred_element_type=F32)
        s2 = lax.dot_general(keys_ref[h, 1].astype(BF16), q2, nt, preferred_element_type=F32)
        return s1, s2

    def write(h, res):
        n1_ref[h], p1_ref[h] = res[:2]
        r2_ref[h] = res[2].astype(r2_ref.dtype)
        p2_ref[h] = res[3].astype(p2_ref.dtype)

    def head_pair(hp, carry):
        heads = (2 * hp, 2 * hp + 1)
        scr = (cand_a, cand_b)
        sc = [scores(h) for h in heads]
        res = [_head_select(*sc[n], cnt_ref[...], scr[n], False) for n in range(2)]
        for n in range(2):
            write(heads[n], res[n])
        for n in range(2):
            @pl.when(jnp.min(res[n][4]) < 0.5)
            def _():
                write(heads[n], _head_select(*sc[n], cnt_ref[...], scr[n], True))
        return carry

    lax.fori_loop(0, PEER_HEADS // 2, head_pair, 0)


def _pair_count_matrix():
    return jnp.array([[1.0 if i == r else 0.0 for (i, _) in _PAIRS]
                      + [0.0] * (_NPAIR_PAD - len(_PAIRS)) for r in range(TOPK)], BF16)


def _peer_topk(q, keys, tt):
    m = q.shape[0]
    cnt = _pair_count_matrix()
    out = pl.BlockSpec((PEER_HEADS, N_KEYS, tt), lambda i: (0, 0, i))
    return pl.pallas_call(
        _peer_topk_kernel,
        grid=(m // tt,),
        in_specs=[pl.BlockSpec((tt, q.shape[1]), lambda i: (i, 0)),
                  pl.BlockSpec(keys.shape, lambda i: (0, 0, 0, 0)),
                  pl.BlockSpec(cnt.shape, lambda i: (0, 0))],
        out_specs=[out] * 4,
        out_shape=[jax.ShapeDtypeStruct((PEER_HEADS, N_KEYS, m), dt)
                   for dt in (F32, F32, BF16, BF16)],
        scratch_shapes=[pltpu.VMEM((_NPAIR_PAD, tt), F32)] * 2,
        compiler_params=_cparams("arbitrary"),
        name="peer_topk",
    )(q, keys, cnt)


def _peer_dense_kernel(h_ref, u_ref, vt_ref, n1_ref, p1_ref, r2_ref, p2_ref, o_ref, w_scr, g_scr,
                       *, et, n_et):
    n = pl.program_id(0)
    cur = jnp.minimum(n, pl.num_programs(0) - 2)
    e_cur = cur % n_et
    e_prev = jnp.maximum(n - 1, 0) % n_et
    nt = (((1,), (1,)), ((), ()))

    @pl.when(n == 0)
    def _():
        w_scr[1] = jnp.zeros(w_scr.shape[1:], BF16)

    @pl.when(e_prev == 0)
    def _():
        o_ref[...] = jnp.zeros(o_ref.shape, F32)

    zero = jnp.zeros((), BF16)
    for j in range(et // N_KEYS):
        e1 = e_cur * (et // N_KEYS) + j
        gate = jnp.zeros((N_KEYS, g_scr.shape[1]), BF16)
        for h in range(PEER_HEADS):
            n1 = n1_ref[h, pl.ds(e1, 1), :].astype(BF16)
            p1 = p1_ref[h, pl.ds(e1, 1), :].astype(BF16)
            gate = gate + jnp.where(r2_ref[h] < n1, p2_ref[h], zero) * p1
        g_scr[j * N_KEYS:(j + 1) * N_KEYS, :] = gate
    s = lax.dot_general(u_ref[...], h_ref[...], nt, preferred_element_type=F32)
    o_ref[...] += jnp.dot(vt_ref[...], w_scr[(n + 1) % 2], preferred_element_type=F32)
    w_scr[n % 2] = _gelu(s).astype(BF16) * g_scr[...]


def _peer_dense(h2, u_tab, vt_tab, n1, p1, r2, p2, tt, et):
    m, d = h2.shape
    n_et = u_tab.shape[0] // et
    steps = (m // tt) * n_et

    def cur(fn):
        return lambda n: fn(jnp.minimum(n, steps - 1))

    def prev(fn):
        return lambda n: fn(jnp.maximum(n - 1, 0))

    once = pl.Buffered(1)
    sel = pl.BlockSpec((PEER_HEADS, N_KEYS, tt), cur(lambda c: (0, 0, c // n_et)),
                       pipeline_mode=once)
    return pl.pallas_call(
        functools.partial(_peer_dense_kernel, et=et, n_et=n_et),
        grid=(steps + 1,),
        in_specs=[pl.BlockSpec((tt, d), cur(lambda c: (c // n_et, 0))),
                  pl.BlockSpec((et, d), cur(lambda c: (c % n_et, 0))),
                  pl.BlockSpec((d, et), prev(lambda c: (0, c % n_et))),
                  sel, sel, sel, sel],
        out_specs=pl.BlockSpec((d, tt), prev(lambda c: (0, c // n_et))),
        out_shape=jax.ShapeDtypeStruct((d, m), F32),
        scratch_shapes=[pltpu.VMEM((2, et, tt), BF16), pltpu.VMEM((et, tt), BF16)],
        compiler_params=_cparams("arbitrary"),
        name="peer_dense",
    )(h2, u_tab, vt_tab, n1, p1, r2, p2)


def _final_kernel(x1_ref, ft_ref, gt_ref, g_ref, b_ref, o_ref, *, alpha):
    f = ft_ref[...].T
    o_ref[...] = _layer_norm_rows(alpha * x1_ref[...] + gt_ref[...] * f,
                                  g_ref[...], b_ref[...], LN_EPS)


def _final_ln(x1, ft, first_token, mod, ln_g, ln_b, per_row, seq_len, alpha, tm):
    m, d = x1.shape
    col0 = first_token // tm
    row = pl.BlockSpec((tm, d), lambda i: (i, 0))
    vec = pl.BlockSpec((1, d), lambda i: (0, 0))
    return pl.pallas_call(
        functools.partial(_final_kernel, alpha=alpha),
        grid=(m // tm,),
        in_specs=[row, pl.BlockSpec((d, tm), lambda i: (0, col0 + i)),
                  _mod_spec(per_row, tm, d, 5, seq_len), vec, vec],
        out_specs=row,
        out_shape=jax.ShapeDtypeStruct((m, d), F32),
        compiler_params=_cparams("arbitrary"),
        name="final_ln2",
    )(x1, ft, mod, ln_g, ln_b)


def _pack_w_in(w_in, d, d_b, heads):
    o_small = 3 * d
    o_u = o_small + SMALL
    o_v = o_u + d_b
    o_ga = o_v + d_b
    o_gb = o_ga + d
    wt = w_in.T
    small = jnp.pad(wt[o_small:o_u], ((0, SMALL_PAD - SMALL), (0, 0)))
    rkv = _chan_swap(wt[:o_small].reshape(3, d, -1), heads, axis=1).reshape(o_small, -1)
    return jnp.concatenate([rkv, wt[o_ga:o_gb], wt[o_gb:o_gb + d], wt[o_u:o_v], wt[o_v:o_ga],
                            small], axis=0).astype(BF16)


def _dense_token_tile(m):
    for tt in (768, 512, 256):
        if (-m) % tt <= m // 20:
            return tt
    return LANES


def _chan_swap(x, heads, axis=-1):
    axis = axis % x.ndim
    shp = x.shape
    x = x.reshape(shp[:axis] + (heads, HEAD_A) + shp[axis + 1:])
    return jnp.swapaxes(x, axis, axis + 1).reshape(shp)


def _chan_unswap(x, heads, axis=-1):
    axis = axis % x.ndim
    shp = x.shape
    x = x.reshape(shp[:axis] + (HEAD_A, heads) + shp[axis + 1:])
    return jnp.swapaxes(x, axis, axis + 1).reshape(shp)


def _split_shift(s, d, heads):
    main = _chan_swap(s[:, :3 * d].reshape(-1, 3, d), heads).reshape(-1, 3 * d)
    return main, jnp.pad(s[:, 3 * d:], ((0, 0), (0, SMALL_PAD - SMALL)))


def _join_shift(proj_rows, d, heads):
    small_off = 6 * d
    main = _chan_unswap(proj_rows[:, :3 * d].reshape(-1, 3, d), heads).reshape(-1, 3 * d)
    return jnp.concatenate([main, proj_rows[:, small_off:small_off + SMALL]], axis=1)


_SCAN_PARAMS = ("lnx_g", "lnx_b", "r_k", "k_k", "k_a")


def _rwkv_prompt(planes, bsz, heads, p, tc):
    tile = lambda x: jnp.tile(x.reshape(heads, HEAD_A).T, (1, bsz))[:, None, :]
    s0 = jnp.zeros((1, HEAD_A, HEAD_A, LANES), F32)
    y, st = _scan(planes, s0, [tile(p[n]) for n in _SCAN_PARAMS], tc)
    wkv = st[0].reshape(HEAD_A, HEAD_A, bsz, heads).transpose(2, 3, 1, 0)
    return y, wkv


def _rwkv_sample(pre, nb, heads, wkv0, p):
    to_l = lambda x: x.reshape(nb, HEAD_A, heads).transpose(1, 2, 0).reshape(HEAD_A, 1, heads * nb)
    rep = lambda x: jnp.repeat(x.reshape(heads, HEAD_A).T, nb, axis=1)[:, None, :]
    s0 = wkv0.transpose(1, 3, 2, 0)
    y, st = _scan([to_l(x) for x in pre], s0, [rep(p[n]) for n in _SCAN_PARAMS], 1)
    y = y.reshape(HEAD_A, heads, nb).transpose(2, 0, 1).reshape(nb, heads * HEAD_A)
    return y, st.transpose(3, 0, 2, 1)


def _layer(x_p, x_s, c_p, c_s, wkv_s, shift_s, p):
    bsz, t_len, d = x_p.shape
    nb = x_s.shape[0]
    heads = d // HEAD_A
    d_b = d // 2
    alpha = p["alpha"]
    mp = bsz * t_len

    assert bsz * heads == LANES and nb == LANES, "RWKV-7 recurrence fills 128 lanes per group"
    swap = functools.partial(_chan_swap, heads=heads)
    mu_main, mu_small = _split_shift(p["mu_shift"][None], d, heads)
    lp = {
        "mu_main": mu_main, "mu_small": mu_small,
        "w0": swap(p["w0"])[None], "w2": swap(p["w2"]), "a0": swap(p["a0"])[None],
        "a2": swap(p["a2"]), "g2": swap(p["g2"]),
        "sgu_ln_g": p["sgu_ln_g"][None], "sgu_ln_b": p["sgu_ln_b"][None], "sgu_w": p["sgu_w"],
        "sgu_b_full": jnp.broadcast_to(p["sgu_b"][:, :, None], p["sgu_b"].shape + (GROUP_B,)),
        "sgu_w_first": jnp.repeat(p["sgu_w"][:, 0, 0], GROUP_B)[None],
        "sgu_b_first": jnp.repeat(p["sgu_b"][:, 0], GROUP_B)[None],
    }
    ln1 = (p["ln1_g"][None], p["ln1_b"][None])
    ln2 = (p["ln2_g"][None], p["ln2_b"][None])

    c_all = jnp.concatenate([c_p, c_s], axis=0)
    mod = _modulation(jnp.pad(c_all, ((0, (-c_all.shape[0]) % SUBLANES), (0, 0))),
                      p["w_ada"], p["b_ada"])
    mod_p = mod[:bsz].reshape(bsz, 1, 6 * d)
    mod_s = mod[bsz:bsz + nb]

    w_pack = _pack_w_in(p["w_in"], d, d_b, heads)
    n_pack = w_pack.shape[0]
    tn_in = _pick(n_pack, (1280, 512))
    xp2 = x_p.reshape(mp, d)
    xs2 = x_s.reshape(nb, d)
    tm_big = _pick(t_len, (1024, 512, 256, 128))
    tm_mid = _pick(t_len, (512, 256, 128))
    tm_small = _pick(t_len, (256, 128))
    proj_p = _inproj(xp2, mod_p, w_pack, False, t_len, tm_big, tn_in)
    proj_s = _inproj(xs2, mod_s, w_pack, True, 1, nb, tn_in)

    zm = jnp.zeros((bsz, 1, 3 * d), F32)
    zs = jnp.zeros((bsz, 1, SMALL_PAD), F32)
    *planes_p, g_p = _rwkv_pre(proj_p.reshape(bsz, t_len, n_pack), zm, zs, lp, d, heads, False,
                               _pick(t_len, (32, 16, 8)))
    sm, ss = _split_shift(shift_s, d, heads)
    *pre_s, g_s = _rwkv_pre(proj_s[None], sm[None], ss[None], lp, d, heads, True, nb)
    y_planes, wkv_p = _rwkv_prompt(planes_p, bsz, heads, p, _pick(t_len, (64, 32, 16, 8)))
    ya_p = _from_planes(y_planes, bsz, heads, _pick(t_len, (128, 64, 32, 16, 8))).reshape(mp, d)
    g_p = g_p.reshape(mp, d)
    ya_s, wkv_s_new = _rwkv_sample([x[0] for x in pre_s], nb, heads, wkv_s, p)
    g_s = g_s[0]

    u_blk = (5 * d) // d_b
    yb_p = _sgu_prompt(proj_p, lp, d_b, u_blk, tm_mid)
    yb_s, vrows_s = _sgu_single(proj_s, lp, d_b, u_blk)

    wa = _chan_swap(p["w_br_a"], heads, axis=0).astype(BF16)
    wb = p["w_br_b"].astype(BF16)
    wo = p["w_o"].astype(BF16)
    wq = p["peer_wq"].astype(BF16)
    tn_mrg = _pick(d, (1024, 512, 256, 128))
    mrg_p = _merge(ya_p, g_p, yb_p, proj_p, wa, wb, d, tm_mid, tn_mrg)
    mrg_s = _merge(ya_s, g_s, yb_s, proj_s, wa, wb, d, nb, tn_mrg)
    x1_p, h2_p, q_p = _out_ln(mrg_p, xp2, mod_p, wo, wq, *ln1, False, t_len, alpha, tm_small)
    x1_s, h2_s, q_s = _out_ln(mrg_s, xs2, mod_s, wo, wq, *ln1, True, 1, alpha, nb)

    m_all = mp + nb
    tt = _dense_token_tile(m_all)
    pad_rows = (-m_all) % tt
    h2 = jnp.concatenate([h2_p, h2_s, h2_p[:pad_rows]], axis=0)
    q = jnp.concatenate([q_p, q_s, q_p[:pad_rows]], axis=0)
    n1, p1, r2, p2 = _peer_topk(q, p["peer_keys"], LANES)
    ft = _peer_dense(h2, p["peer_u"].astype(BF16), p["peer_v"].T.astype(BF16), n1, p1, r2, p2,
                     tt, 8 * N_KEYS)

    y_p = _final_ln(x1_p, ft, 0, mod_p, *ln2, False, t_len, alpha, tm_small)
    y_s = _final_ln(x1_s, ft, mp, mod_s, *ln2, True, 1, alpha, nb)

    last = proj_p.reshape(bsz, t_len, -1)[:, -1]
    return (y_p.reshape(bsz, t_len, d), y_s.reshape(nb, 1, d), wkv_p, _join_shift(last, d, heads),
            wkv_s_new, _join_shift(proj_s, d, heads), vrows_s.reshape(nb, 1, d_b))


def kernel(x_prompt, x_sample, c_prompt, c_sample, state_wkv, state_shift, w_ada, b_ada, w_in, mu_shift, w0, w2, a0, a2, g2, k_k, k_a, r_k, lnx_g, lnx_b, sgu_ln_g, sgu_ln_b, sgu_w, sgu_b, w_br_a, w_br_b, w_o, ln1_g, ln1_b, peer_wq, peer_keys, peer_u, peer_v, ln2_g, ln2_b):
    names = ("w_ada", "b_ada", "w_in", "mu_shift", "w0", "w2", "a0", "a2", "g2", "k_k", "k_a",
             "r_k", "lnx_g", "lnx_b", "sgu_ln_g", "sgu_ln_b", "sgu_w", "sgu_b", "w_br_a",
             "w_br_b", "w_o", "ln1_g", "ln1_b", "peer_wq", "peer_keys", "peer_u", "peer_v",
             "ln2_g", "ln2_b")
    stacked = (w_ada, b_ada, w_in, mu_shift, w0, w2, a0, a2, g2, k_k, k_a, r_k, lnx_g, lnx_b,
               sgu_ln_g, sgu_ln_b, sgu_w, sgu_b, w_br_a, w_br_b, w_o, ln1_g, ln1_b, peer_wq,
               peer_keys, peer_u, peer_v, ln2_g, ln2_b)
    depth = w_ada.shape[0]
    alpha = (2 * depth) ** 0.25
    y_p, y_s = x_prompt, x_sample
    outs = [[] for _ in range(5)]
    for l in range(depth):
        p = {n: a[l] for n, a in zip(names, stacked)}
        p["alpha"] = alpha
        y_p, y_s, *state = _layer(y_p, y_s, c_prompt, c_sample, state_wkv[l], state_shift[l], p)
        for acc, s in zip(outs, state):
            acc.append(s)
    return (y_p, y_s) + tuple(jnp.stack(o) for o in outs)
```

```python
import functools

import jax
import jax.numpy as jnp
from jax import lax
from jax.experimental import pallas as pl
from jax.experimental.pallas import tpu as pltpu

F32 = jnp.float32
BF16 = jnp.bfloat16

HEAD_A = 64
R_DECAY = 96
R_AAA = 96
R_GATE = 256
SMALL = R_DECAY + R_AAA + R_GATE
SMALL_PAD = 512
CHUNK = 128
GROUP_B = 128
PEER_HEADS = 8
N_KEYS = 128
DK_HALF = 128
TOPK = 16
LN_EPS = 1e-5
GN_EPS = 64e-5

LANES = 128
SUBLANES = 8
VMEM_LIMIT = 56 * 1024 * 1024


def _cparams(*sem):
    return pltpu.CompilerParams(dimension_semantics=sem, vmem_limit_bytes=VMEM_LIMIT)


def _gelu(x):
    return jax.nn.gelu(x, approximate=True)


def _layer_norm_rows(x, g, b, eps):
    mu = jnp.mean(x, axis=-1, keepdims=True)
    xc = x - mu
    var = jnp.mean(xc * xc, axis=-1, keepdims=True)
    return xc * lax.rsqrt(var + eps) * g + b


def _pick(n, prefs):
    for p in prefs:
        if n % p == 0:
            return p
    return n


def _mod_spec(per_row, tm, d, chunk, rows_per_seq):
    if per_row:
        return pl.BlockSpec((tm, d), lambda i, *_: (i, chunk))
    return pl.BlockSpec((None, 1, d), lambda i, *_: ((i * tm) // rows_per_seq, 0, chunk))


def _mod_kernel(c_ref, w_ref, b_ref, o_ref):
    c = c_ref[...]
    s = (c * jax.nn.sigmoid(c)).astype(BF16)
    o_ref[...] = jnp.dot(s, w_ref[...].astype(BF16), preferred_element_type=F32) + b_ref[...]


def _modulation(c_all, w_ada, b_ada):
    m, d = c_all.shape
    n = w_ada.shape[1]
    tn = _pick(n, (1536, 512))
    return pl.pallas_call(
        _mod_kernel,
        grid=(n // tn,),
        in_specs=[pl.BlockSpec((m, d), lambda j: (0, 0)),
                  pl.BlockSpec((d, tn), lambda j: (0, j)),
                  pl.BlockSpec((1, tn), lambda j: (0, j))],
        out_specs=pl.BlockSpec((m, tn), lambda j: (0, j)),
        out_shape=jax.ShapeDtypeStruct((m, n), F32),
        compiler_params=_cparams("arbitrary"),
        name="adaln_mod",
    )(c_all, w_ada, b_ada.reshape(1, n))


def _inproj_kernel(x_ref, sh_ref, sc_ref, w_ref, o_ref, h_scr):
    @pl.when(pl.program_id(1) == 0)
    def _():
        h_scr[...] = (x_ref[...] * (1.0 + sc_ref[...]) + sh_ref[...]).astype(BF16)

    o_ref[...] = lax.dot_general(h_scr[...], w_ref[...], (((1,), (1,)), ((), ())),
                                 preferred_element_type=F32)


def _inproj(x, mod, w_pack, per_row, seq_len, tm, tn):
    m, d = x.shape
    n = w_pack.shape[0]
    return pl.pallas_call(
        _inproj_kernel,
        grid=(m // tm, n // tn),
        in_specs=[pl.BlockSpec((tm, d), lambda i, j: (i, 0)),
                  _mod_spec(per_row, tm, d, 0, seq_len),
                  _mod_spec(per_row, tm, d, 1, seq_len),
                  pl.BlockSpec((tn, d), lambda i, j: (j, 0))],
        out_specs=pl.BlockSpec((tm, tn), lambda i, j: (i, j)),
        out_shape=jax.ShapeDtypeStruct((m, n), F32),
        scratch_shapes=[pltpu.VMEM((tm, d), BF16)],
        compiler_params=_cparams("arbitrary", "arbitrary"),
        name="in_proj",
    )(x, mod, mod, w_pack)


def _swap_lane_groups(xs, width):
    n = len(xs)
    grp = lax.broadcasted_iota(jnp.int32, xs[0].shape, 1) // width
    moved = []
    for s in range(n):
        acc = xs[0]
        for b in range(1, n):
            acc = jnp.where(grp == (b - s) % n, xs[b], acc)
        moved.append(acc if s == 0 else pltpu.roll(acc, s * width, 1))
    outs = []
    for q in range(n):
        acc = moved[0]
        for s in range(1, n):
            acc = jnp.where(grp == (q + s) % n, moved[s], acc)
        outs.append(acc)
    return outs


def _to_planes(xs, o_ref, heads):
    per = LANES // heads
    for j in range(xs[0].shape[1] // LANES):
        outs = _swap_lane_groups([x[:, j * LANES:(j + 1) * LANES] for x in xs], heads)
        for q in range(per):
            o_ref[j * per + q] = outs[q]


def _from_planes_kernel(y_ref, o_ref, *, heads):
    per = LANES // heads
    for j in range(y_ref.shape[0] // per):
        outs = _swap_lane_groups([y_ref[j * per + q] for q in range(per)], heads)
        for b in range(per):
            o_ref[b, :, j * LANES:(j + 1) * LANES] = outs[b]


def _from_planes(y, nseq, heads, tm):
    n, t_len, _ = y.shape
    d = n * heads
    return pl.pallas_call(
        functools.partial(_from_planes_kernel, heads=heads),
        grid=(t_len // tm,),
        in_specs=[pl.BlockSpec((n, tm, LANES), lambda i: (0, i, 0))],
        out_specs=pl.BlockSpec((nseq, tm, d), lambda i: (0, i, 0)),
        out_shape=jax.ShapeDtypeStruct((nseq, t_len, d), F32),
        compiler_params=_cparams("arbitrary"),
        name="rwkv_from_planes",
    )(y)


def _rwkv_pre_kernel(pm_ref, ps_ref, qm_ref, qs_ref, stm_ref, sts_ref, mum_ref, mus_ref,
                     w0_ref, w2_ref, a0_ref, a2_ref, g2_ref,
                     r_o, w_o, k_o, v_o, a_o, g_o, *, d, heads, per_row_state):
    nseq, tm, _ = pm_ref.shape

    def shifted(p, q, st):
        if per_row_state:
            return st
        first = pl.program_id(0) == 0
        prev_row = jnp.where(first, st, q[SUBLANES - 1:SUBLANES, :])
        row = lax.broadcasted_iota(jnp.int32, p.shape, 0)
        return jnp.where(row == 0, prev_row, pltpu.roll(p, 1, 0))

    def mix(p, q, st, mu):
        return p + (shifted(p, q, st) - p) * mu

    seqs = range(nseq)
    xs_s = jnp.concatenate([mix(ps_ref[b], qs_ref[b], sts_ref[b], mus_ref[...]) for b in seqs],
                           axis=0)
    wl = xs_s[:, 0:R_DECAY]
    al = xs_s[:, R_DECAY:R_DECAY + R_AAA]
    gl = xs_s[:, R_DECAY + R_AAA:SMALL]
    wlin = w0_ref[...] + jnp.dot(jnp.tanh(wl).astype(BF16), w2_ref[...].astype(BF16),
                                 preferred_element_type=F32)
    w = -jax.nn.softplus(-wlin) - 0.5
    decay = jnp.exp(-jnp.exp(w))
    a = jax.nn.sigmoid(a0_ref[...] + jnp.dot(al.astype(BF16), a2_ref[...].astype(BF16),
                                             preferred_element_type=F32))
    g = jnp.dot(jax.nn.sigmoid(gl).astype(BF16), g2_ref[...].astype(BF16),
                preferred_element_type=F32)

    def of_seq(x, b):
        return x[b * tm:(b + 1) * tm]

    def part(n, b):
        cols = slice(n * d, (n + 1) * d)
        return mix(pm_ref[b, :, cols], qm_ref[b, :, cols], stm_ref[b, :, cols], mum_ref[:, cols])

    def emit(o_ref, xs):
        if per_row_state:
            for b in seqs:
                o_ref[b] = xs[b]
        else:
            _to_planes(xs, o_ref, heads)

    for b in seqs:
        g_o[b] = of_seq(g, b)
    emit(r_o, [part(0, b) for b in seqs])
    emit(w_o, [of_seq(decay, b) for b in seqs])
    emit(k_o, [part(1, b) for b in seqs])
    emit(v_o, [part(2, b) for b in seqs])
    emit(a_o, [of_seq(a, b) for b in seqs])


def _rwkv_pre(proj, state_main, state_small, lp, d, heads, per_row_state, tm):
    nseq, t_len, _ = proj.shape
    small_blk = (6 * d) // SMALL_PAD

    def prev_map(col):
        def f(i):
            return (0, jnp.maximum(i * (tm // SUBLANES) - 1, 0), col)
        return f

    if per_row_state:
        stm_spec = pl.BlockSpec((nseq, tm, 3 * d), lambda i: (0, i, 0))
        sts_spec = pl.BlockSpec((nseq, tm, SMALL_PAD), lambda i: (0, i, 0))
    else:
        stm_spec = pl.BlockSpec((nseq, 1, 3 * d), lambda i: (0, 0, 0))
        sts_spec = pl.BlockSpec((nseq, 1, SMALL_PAD), lambda i: (0, 0, 0))

    vec = lambda w: pl.BlockSpec((1, w), lambda i: (0, 0))
    full = lambda a: pl.BlockSpec(a.shape, lambda i: (0,) * a.ndim)
    rows = pl.BlockSpec((nseq, tm, d), lambda i: (0, i, 0))
    rows_shape = jax.ShapeDtypeStruct((nseq, t_len, d), F32)
    if per_row_state:
        vec_out, vec_shape = rows, rows_shape
    else:
        vec_out = pl.BlockSpec((HEAD_A, tm, LANES), lambda i: (0, i, 0))
        vec_shape = jax.ShapeDtypeStruct((HEAD_A, t_len, LANES), F32)
    kern = functools.partial(_rwkv_pre_kernel, d=d, heads=heads, per_row_state=per_row_state)
    in_specs = [pl.BlockSpec((nseq, tm, 3 * d), lambda i: (0, i, 0)),
                pl.BlockSpec((nseq, tm, SMALL_PAD), lambda i: (0, i, small_blk)),
                pl.BlockSpec((nseq, SUBLANES, 3 * d), prev_map(0)),
                pl.BlockSpec((nseq, SUBLANES, SMALL_PAD), prev_map(small_blk)),
                stm_spec, sts_spec, vec(3 * d), vec(SMALL_PAD), vec(d), full(lp["w2"]), vec(d),
                full(lp["a2"]), full(lp["g2"])]
    return pl.pallas_call(
        kern,
        grid=(t_len // tm,),
        in_specs=in_specs,
        out_specs=[vec_out] * 5 + [rows],
        out_shape=[vec_shape] * 5 + [rows_shape],
        compiler_params=_cparams("arbitrary"),
        name="rwkv_pre",
    )(proj, proj, proj, proj, state_main, state_small, lp["mu_main"], lp["mu_small"], lp["w0"],
      lp["w2"], lp["a0"], lp["a2"], lp["g2"])


def _scan_kernel(r_ref, w_ref, kraw_ref, v_ref, a_ref, s0_ref, lng_ref, lnb_ref, rk_ref, kkp_ref,
                 kap_ref, y_ref, st_ref, s_scr, aa_scr, bb_scr, k_ref, bon_scr, yt_scr, *, tc):
    n = HEAD_A
    c = pl.program_id(1)

    @pl.when(c == 0)
    def _():
        s_scr[...] = s0_ref[...]

    plane0 = jnp.zeros((tc, LANES), F32)

    def norm_acc(i, acc):
        kk = kraw_ref[i] * kkp_ref[i]
        return acc + kk * kk

    nrm2 = lax.fori_loop(0, n, norm_acc, plane0, unroll=SUBLANES)
    inv = 1.0 / jnp.maximum(jnp.sqrt(nrm2), 1e-12)

    def fill(i, bon):
        kraw = kraw_ref[i]
        a = a_ref[i]
        kkn = kraw * kkp_ref[i] * inv
        aa_scr[i] = -kkn
        bb_scr[i] = kkn * a
        kmod = kraw * (1.0 + (a - 1.0) * kap_ref[i])
        k_ref[i] = kmod
        return bon + r_ref[i] * kmod * rk_ref[i]

    bon_scr[...] = lax.fori_loop(0, n, fill, plane0, unroll=SUBLANES)

    zeros = jnp.zeros((n, LANES), F32)
    nblk = n // SUBLANES
    sub = lax.broadcasted_iota(jnp.int32, (SUBLANES, LANES), 0)

    def row(ref, i, t):
        return ref[i, pl.ds(t, 1), :]

    def tile_of_step(ref, t):
        blocks = []
        for m in range(nblk):
            acc = jnp.broadcast_to(row(ref, m * SUBLANES, t), (SUBLANES, LANES))
            for s in range(1, SUBLANES):
                acc = jnp.where(sub == s, row(ref, m * SUBLANES + s, t), acc)
            blocks.append(acc)
        return jnp.concatenate(blocks, axis=0)

    def sa_first(kb, acc):
        for j in range(SUBLANES):
            i = kb * SUBLANES + j
            acc = acc + s_scr[i] * row(aa_scr, i, 0)
        return acc

    sa0 = lax.fori_loop(0, nblk, sa_first, zeros)

    def step(t, sa):
        tn = jnp.minimum(t + 1, tc - 1)
        vt = tile_of_step(v_ref, t)

        def kblock(kb, carry):
            y, san = carry
            for j in range(SUBLANES):
                i = kb * SUBLANES + j
                sn = s_scr[i] * row(w_ref, i, t) + sa * row(bb_scr, i, t) + vt * row(k_ref, i, t)
                s_scr[i] = sn
                y = y + sn * row(r_ref, i, t)
                san = san + sn * row(aa_scr, i, tn)
            return y, san

        y, san = lax.fori_loop(0, nblk, kblock, (zeros, zeros), unroll=True)
        return san, y

    grp = min(SUBLANES, tc)

    def group(gb, sa):
        t0 = pl.multiple_of(gb * grp, grp)
        rows = pl.ds(t0, grp)

        def one(s, sa):
            sa, y = step(t0 + s, sa)
            yt_scr[s] = y
            return sa

        sa = lax.fori_loop(0, grp, one, sa)
        total = jnp.zeros((grp, LANES), F32)
        for i in range(n):
            acc = jnp.broadcast_to(yt_scr[0, i:i + 1, :], (grp, LANES))
            for s in range(1, grp):
                acc = jnp.where(sub[:grp] == s, yt_scr[s, i:i + 1, :], acc)
            y_ref[i, rows, :] = acc
            total = total + acc
        mu = total * (1.0 / n)
        sq = jnp.zeros((grp, LANES), F32)
        for i in range(n):
            dlt = y_ref[i, rows, :] - mu
            sq = sq + dlt * dlt
        rstd = lax.rsqrt(sq * (1.0 / n) + GN_EPS)
        bon = bon_scr[rows, :]
        for i in range(n):
            yn = (y_ref[i, rows, :] - mu) * rstd * lng_ref[i] + lnb_ref[i]
            y_ref[i, rows, :] = yn + bon * v_ref[i, rows, :]
        return sa

    lax.fori_loop(0, tc // grp, group, sa0)

    @pl.when(c == pl.num_programs(1) - 1)
    def _():
        st_ref[...] = s_scr[...]


def _scan(seq, s0, pars, tc):
    n, t_len, gl = seq[0].shape
    g = gl // LANES
    step_spec = pl.BlockSpec((n, tc, LANES), lambda gi, c: (0, c, gi))
    par_spec = pl.BlockSpec((n, 1, LANES), lambda gi, c: (0, 0, gi))
    st_spec = pl.BlockSpec((None, n, n, LANES), lambda gi, c: (gi, 0, 0, 0))
    return pl.pallas_call(
        functools.partial(_scan_kernel, tc=tc),
        grid=(g, t_len // tc),
        in_specs=[step_spec] * 5 + [st_spec] + [par_spec] * 5,
        out_specs=[step_spec, st_spec],
        out_shape=[jax.ShapeDtypeStruct((n, t_len, gl), F32),
                   jax.ShapeDtypeStruct((g, n, n, LANES), F32)],
        scratch_shapes=[pltpu.VMEM((n, n, LANES), F32),
                        pltpu.VMEM((n, tc, LANES), F32),
                        pltpu.VMEM((n, tc, LANES), F32),
                        pltpu.VMEM((n, tc, LANES), F32),
                        pltpu.VMEM((tc, LANES), F32),
                        pltpu.VMEM((min(SUBLANES, tc), n, LANES), F32)],
        compiler_params=_cparams("arbitrary", "arbitrary"),
        name="rwkv_scan",
    )(*seq, s0, *pars)


def _sgu_kernel(pu_ref, pv_ref, lg_ref, lb_ref, w_ref, b_ref, y_ref, *, chunks):
    ng = w_ref.shape[0]
    row = lax.broadcasted_iota(jnp.int32, (CHUNK, CHUNK), 0)
    col = lax.broadcasted_iota(jnp.int32, (CHUNK, CHUNK), 1)
    causal = col <= row
    for ci in range(chunks):
        rows = slice(ci * CHUNK, (ci + 1) * CHUNK)
        u = _gelu(pu_ref[rows, :])
        v = _layer_norm_rows(_gelu(pv_ref[rows, :]), lg_ref[...], lb_ref[...], LN_EPS)
        for g in range(ng):
            cols = slice(g * GROUP_B, (g + 1) * GROUP_B)
            ws = jnp.where(causal, w_ref[g], 0.0).astype(BF16)
            mixed = jnp.dot(ws, v[:, cols].astype(BF16), preferred_element_type=F32) + b_ref[g]
            y_ref[rows, cols] = (u[:, cols] * mixed).astype(y_ref.dtype)


def _sgu_prompt(proj, lp, d_b, u_blk, tm):
    m = proj.shape[0]
    ng = d_b // GROUP_B
    return pl.pallas_call(
        functools.partial(_sgu_kernel, chunks=tm // CHUNK),
        grid=(m // tm,),
        in_specs=[pl.BlockSpec((tm, d_b), lambda i: (i, u_blk)),
                  pl.BlockSpec((tm, d_b), lambda i: (i, u_blk + 1)),
                  pl.BlockSpec((1, d_b), lambda i: (0, 0)),
                  pl.BlockSpec((1, d_b), lambda i: (0, 0)),
                  pl.BlockSpec((ng, CHUNK, CHUNK), lambda i: (0, 0, 0)),
                  pl.BlockSpec((ng, CHUNK, GROUP_B), lambda i: (0, 0, 0))],
        out_specs=pl.BlockSpec((tm, d_b), lambda i: (i, 0)),
        out_shape=jax.ShapeDtypeStruct((m, d_b), BF16),
        compiler_params=_cparams("arbitrary"),
        name="sgu_chunked",
    )(proj, proj, lp["sgu_ln_g"], lp["sgu_ln_b"], lp["sgu_w"], lp["sgu_b_full"])


def _sgu_first_kernel(pu_ref, pv_ref, lg_ref, lb_ref, wd_ref, bd_ref, y_ref, v_ref):
    u = _gelu(pu_ref[...])
    v = _layer_norm_rows(_gelu(pv_ref[...]), lg_ref[...], lb_ref[...], LN_EPS)
    v_ref[...] = v
    vb = v.astype(BF16).astype(F32)
    wd = wd_ref[...].astype(BF16).astype(F32)
    y_ref[...] = (u * (vb * wd + bd_ref[...])).astype(y_ref.dtype)


def _sgu_single(proj, lp, d_b, u_blk):
    m = proj.shape[0]
    vec = pl.BlockSpec((1, d_b), lambda i: (0, 0))
    return pl.pallas_call(
        _sgu_first_kernel,
        grid=(1,),
        in_specs=[pl.BlockSpec((m, d_b), lambda i: (0, u_blk)),
                  pl.BlockSpec((m, d_b), lambda i: (0, u_blk + 1)), vec, vec, vec, vec],
        out_specs=[pl.BlockSpec((m, d_b), lambda i: (0, 0))] * 2,
        out_shape=[jax.ShapeDtypeStruct((m, d_b), BF16), jax.ShapeDtypeStruct((m, d_b), F32)],
        compiler_params=_cparams("arbitrary"),
        name="sgu_single",
    )(proj, proj, lp["sgu_ln_g"], lp["sgu_ln_b"], lp["sgu_w_first"], lp["sgu_b_first"])


def _merge_kernel(ya_ref, g_ref, yb_ref, ga_ref, gb_ref, wa_ref, wb_ref, o_ref, a_scr):
    @pl.when(pl.program_id(1) == 0)
    def _():
        a_scr[...] = (ya_ref[...] * g_ref[...]).astype(BF16)

    ta = jnp.dot(a_scr[...], wa_ref[...], preferred_element_type=F32)
    tb = jnp.dot(yb_ref[...], wb_ref[...], preferred_element_type=F32)
    o_ref[...] = (jax.nn.sigmoid(ga_ref[...]) * ta
                  + jax.nn.sigmoid(gb_ref[...]) * tb).astype(o_ref.dtype)


def _merge(ya, g, yb, proj, wa, wb, d, tm, tn):
    m = ya.shape[0]
    d_b = yb.shape[1]
    ga_blk = (3 * d) // tn
    gb_blk = (4 * d) // tn
    return pl.pallas_call(
        _merge_kernel,
        grid=(m // tm, d // tn),
        in_specs=[pl.BlockSpec((tm, d), lambda i, j: (i, 0)),
                  pl.BlockSpec((tm, d), lambda i, j: (i, 0)),
                  pl.BlockSpec((tm, d_b), lambda i, j: (i, 0)),
                  pl.BlockSpec((tm, tn), lambda i, j: (i, ga_blk + j)),
                  pl.BlockSpec((tm, tn), lambda i, j: (i, gb_blk + j)),
                  pl.BlockSpec((d, tn), lambda i, j: (0, j)),
                  pl.BlockSpec((d_b, tn), lambda i, j: (0, j))],
        out_specs=pl.BlockSpec((tm, tn), lambda i, j: (i, j)),
        out_shape=jax.ShapeDtypeStruct((m, d), BF16),
        scratch_shapes=[pltpu.VMEM((tm, d), BF16)],
        compiler_params=_cparams("arbitrary", "arbitrary"),
        name="branch_merge",
    )(ya, g, yb, proj, proj, wa, wb)


def _out_ln_kernel(m_ref, x_ref, gt_ref, sh_ref, sc_ref, wo_ref, wq_ref, g_ref, b_ref,
                   x1_ref, h2_ref, q_ref, *, alpha):
    z = jnp.dot(m_ref[...], wo_ref[...], preferred_element_type=F32)
    x1 = _layer_norm_rows(alpha * x_ref[...] + gt_ref[...] * z, g_ref[...], b_ref[...], LN_EPS)
    x1_ref[...] = x1
    h2 = (x1 * (1.0 + sc_ref[...]) + sh_ref[...]).astype(BF16)
    h2_ref[...] = h2
    q_ref[...] = jnp.dot(h2, wq_ref[...], preferred_element_type=F32).astype(q_ref.dtype)


def _out_ln(mrg, x, mod, wo, wq, ln_g, ln_b, per_row, seq_len, alpha, tm):
    m, d = x.shape
    nq = wq.shape[1]
    vec = pl.BlockSpec((1, d), lambda i: (0, 0))
    row = pl.BlockSpec((tm, d), lambda i: (i, 0))
    return pl.pallas_call(
        functools.partial(_out_ln_kernel, alpha=alpha),
        grid=(m // tm,),
        in_specs=[row, row,
                  _mod_spec(per_row, tm, d, 2, seq_len),
                  _mod_spec(per_row, tm, d, 3, seq_len),
                  _mod_spec(per_row, tm, d, 4, seq_len),
                  pl.BlockSpec((d, d), lambda i: (0, 0)),
                  pl.BlockSpec((d, nq), lambda i: (0, 0)), vec, vec],
        out_specs=[row, row, pl.BlockSpec((tm, nq), lambda i: (i, 0))],
        out_shape=[jax.ShapeDtypeStruct((m, d), F32), jax.ShapeDtypeStruct((m, d), BF16),
                   jax.ShapeDtypeStruct((m, nq), BF16)],
        compiler_params=_cparams("arbitrary"),
        name="out_proj_ln1",
    )(mrg, x, mod, mod, mod, wo, wq, ln_g, ln_b)


_PAIRS = [(i, j) for i in range(TOPK) for j in range(TOPK) if (i + 1) * (j + 1) <= TOPK]
_NPAIR_PAD = -(-len(_PAIRS) // SUBLANES) * SUBLANES


def _take_top(work, rounds, break_ties, want_rank=True):
    rows = work.shape[0]
    iota = lax.broadcasted_iota(jnp.int32, work.shape, 0).astype(F32)
    rank = jnp.full(work.shape, float(rounds), F32) if want_rank else None
    vals = []
    for r in range(rounds):
        mx = jnp.max(work, axis=0, keepdims=True)
        hit = work == mx
        if break_ties:
            first = jnp.min(jnp.where(hit, iota, float(rows)), axis=0, keepdims=True)
            hit = iota == first
        if want_rank:
            rank = jnp.where(hit, float(r), rank)
        work = jnp.where(hit, -jnp.inf, work)
        vals.append(mx)
    return vals, rank


def _exactly_k(selected, k):
    n = jnp.sum(jnp.where(selected, 1.0, 0.0), axis=0, keepdims=True)
    return jnp.where(n == float(k), 1.0, 0.0)


def _head_select(s1, s2, cnt_mat, cand_scr, break_ties):
    a1, rank1 = _take_top(s1, TOPK, break_ties, want_rank=break_ties)
    a2, rank2 = _take_top(s2, TOPK, break_ties)
    for pos, (i, j) in enumerate(_PAIRS):
        cand_scr[pos:pos + 1, :] = a1[i] + a2[j]
    cand = cand_scr[...]
    _, rank_c = _take_top(cand, TOPK, break_ties)
    sel = jnp.where(rank_c < float(TOPK), 1.0, 0.0)
    top = a1[0] + a2[0]
    z = jnp.sum(sel * jnp.exp(cand - top), axis=0, keepdims=True)
    cnt = jnp.dot(cnt_mat, sel.astype(BF16), preferred_element_type=F32)
    n1 = jnp.zeros(s1.shape, F32)
    for i in range(TOPK):
        is_i = rank1 == float(i) if break_ties else s1 == a1[i]
        n1 = jnp.where(is_i, cnt[i:i + 1, :], n1)
    sel1 = rank1 < float(TOPK) if break_ties else s1 >= a1[TOPK - 1]
    sel2 = rank2 < float(TOPK)
    p1 = jnp.where(sel1, jnp.exp(s1 - a1[0]), 0.0)
    p2 = jnp.where(sel2, jnp.exp(s2 - a2[0]), 0.0) / z
    clean = _exactly_k(sel1, TOPK) * _exactly_k(sel2, TOPK) * _exactly_k(rank_c < float(TOPK), TOPK)
    return n1, p1, rank2, p2, clean


def _peer_topk_kernel(q_ref, keys_ref, cnt_ref, n1_ref, p1_ref, r2_ref, p2_ref, cand_a, cand_b):
    nt = (((1,), (1,)), ((), ()))
    cand_a[...] = jnp.full(cand_a.shape, -jnp.inf, F32)
    cand_b[...] = jnp.full(cand_b.shape, -jnp.inf, F32)

    def scores(h):
        base = pl.multiple_of(h * 2 * DK_HALF, 2 * DK_HALF)
        q1 = q_ref[:, pl.ds(base, DK_HALF)]
        q2 = q_ref[:, pl.ds(base + DK_HALF, DK_HALF)]
        s1 = lax.dot_general(keys_ref[h, 0].astype(BF16), q1, nt, preferred_element_type=F32)
        s2 = lax.dot_general(keys_ref[h, 1].astype(BF16), q2, nt, preferred_element_type=F32)
        return s1, s2

    def write(h, res):
        n1_ref[h], p1_ref[h] = res[:2]
        r2_ref[h] = res[2].astype(r2_ref.dtype)
        p2_ref[h] = res[3].astype(p2_ref.dtype)

    def head_pair(hp, carry):
        heads = (2 * hp, 2 * hp + 1)
        scr = (cand_a, cand_b)
        sc = [scores(h) for h in heads]
        res = [_head_select(*sc[n], cnt_ref[...], scr[n], False) for n in range(2)]
        for n in range(2):
            write(heads[n], res[n])
        for n in range(2):
            @pl.when(jnp.min(res[n][4]) < 0.5)
            def _():
                write(heads[n], _head_select(*sc[n], cnt_ref[...], scr[n], True))
        return carry

    lax.fori_loop(0, PEER_HEADS // 2, head_pair, 0)


def _pair_count_matrix():
    return jnp.array([[1.0 if i == r else 0.0 for (i, _) in _PAIRS]
                      + [0.0] * (_NPAIR_PAD - len(_PAIRS)) for r in range(TOPK)], BF16)


def _peer_topk(q, keys, tt):
    m = q.shape[0]
    cnt = _pair_count_matrix()
    out = pl.BlockSpec((PEER_HEADS, N_KEYS, tt), lambda i: (0, 0, i))
    return pl.pallas_call(
        _peer_topk_kernel,
        grid=(m // tt,),
        in_specs=[pl.BlockSpec((tt, q.shape[1]), lambda i: (i, 0)),
                  pl.BlockSpec(keys.shape, lambda i: (0, 0, 0, 0)),
                  pl.BlockSpec(cnt.shape, lambda i: (0, 0))],
        out_specs=[out] * 4,
        out_shape=[jax.ShapeDtypeStruct((PEER_HEADS, N_KEYS, m), dt)
                   for dt in (F32, F32, BF16, BF16)],
        scratch_shapes=[pltpu.VMEM((_NPAIR_PAD, tt), F32)] * 2,
        compiler_params=_cparams("arbitrary"),
        name="peer_topk",
    )(q, keys, cnt)


def _peer_dense_kernel(h_ref, u_ref, vt_ref, n1_ref, p1_ref, r2_ref, p2_ref, o_ref, w_scr, g_scr,
                       *, et, n_et):
    n = pl.program_id(0)
    cur = jnp.minimum(n, pl.num_programs(0) - 2)
    e_cur = cur % n_et
    e_prev = jnp.maximum(n - 1, 0) % n_et
    nt = (((1,), (1,)), ((), ()))

    @pl.when(n == 0)
    def _():
        w_scr[1] = jnp.zeros(w_scr.shape[1:], BF16)

    @pl.when(e_prev == 0)
    def _():
        o_ref[...] = jnp.zeros(o_ref.shape, F32)

    zero = jnp.zeros((), BF16)
    for j in range(et // N_KEYS):
        e1 = e_cur * (et // N_KEYS) + j
        gate = jnp.zeros((N_KEYS, g_scr.shape[1]), BF16)
        for h in range(PEER_HEADS):
            n1 = n1_ref[h, pl.ds(e1, 1), :].astype(BF16)
            p1 = p1_ref[h, pl.ds(e1, 1), :].astype(BF16)
            gate = gate + jnp.where(r2_ref[h] < n1, p2_ref[h], zero) * p1
        g_scr[j * N_KEYS:(j + 1) * N_KEYS, :] = gate
    s = lax.dot_general(u_ref[...], h_ref[...], nt, preferred_element_type=F32)
    o_ref[...] += jnp.dot(vt_ref[...], w_scr[(n + 1) % 2], preferred_element_type=F32)
    w_scr[n % 2] = _gelu(s).astype(BF16) * g_scr[...]


def _peer_dense(h2, u_tab, vt_tab, n1, p1, r2, p2, tt, et):
    m, d = h2.shape
    n_et = u_tab.shape[0] // et
    steps = (m // tt) * n_et

    def cur(fn):
        return lambda n: fn(jnp.minimum(n, steps - 1))

    def prev(fn):
        return lambda n: fn(jnp.maximum(n - 1, 0))

    once = pl.Buffered(1)
    sel = pl.BlockSpec((PEER_HEADS, N_KEYS, tt), cur(lambda c: (0, 0, c // n_et)),
                       pipeline_mode=once)
    return pl.pallas_call(
        functools.partial(_peer_dense_kernel, et=et, n_et=n_et),
        grid=(steps + 1,),
        in_specs=[pl.BlockSpec((tt, d), cur(lambda c: (c // n_et, 0))),
                  pl.BlockSpec((et, d), cur(lambda c: (c % n_et, 0))),
                  pl.BlockSpec((d, et), prev(lambda c: (0, c % n_et))),
                  sel, sel, sel, sel],
        out_specs=pl.BlockSpec((d, tt), prev(lambda c: (0, c // n_et))),
        out_shape=jax.ShapeDtypeStruct((d, m), F32),
        scratch_shapes=[pltpu.VMEM((2, et, tt), BF16), pltpu.VMEM((et, tt), BF16)],
        compiler_params=_cparams("arbitrary"),
        name="peer_dense",
    )(h2, u_tab, vt_tab, n1, p1, r2, p2)


def _final_kernel(x1_ref, ft_ref, gt_ref, g_ref, b_ref, o_ref, *, alpha):
    f = ft_ref[...].T
    o_ref[...] = _layer_norm_rows(alpha * x1_ref[...] + gt_ref[...] * f,
                                  g_ref[...], b_ref[...], LN_EPS)


def _final_ln(x1, ft, first_token, mod, ln_g, ln_b, per_row, seq_len, alpha, tm):
    m, d = x1.shape
    col0 = first_token // tm
    row = pl.BlockSpec((tm, d), lambda i: (i, 0))
    vec = pl.BlockSpec((1, d), lambda i: (0, 0))
    return pl.pallas_call(
        functools.partial(_final_kernel, alpha=alpha),
        grid=(m // tm,),
        in_specs=[row, pl.BlockSpec((d, tm), lambda i: (0, col0 + i)),
                  _mod_spec(per_row, tm, d, 5, seq_len), vec, vec],
        out_specs=row,
        out_shape=jax.ShapeDtypeStruct((m, d), F32),
        compiler_params=_cparams("arbitrary"),
        name="final_ln2",
    )(x1, ft, mod, ln_g, ln_b)


def _pack_w_in(w_in, d, d_b, heads):
    o_small = 3 * d
    o_u = o_small + SMALL
    o_v = o_u + d_b
    o_ga = o_v + d_b
    o_gb = o_ga + d
    wt = w_in.T
    small = jnp.pad(wt[o_small:o_u], ((0, SMALL_PAD - SMALL), (0, 0)))
    rkv = _chan_swap(wt[:o_small].reshape(3, d, -1), heads, axis=1).reshape(o_small, -1)
    return jnp.concatenate([rkv, wt[o_ga:o_gb], wt[o_gb:o_gb + d], wt[o_u:o_v], wt[o_v:o_ga],
                            small], axis=0).astype(BF16)


def _dense_token_tile(m):
    for tt in (768, 512, 256):
        if (-m) % tt <= m // 20:
            return tt
    return LANES


def _chan_swap(x, heads, axis=-1):
    axis = axis % x.ndim
    shp = x.shape
    x = x.reshape(shp[:axis] + (heads, HEAD_A) + shp[axis + 1:])
    return jnp.swapaxes(x, axis, axis + 1).reshape(shp)


def _chan_unswap(x, heads, axis=-1):
    axis = axis % x.ndim
    shp = x.shape
    x = x.reshape(shp[:axis] + (HEAD_A, heads) + shp[axis + 1:])
    return jnp.swapaxes(x, axis, axis + 1).reshape(shp)


def _split_shift(s, d, heads):
    main = _chan_swap(s[:, :3 * d].reshape(-1, 3, d), heads).reshape(-1, 3 * d)
    return main, jnp.pad(s[:, 3 * d:], ((0, 0), (0, SMALL_PAD - SMALL)))


def _join_shift(proj_rows, d, heads):
    small_off = 6 * d
    main = _chan_unswap(proj_rows[:, :3 * d].reshape(-1, 3, d), heads).reshape(-1, 3 * d)
    return jnp.concatenate([main, proj_rows[:, small_off:small_off + SMALL]], axis=1)


_SCAN_PARAMS = ("lnx_g", "lnx_b", "r_k", "k_k", "k_a")


def _rwkv_prompt(planes, bsz, heads, p, tc):
    tile = lambda x: jnp.tile(x.reshape(heads, HEAD_A).T, (1, bsz))[:, None, :]
    s0 = jnp.zeros((1, HEAD_A, HEAD_A, LANES), F32)
    y, st = _scan(planes, s0, [tile(p[n]) for n in _SCAN_PARAMS], tc)
    wkv = st[0].reshape(HEAD_A, HEAD_A, bsz, heads).transpose(2, 3, 1, 0)
    return y, wkv


def _rwkv_sample(pre, nb, heads, wkv0, p):
    to_l = lambda x: x.reshape(nb, HEAD_A, heads).transpose(1, 2, 0).reshape(HEAD_A, 1, heads * nb)
    rep = lambda x: jnp.repeat(x.reshape(heads, HEAD_A).T, nb, axis=1)[:, None, :]
    s0 = wkv0.transpose(1, 3, 2, 0)
    y, st = _scan([to_l(x) for x in pre], s0, [rep(p[n]) for n in _SCAN_PARAMS], 1)
    y = y.reshape(HEAD_A, heads, nb).transpose(2, 0, 1).reshape(nb, heads * HEAD_A)
    return y, st.transpose(3, 0, 2, 1)


def _layer(x_p, x_s, c_p, c_s, wkv_s, shift_s, p):
    bsz, t_len, d = x_p.shape
    nb = x_s.shape[0]
    heads = d // HEAD_A
    d_b = d // 2
    alpha = p["alpha"]
    mp = bsz * t_len

    assert bsz * heads == LANES and nb == LANES, "RWKV-7 recurrence fills 128 lanes per group"
    swap = functools.partial(_chan_swap, heads=heads)
    mu_main, mu_small = _split_shift(p["mu_shift"][None], d, heads)
    lp = {
        "mu_main": mu_main, "mu_small": mu_small,
        "w0": swap(p["w0"])[None], "w2": swap(p["w2"]), "a0": swap(p["a0"])[None],
        "a2": swap(p["a2"]), "g2": swap(p["g2"]),
        "sgu_ln_g": p["sgu_ln_g"][None], "sgu_ln_b": p["sgu_ln_b"][None], "sgu_w": p["sgu_w"],
        "sgu_b_full": jnp.broadcast_to(p["sgu_b"][:, :, None], p["sgu_b"].shape + (GROUP_B,)),
        "sgu_w_first": jnp.repeat(p["sgu_w"][:, 0, 0], GROUP_B)[None],
        "sgu_b_first": jnp.repeat(p["sgu_b"][:, 0], GROUP_B)[None],
    }
    ln1 = (p["ln1_g"][None], p["ln1_b"][None])
    ln2 = (p["ln2_g"][None], p["ln2_b"][None])

    c_all = jnp.concatenate([c_p, c_s], axis=0)
    mod = _modulation(jnp.pad(c_all, ((0, (-c_all.shape[0]) % SUBLANES), (0, 0))),
                      p["w_ada"], p["b_ada"])
    mod_p = mod[:bsz].reshape(bsz, 1, 6 * d)
    mod_s = mod[bsz:bsz + nb]

    w_pack = _pack_w_in(p["w_in"], d, d_b, heads)
    n_pack = w_pack.shape[0]
    tn_in = _pick(n_pack, (1280, 512))
    xp2 = x_p.reshape(mp, d)
    xs2 = x_s.reshape(nb, d)
    tm_big = _pick(t_len, (1024, 512, 256, 128))
    tm_mid = _pick(t_len, (512, 256, 128))
    tm_small = _pick(t_len, (256, 128))
    proj_p = _inproj(xp2, mod_p, w_pack, False, t_len, tm_big, tn_in)
    proj_s = _inproj(xs2, mod_s, w_pack, True, 1, nb, tn_in)

    zm = jnp.zeros((bsz, 1, 3 * d), F32)
    zs = jnp.zeros((bsz, 1, SMALL_PAD), F32)
    *planes_p, g_p = _rwkv_pre(proj_p.reshape(bsz, t_len, n_pack), zm, zs, lp, d, heads, False,
                               _pick(t_len, (32, 16, 8)))
    sm, ss = _split_shift(shift_s, d, heads)
    *pre_s, g_s = _rwkv_pre(proj_s[None], sm[None], ss[None], lp, d, heads, True, nb)
    y_planes, wkv_p = _rwkv_prompt(planes_p, bsz, heads, p, _pick(t_len, (64, 32, 16, 8)))
    ya_p = _from_planes(y_planes, bsz, heads, _pick(t_len, (128, 64, 32, 16, 8))).reshape(mp, d)
    g_p = g_p.reshape(mp, d)
    ya_s, wkv_s_new = _rwkv_sample([x[0] for x in pre_s], nb, heads, wkv_s, p)
    g_s = g_s[0]

    u_blk = (5 * d) // d_b
    yb_p = _sgu_prompt(proj_p, lp, d_b, u_blk, tm_mid)
    yb_s, vrows_s = _sgu_single(proj_s, lp, d_b, u_blk)

    wa = _chan_swap(p["w_br_a"], heads, axis=0).astype(BF16)
    wb = p["w_br_b"].astype(BF16)
    wo = p["w_o"].astype(BF16)
    wq = p["peer_wq"].astype(BF16)
    tn_mrg = _pick(d, (1024, 512, 256, 128))
    mrg_p = _merge(ya_p, g_p, yb_p, proj_p, wa, wb, d, tm_mid, tn_mrg)
    mrg_s = _merge(ya_s, g_s, yb_s, proj_s, wa, wb, d, nb, tn_mrg)
    x1_p, h2_p, q_p = _out_ln(mrg_p, xp2, mod_p, wo, wq, *ln1, False, t_len, alpha, tm_small)
    x1_s, h2_s, q_s = _out_ln(mrg_s, xs2, mod_s, wo, wq, *ln1, True, 1, alpha, nb)

    m_all = mp + nb
    tt = _dense_token_tile(m_all)
    pad_rows = (-m_all) % tt
    h2 = jnp.concatenate([h2_p, h2_s, h2_p[:pad_rows]], axis=0)
    q = jnp.concatenate([q_p, q_s, q_p[:pad_rows]], axis=0)
    n1, p1, r2, p2 = _peer_topk(q, p["peer_keys"], 2 * LANES if tt % (2 * LANES) == 0 else LANES)
    ft = _peer_dense(h2, p["peer_u"].astype(BF16), p["peer_v"].T.astype(BF16), n1, p1, r2, p2,
                     tt, 8 * N_KEYS)

    y_p = _final_ln(x1_p, ft, 0, mod_p, *ln2, False, t_len, alpha, tm_small)
    y_s = _final_ln(x1_s, ft, mp, mod_s, *ln2, True, 1, alpha, nb)

    last = proj_p.reshape(bsz, t_len, -1)[:, -1]
    return (y_p.reshape(bsz, t_len, d), y_s.reshape(nb, 1, d), wkv_p, _join_shift(last, d, heads),
            wkv_s_new, _join_shift(proj_s, d, heads), vrows_s.reshape(nb, 1, d_b))


def kernel(x_prompt, x_sample, c_prompt, c_sample, state_wkv, state_shift, w_ada, b_ada, w_in, mu_shift, w0, w2, a0, a2, g2, k_k, k_a, r_k, lnx_g, lnx_b, sgu_ln_g, sgu_ln_b, sgu_w, sgu_b, w_br_a, w_br_b, w_o, ln1_g, ln1_b, peer_wq, peer_keys, peer_u, peer_v, ln2_g, ln2_b):
    names = ("w_ada", "b_ada", "w_in", "mu_shift", "w0", "w2", "a0", "a2", "g2", "k_k", "k_a",
             "r_k", "lnx_g", "lnx_b", "sgu_ln_g", "sgu_ln_b", "sgu_w", "sgu_b", "w_br_a",
             "w_br_b", "w_o", "ln1_g", "ln1_b", "peer_wq", "peer_keys", "peer_u", "peer_v",
             "ln2_g", "ln2_b")
    stacked = (w_ada, b_ada, w_in, mu_shift, w0, w2, a0, a2, g2, k_k, k_a, r_k, lnx_g, lnx_b,
               sgu_ln_g, sgu_ln_b, sgu_w, sgu_b, w_br_a, w_br_b, w_o, ln1_g, ln1_b, peer_wq,
               peer_keys, peer_u, peer_v, ln2_g, ln2_b)
    depth = w_ada.shape[0]
    alpha = (2 * depth) ** 0.25
    y_p, y_s = x_prompt, x_sample
    outs = [[] for _ in range(5)]
    for l in range(depth):
        p = {n: a[l] for n, a in zip(names, stacked)}
        p["alpha"] = alpha
        y_p, y_s, *state = _layer(y_p, y_s, c_prompt, c_sample, state_wkv[l], state_shift[l], p)
        for acc, s in zip(outs, state):
            acc.append(s)
    return (y_p, y_s) + tuple(jnp.stack(o) for o in outs)
```
